```python
import jax, jax.numpy as jnp
from jax import lax
import numpy as np

D_MODEL = 1024
BATCH = 4
SEQ = 4096
DEPTH = 2
DEC_BATCH = 32
DEC_SEQ = 8
PAST_LEN = 8192
PAGE_SIZE = 128

MIX_W = D_MODEL
ATT_W = MIX_W // 2
REC_W = MIX_W - ATT_W
HEAD_DIM = 64
H_ATT = ATT_W // HEAD_DIM
Q_BLOCK = 128
N_REC_BLOCKS = 8
REC_BLOCK = REC_W // N_REC_BLOCKS
CONV_W = 4
RG_C = 8.0
N_EXPERTS = 64
TOP_K = 8
N_GROUPS = 8
TOPK_GROUPS = 4
D_EXPERT = D_MODEL // 4
D_SHARED = D_EXPERT
ROUTED_SCALE = 2.5
IN_W = 3 * ATT_W + H_ATT + 2 * REC_W
ALPHA = (2 * DEPTH) ** 0.25
BETA = (8 * DEPTH) ** -0.25
EPS = 1e-5

kernel_name = "hymba_fox_rglru_moe_deepnorm_step"


def layer_norm(x, g, b):
    xf = x.astype(jnp.float32)
    mu = jnp.mean(xf, axis=-1, keepdims=True)
    xc = xf - mu
    var = jnp.mean(xc * xc, axis=-1, keepdims=True)
    return (xc * lax.rsqrt(var + EPS) * g.astype(jnp.float32) + b.astype(jnp.float32)).astype(x.dtype)


def rms_norm(x, g):
    xf = x.astype(jnp.float32)
    y = xf * lax.rsqrt(jnp.mean(xf * xf, axis=-1, keepdims=True) + EPS)
    return (y * g.astype(jnp.float32)).astype(x.dtype)


def split_in(u):
    cuts = [ATT_W, 2 * ATT_W, 3 * ATT_W, 3 * ATT_W + H_ATT, 3 * ATT_W + H_ATT + REC_W]
    return jnp.split(u, cuts, axis=-1)


def heads(t):
    return t.reshape(t.shape[0], t.shape[1], H_ATT, HEAD_DIM)


def fox_prompt(q, k, v, logf):
    B, S, H, Dh = q.shape
    nblk = S // Q_BLOCK
    scale = HEAD_DIM ** -0.5
    Ft = jnp.cumsum(logf, axis=1).transpose(0, 2, 1)
    q_blocks = q.reshape(B, nblk, Q_BLOCK, H, Dh).transpose(1, 0, 2, 3, 4)
    Fq_blocks = Ft.reshape(B, H, nblk, Q_BLOCK).transpose(2, 0, 1, 3)
    key_pos = jnp.arange(S)

    def block(args):
        i, qi, Fqi = args
        s = jnp.einsum('bqhd,bkhd->bhqk', qi, k, preferred_element_type=jnp.float32) * scale
        s = s + (Fqi[..., :, None] - Ft[:, :, None, :])
        q_pos = i * Q_BLOCK + jnp.arange(Q_BLOCK)
        s = jnp.where(key_pos[None, :] <= q_pos[:, None], s, -jnp.inf)
        p = jax.nn.softmax(s, axis=-1)
        return jnp.einsum('bhqk,bkhd->bqhd', p.astype(v.dtype), v)

    o = lax.map(block, (jnp.arange(nblk), q_blocks, Fq_blocks))
    return o.transpose(1, 0, 2, 3, 4).reshape(B, S, H * Dh)


def fox_sample(q, k_new, v_new, logf_new, k_past, v_past, logf_past):
    Bd, T, H, Dh = q.shape
    P = k_past.shape[1]
    scale = HEAD_DIM ** -0.5
    k_all = jnp.concatenate([k_past.astype(k_new.dtype), k_new], axis=1)
    v_all = jnp.concatenate([v_past.astype(v_new.dtype), v_new], axis=1)
    logf_all = jnp.concatenate([logf_past.astype(jnp.float32), logf_new], axis=1)
    Ft = jnp.cumsum(logf_all, axis=1).transpose(0, 2, 1)
    s = jnp.einsum('bqhd,bkhd->bhqk', q, k_all, preferred_element_type=jnp.float32) * scale
    s = s + (Ft[:, :, P:, None] - Ft[:, :, None, :])
    mask = jnp.arange(P + T)[None, :] <= (P + jnp.arange(T))[:, None]
    s = jnp.where(mask, s, -jnp.inf)
    p = jax.nn.softmax(s, axis=-1)
    o = jnp.einsum('bhqk,bkhd->bqhd', p.astype(v_all.dtype), v_all)
    return o.reshape(Bd, T, H * Dh)


def gather_pages(pool, page_table):
    g = pool[page_table]
    return g.reshape(g.shape[0], g.shape[1] * g.shape[2], *pool.shape[2:])


def causal_conv(xr, buf, conv_w, conv_b):
    S = xr.shape[1]
    xp = jnp.concatenate([buf.astype(xr.dtype), xr], axis=1)
    out = conv_b
    for i in range(CONV_W):
        out = out + xp[:, i:i + S] * conv_w[i]
    return out, xp[:, S:]


def lin_scan(a, b):
    def comb(l, r):
        return (l[0] * r[0], r[0] * l[1] + r[1])
    _, h = lax.associative_scan(comb, (a, b), axis=1)
    return h


def rglru_branch(xr, gr, conv_buf, h0, conv_w, conv_b, w_gr, b_gr, w_gi, b_gi, lam):
    xc, new_buf = causal_conv(xr, conv_buf, conv_w, conv_b)
    B, S, _ = xc.shape
    xb = xc.reshape(B, S, N_REC_BLOCKS, REC_BLOCK)
    r = jax.nn.sigmoid((jnp.einsum('bsnk,nkj->bsnj', xb, w_gr).reshape(B, S, REC_W) + b_gr).astype(jnp.float32))
    ig = jax.nn.sigmoid((jnp.einsum('bsnk,nkj->bsnj', xb, w_gi).reshape(B, S, REC_W) + b_gi).astype(jnp.float32))
    log_a = -RG_C * jax.nn.softplus(-lam.astype(jnp.float32)) * r
    a = jnp.exp(log_a)
    bterm = jnp.sqrt(-jnp.expm1(2.0 * log_a)) * (ig * xc.astype(jnp.float32))
    bterm = bterm.at[:, 0].add(a[:, 0] * h0.astype(jnp.float32))
    h = lin_scan(a, bterm)
    y = h.astype(xr.dtype) * jax.nn.gelu(gr)
    return y, new_buf, h[:, -1].astype(xr.dtype)


def moe(x, w_router, router_bias, w1, w3, w2, ws1, ws3, ws2):
    shp = x.shape
    xt = x.reshape(-1, D_MODEL)
    T = xt.shape[0]
    scores = jax.nn.sigmoid(jnp.dot(xt, w_router, preferred_element_type=jnp.float32))
    biased = scores + router_bias.astype(jnp.float32)
    gsz = N_EXPERTS // N_GROUPS
    grp_score = lax.top_k(biased.reshape(T, N_GROUPS, gsz), 2)[0].sum(-1)
    _, gidx = lax.top_k(grp_score, TOPK_GROUPS)
    gmask = jnp.any(gidx[:, :, None] == jnp.arange(N_GROUPS)[None, None, :], axis=1)
    emask = jnp.repeat(gmask, gsz, axis=1)
    _, idx = lax.top_k(jnp.where(emask, biased, -jnp.inf), TOP_K)
    gate = jnp.take_along_axis(scores, idx, axis=-1)
    gate = gate / jnp.sum(gate, axis=-1, keepdims=True) * ROUTED_SCALE
    flat_e = idx.reshape(-1)
    order = jnp.argsort(flat_e)
    tok = order // TOP_K
    sizes = jnp.bincount(flat_e, length=N_EXPERTS).astype(jnp.int32)
    xs = xt[tok]
    h = jax.nn.silu(lax.ragged_dot(xs, w1, sizes)) * lax.ragged_dot(xs, w3, sizes)
    ys = lax.ragged_dot(h, w2, sizes) * gate.reshape(-1)[order][:, None].astype(xt.dtype)
    routed = jax.ops.segment_sum(ys, tok, num_segments=T)
    shared = (jax.nn.silu(xt @ ws1) * (xt @ ws3)) @ ws2
    return (routed + shared).reshape(shp)


def setup_inputs(seed: int = 0) -> dict:
    key = jax.random.key(seed)
    ks = jax.random.split(key, 40)
    f32 = jnp.float32
    L = DEPTH
    n_pages = PAST_LEN // PAGE_SIZE
    n_used = DEC_BATCH * n_pages
    n_pool = n_used + (n_used + 3) // 4

    def nrm(k, shape, s):
        return jax.random.normal(k, shape, f32) * s

    page_table = jax.random.permutation(ks[7], n_pool)[:n_used].reshape(DEC_BATCH, n_pages).astype(jnp.int32)
    a_c = jax.random.uniform(ks[14], (L, REC_W), f32, minval=0.9, maxval=0.999)
    sig = a_c ** (1.0 / RG_C)
    lam = jnp.log(sig) - jnp.log1p(-sig)
    return {
        "x_prompt": nrm(ks[0], (BATCH, SEQ, D_MODEL), 1.0),
        "x_sample": nrm(ks[1], (DEC_BATCH, DEC_SEQ, D_MODEL), 1.0),
        "cache_k": nrm(ks[2], (L, n_pool, PAGE_SIZE, H_ATT, HEAD_DIM), 1.0),
        "cache_v": nrm(ks[3], (L, n_pool, PAGE_SIZE, H_ATT, HEAD_DIM), 1.0),
        "cache_logf": jax.nn.log_sigmoid(nrm(ks[4], (L, n_pool, PAGE_SIZE, H_ATT), 1.0) + 3.0),
        "state_h": nrm(ks[5], (L, DEC_BATCH, REC_W), 0.5),
        "state_conv": nrm(ks[6], (L, DEC_BATCH, CONV_W - 1, REC_W), 1.0),
        "page_table": page_table,
        "w_in": nrm(ks[8], (L, D_MODEL, IN_W), D_MODEL ** -0.5),
        "b_f": jax.random.uniform(ks[9], (L, H_ATT), f32, minval=1.0, maxval=4.0),
        "conv_w": nrm(ks[10], (L, CONV_W, REC_W), CONV_W ** -0.5),
        "conv_b": nrm(ks[11], (L, REC_W), 0.01),
        "w_gr": nrm(ks[12], (L, N_REC_BLOCKS, REC_BLOCK, REC_BLOCK), REC_BLOCK ** -0.5),
        "b_gr": nrm(ks[13], (L, REC_W), 0.01),
        "w_gi": nrm(ks[15], (L, N_REC_BLOCKS, REC_BLOCK, REC_BLOCK), REC_BLOCK ** -0.5),
        "b_gi": nrm(ks[16], (L, REC_W), 0.01),
        "lam": lam,
        "g_att": 1.0 + nrm(ks[17], (L, ATT_W), 0.01),
        "g_rec": 1.0 + nrm(ks[18], (L, REC_W), 0.01),
        "w_out": nrm(ks[19], (L, MIX_W, D_MODEL), BETA * MIX_W ** -0.5),
        "ln1_g": 1.0 + nrm(ks[20], (L, D_MODEL), 0.01),
        "ln1_b": nrm(ks[21], (L, D_MODEL), 0.01),
        "w_router": nrm(ks[22], (L, D_MODEL, N_EXPERTS), D_MODEL ** -0.5),
        "router_bias": nrm(ks[23], (L, N_EXPERTS), 0.01),
        "w1": nrm(ks[24], (L, N_EXPERTS, D_MODEL, D_EXPERT), D_MODEL ** -0.5),
        "w3": nrm(ks[25], (L, N_EXPERTS, D_MODEL, D_EXPERT), D_MODEL ** -0.5),
        "w2": nrm(ks[26], (L, N_EXPERTS, D_EXPERT, D_MODEL), BETA * D_EXPERT ** -0.5),
        "ws1": nrm(ks[27], (L, D_MODEL, D_SHARED), D_MODEL ** -0.5),
        "ws3": nrm(ks[28], (L, D_MODEL, D_SHARED), D_MODEL ** -0.5),
        "ws2": nrm(ks[29], (L, D_SHARED, D_MODEL), BETA * D_SHARED ** -0.5),
        "ln2_g": 1.0 + nrm(ks[30], (L, D_MODEL), 0.01),
        "ln2_b": nrm(ks[31], (L, D_MODEL), 0.01),
    }


def reference(x_prompt, x_sample, cache_k, cache_v, cache_logf, state_h, state_conv, page_table,
              w_in, b_f, conv_w, conv_b, w_gr, b_gr, w_gi, b_gi, lam, g_att, g_rec, w_out,
              ln1_g, ln1_b, w_router, router_bias, w1, w3, w2, ws1, ws3, ws2, ln2_g, ln2_b):
    xp, xs = x_prompt, x_sample
    Bp = xp.shape[0]
    kp_l, vp_l, fp_l, hp_l, cp_l = [], [], [], [], []
    ks_l, vs_l, fs_l, hs_l, cs_l = [], [], [], [], []
    for l in range(DEPTH):
        rec_params = (conv_w[l], conv_b[l], w_gr[l], b_gr[l], w_gi[l], b_gi[l], lam[l])
        qp, kp, vp, flp, xrp, grp = split_in(xp @ w_in[l])
        logfp = jax.nn.log_sigmoid(flp.astype(jnp.float32) + b_f[l].astype(jnp.float32))
        kph, vph = heads(kp), heads(vp)
        att_p = fox_prompt(heads(qp), kph, vph, logfp)
        rec_p, conv_p, h_p = rglru_branch(
            xrp, grp, jnp.zeros((Bp, CONV_W - 1, REC_W), xp.dtype), jnp.zeros((Bp, REC_W), xp.dtype), *rec_params)
        mix_p = jnp.concatenate([rms_norm(att_p, g_att[l]), rms_norm(rec_p, g_rec[l])], axis=-1) @ w_out[l]
        xp = layer_norm(ALPHA * xp + mix_p, ln1_g[l], ln1_b[l])
        xp = layer_norm(ALPHA * xp + moe(xp, w_router[l], router_bias[l], w1[l], w3[l], w2[l],
                                         ws1[l], ws3[l], ws2[l]), ln2_g[l], ln2_b[l])
        kp_l.append(kph); vp_l.append(vph); fp_l.append(logfp); hp_l.append(h_p); cp_l.append(conv_p)
        qs, kn, vn, fln, xrs, grs = split_in(xs @ w_in[l])
        logfs = jax.nn.log_sigmoid(fln.astype(jnp.float32) + b_f[l].astype(jnp.float32))
        ksh, vsh = heads(kn), heads(vn)
        att_s = fox_sample(heads(qs), ksh, vsh, logfs,
                           gather_pages(cache_k[l], page_table),
                           gather_pages(cache_v[l], page_table),
                           gather_pages(cache_logf[l], page_table))
        rec_s, conv_s, h_s = rglru_branch(xrs, grs, state_conv[l], state_h[l], *rec_params)
        mix_s = jnp.concatenate([rms_norm(att_s, g_att[l]), rms_norm(rec_s, g_rec[l])], axis=-1) @ w_out[l]
        xs = layer_norm(ALPHA * xs + mix_s, ln1_g[l], ln1_b[l])
        xs = layer_norm(ALPHA * xs + moe(xs, w_router[l], router_bias[l], w1[l], w3[l], w2[l],
                                         ws1[l], ws3[l], ws2[l]), ln2_g[l], ln2_b[l])
        ks_l.append(ksh); vs_l.append(vsh); fs_l.append(logfs); hs_l.append(h_s); cs_l.append(conv_s)
    return (xp, xs,
            jnp.stack(kp_l), jnp.stack(vp_l), jnp.stack(fp_l), jnp.stack(hp_l), jnp.stack(cp_l),
            jnp.stack(ks_l), jnp.stack(vs_l), jnp.stack(fs_l), jnp.stack(hs_l), jnp.stack(cs_l))
```

```python
import functools

import jax
import jax.numpy as jnp
from jax import lax
from jax.experimental import pallas as pl
from jax.experimental.pallas import tpu as pltpu

F32 = jnp.float32
BF16 = jnp.bfloat16

D_MODEL = 1024
BATCH = 4
SEQ = 4096
DEPTH = 2
DEC_BATCH = 32
DEC_SEQ = 8
PAST_LEN = 8192
PAGE_SIZE = 128
ATT_W = 512
REC_W = 512
HEAD_DIM = 64
H_ATT = 8
N_REC_BLOCKS = 8
CONV_W = 4
RG_C = 8.0
N_EXPERTS = 64
TOP_K = 8
N_GROUPS = 8
TOPK_GROUPS = 4
D_EXPERT = 256
ROUTED_SCALE = 2.5
ALPHA = (2 * DEPTH) ** 0.25
EPS = 1e-5
ATT_SCALE = HEAD_DIM ** -0.5

N_PROMPT = BATCH * SEQ
N_SAMPLE = DEC_BATCH * DEC_SEQ
N_TOK = N_PROMPT + N_SAMPLE
N_PAGES = PAST_LEN // PAGE_SIZE
U_W = 3 * ATT_W + 2 * REC_W
LANES = 128
NEG = -1e30

TM_PROJ = 1280
TN_PROJ = 640
TM_OUT = 640
TM_MOE = 1280
TQ = 512
TC_REC = 512
PAGES_PER_STEP = 8
VMEM_LIMIT = 56 * 1024 * 1024


def _cparams(sem, vmem=None):
    return pltpu.CompilerParams(dimension_semantics=sem, vmem_limit_bytes=vmem)


def _mm_kernel(x_ref, w_ref, o_ref):
    o_ref[...] = jnp.dot(x_ref[...].astype(BF16), w_ref[...], preferred_element_type=F32)


def _matmul(x, w, tm, tn):
    m, k = x.shape
    n = w.shape[1]
    return pl.pallas_call(
        _mm_kernel,
        grid=(m // tm, n // tn),
        in_specs=[pl.BlockSpec((tm, k), lambda i, j: (i, 0)),
                  pl.BlockSpec((k, tn), lambda i, j: (0, j))],
        out_specs=pl.BlockSpec((tm, tn), lambda i, j: (i, j)),
        out_shape=jax.ShapeDtypeStruct((m, n), F32),
        compiler_params=_cparams(("parallel", "arbitrary"), VMEM_LIMIT),
        name="in_proj",
    )(x, w)


def _forget_kernel(x_ref, w_ref, b_ref, o_ref):
    z = jnp.dot(x_ref[...].astype(BF16), w_ref[...], preferred_element_type=F32) + b_ref[...]
    o_ref[...] = jnp.minimum(z, 0.0) - jnp.log1p(jnp.exp(-jnp.abs(z)))


def _forget_logits(x, w, b, tm):
    m, k = x.shape
    n = w.shape[1]
    return pl.pallas_call(
        _forget_kernel,
        grid=(m // tm,),
        in_specs=[pl.BlockSpec((tm, k), lambda i: (i, 0)),
                  pl.BlockSpec((k, n), lambda i: (0, 0)),
                  pl.BlockSpec((1, n), lambda i: (0, 0))],
        out_specs=pl.BlockSpec((tm, n), lambda i: (i, 0)),
        out_shape=jax.ShapeDtypeStruct((m, n), F32),
        compiler_params=_cparams(("parallel",), VMEM_LIMIT),
        name="forget_logits",
    )(x, w, b)


def _split3(x):
    hi = x.astype(BF16)
    r1 = x - hi.astype(F32)
    mid = r1.astype(BF16)
    lo = (r1 - mid.astype(F32)).astype(BF16)
    return hi, mid, lo


def _cumsum_kernel(x_ref, o_ref, *, nchunk):
    rows = x_ref.shape[0]
    r = lax.broadcasted_iota(jnp.int32, (LANES, LANES), 0)
    c = lax.broadcasted_iota(jnp.int32, (LANES, LANES), 1)
    upper = jnp.where(r <= c, 1.0, 0.0).astype(BF16)
    carry = jnp.zeros((rows, 1), F32)
    for ch in range(nchunk):
        x = x_ref[:, ch * LANES:(ch + 1) * LANES]
        hi, mid, lo = _split3(x)
        w = (jnp.dot(hi, upper, preferred_element_type=F32)
             + jnp.dot(mid, upper, preferred_element_type=F32)
             + jnp.dot(lo, upper, preferred_element_type=F32)) + carry
        o_ref[:, ch * LANES:(ch + 1) * LANES] = w
        carry = w[:, LANES - 1:LANES]


def _cumsum_lanes(x, tr):
    rows, s = x.shape
    return pl.pallas_call(
        functools.partial(_cumsum_kernel, nchunk=s // LANES),
        grid=(rows // tr,),
        in_specs=[pl.BlockSpec((tr, s), lambda i: (i, 0))],
        out_specs=pl.BlockSpec((tr, s), lambda i: (i, 0)),
        out_shape=jax.ShapeDtypeStruct((rows, s), F32),
        compiler_params=_cparams(("parallel",)),
        name="cumsum_lanes",
    )(x)


def _attn_prompt_kernel(q_ref, k_ref, v_ref, fc_ref, fr_ref, o_ref, s_sc):
    qb = pl.program_id(2)
    lane = lax.broadcasted_iota(jnp.int32, (TQ, LANES), 1)
    row = lax.broadcasted_iota(jnp.int32, (TQ, TQ), 0)
    col = lax.broadcasted_iota(jnp.int32, (TQ, TQ), 1)
    q = q_ref[...] * ATT_SCALE
    fc = fc_ref[0, 0]
    outs = []
    for h in range(2):
        in_head = (lane < HEAD_DIM) if h == 0 else (lane >= HEAD_DIM)
        qm = jnp.where(in_head, q, 0.0).astype(BF16)
        fq = fc[:, h:h + 1]

        def scores(kb, carry, masked):
            m, l = carry
            k0 = pl.multiple_of(kb * TQ, TQ)
            k = k_ref[pl.ds(k0, TQ), :].astype(BF16)
            fk = fr_ref[0, 0, h:h + 1, pl.ds(k0, TQ)]
            s = lax.dot_general(qm, k, (((1,), (1,)), ((), ())), preferred_element_type=F32)
            s = s + (fq - fk)
            if masked:
                s = jnp.where(col <= row, s, NEG)
            s_sc[kb] = s
            m_new = jnp.maximum(m, jnp.max(s, axis=1, keepdims=True))
            l = jnp.exp(m - m_new) * l + jnp.sum(jnp.exp(s - m_new), axis=1, keepdims=True)
            return m_new, l

        init = (jnp.full((TQ, 1), NEG, F32), jnp.zeros((TQ, 1), F32))
        carry = lax.fori_loop(0, qb, lambda kb, c: scores(kb, c, False), init)
        m, l = scores(qb, carry, True)

        def values(kb, acc):
            k0 = pl.multiple_of(kb * TQ, TQ)
            v = v_ref[pl.ds(k0, TQ), :].astype(BF16)
            p = jnp.exp(s_sc[kb] - m) / l
            return acc + jnp.dot(p.astype(BF16), v, preferred_element_type=F32)

        outs.append(lax.fori_loop(0, qb + 1, values, jnp.zeros((TQ, LANES), F32)))
    o_ref[...] = jnp.where(lane < HEAD_DIM, outs[0], outs[1])


def _attn_prompt(u, fc, fr):
    nq = SEQ // TQ
    kblk = ATT_W // LANES
    return pl.pallas_call(
        _attn_prompt_kernel,
        grid=(BATCH, H_ATT // 2, nq),
        in_specs=[
            pl.BlockSpec((TQ, LANES), lambda b, p, i: (b * nq + i, p)),
            pl.BlockSpec((SEQ, LANES), lambda b, p, i: (b, kblk + p)),
            pl.BlockSpec((SEQ, LANES), lambda b, p, i: (b, 2 * kblk + p)),
            pl.BlockSpec((1, 1, TQ, 2), lambda b, p, i: (b, p, i, 0)),
            pl.BlockSpec((1, 1, 2, SEQ), lambda b, p, i: (b, p, 0, 0)),
        ],
        out_specs=pl.BlockSpec((TQ, LANES), lambda b, p, i: (b * nq + i, p)),
        out_shape=jax.ShapeDtypeStruct((N_PROMPT, ATT_W), F32),
        scratch_shapes=[pltpu.VMEM((nq, TQ, TQ), F32)],
        compiler_params=_cparams(("parallel", "parallel", "arbitrary"), VMEM_LIMIT),
        name="attn_prompt",
    )(u, u, u, fc, fr)


def _expand_heads(x):
    n = x.shape[1]
    return jnp.concatenate(
        [jnp.broadcast_to(x[h:h + 1, :], (DEC_SEQ, n)) for h in range(H_ATT)], axis=0)


def _attn_sample_kernel(pt_ref, q_ref, kn_ref, vn_ref, fq_ref, fn_ref, *rest):
    del pt_ref
    g = PAGES_PER_STEP
    nstep = N_PAGES // g
    k_refs, v_refs, w_refs = rest[:g], rest[g:2 * g], rest[2 * g:3 * g]
    o_ref = rest[3 * g]
    qbd_sc, s_sc, m_sc, l_sc, acc_sc, suf_sc = rest[3 * g + 1:]
    step = pl.program_id(1)
    nrow = H_ATT * DEC_SEQ
    row = lax.broadcasted_iota(jnp.int32, (nrow, ATT_W), 0)
    colw = lax.broadcasted_iota(jnp.int32, (nrow, ATT_W), 1)
    own_head = (colw >> 6) == (row >> 3)
    fq = fq_ref[0]
    pad = jnp.zeros((PAGE_SIZE - DEC_SEQ, ATT_W), F32)

    def track(s):
        m = m_sc[...]
        m_new = jnp.maximum(m, jnp.max(s, axis=1, keepdims=True))
        l_sc[...] = jnp.exp(m - m_new) * l_sc[...] + jnp.sum(jnp.exp(s - m_new), axis=1, keepdims=True)
        m_sc[...] = m_new

    @pl.when(step == 0)
    def _():
        q = q_ref[...] * ATT_SCALE
        qt = jnp.concatenate([q] * H_ATT, axis=0)
        qbd_sc[...] = jnp.where(own_head, qt, 0.0).astype(BF16)
        m_sc[...] = jnp.full((nrow, 1), NEG, F32)
        l_sc[...] = jnp.zeros((nrow, 1), F32)
        suf_sc[...] = jnp.zeros((H_ATT, 1), F32)
        kn = jnp.concatenate([kn_ref[...], pad], axis=0).astype(BF16)
        s = lax.dot_general(qbd_sc[...], kn, (((1,), (1,)), ((), ())), preferred_element_type=F32)
        s = s + (fq - _expand_heads(fn_ref[0]))
        r = lax.broadcasted_iota(jnp.int32, (nrow, PAGE_SIZE), 0)
        c = lax.broadcasted_iota(jnp.int32, (nrow, PAGE_SIZE), 1)
        s = jnp.where(c <= (r & (DEC_SEQ - 1)), s, NEG)
        s_sc[N_PAGES] = s
        track(s)

    @pl.when(step < nstep)
    def _():
        first = (nstep - 1 - step) * g
        qbd = qbd_sc[...]
        for i in range(g - 1, -1, -1):
            w = w_refs[i][0, 0]
            tot = w[:, PAGE_SIZE - 1:PAGE_SIZE]
            suf = suf_sc[...] + tot
            k = k_refs[i][0, 0].astype(BF16)
            s = lax.dot_general(qbd, k, (((1,), (1,)), ((), ())), preferred_element_type=F32)
            s = s + (fq + _expand_heads(suf - w))
            s_sc[first + i] = s
            track(s)
            suf_sc[...] = suf

    @pl.when(step >= nstep)
    def _():
        first = (2 * nstep - 1 - step) * g
        m = m_sc[...]
        l = l_sc[...]

        def probs(s):
            return (jnp.exp(s - m) / l).astype(BF16)

        @pl.when(step == nstep)
        def _():
            vn = jnp.concatenate([vn_ref[...], pad], axis=0).astype(BF16)
            acc_sc[...] = jnp.dot(probs(s_sc[N_PAGES]), vn, preferred_element_type=F32)

        for i in range(g - 1, -1, -1):
            v = v_refs[i][0, 0].astype(BF16)
            acc_sc[...] += jnp.dot(probs(s_sc[first + i]), v, preferred_element_type=F32)

    @pl.when(step == 2 * nstep - 1)
    def _():
        o = jnp.where(own_head, acc_sc[...], 0.0)
        out = o[0:DEC_SEQ, :]
        for h in range(1, H_ATT):
            out = out + o[h * DEC_SEQ:(h + 1) * DEC_SEQ, :]
        o_ref[...] = out


def _attn_sample(layer, page_table, u, cache_k, cache_v, wt, fq, fn):
    g = PAGES_PER_STEP
    nstep = N_PAGES // g
    srow = N_PROMPT // DEC_SEQ
    nrow = H_ATT * DEC_SEQ

    def key_map(i):
        return lambda b, s, pt: (layer, pt[b, (nstep - 1 - jnp.minimum(s, nstep - 1)) * g + i], 0, 0)

    def value_map(i):
        return lambda b, s, pt: (layer, pt[b, (nstep - 1 - jnp.maximum(s - nstep, 0)) * g + i], 0, 0)

    in_specs = [
        pl.BlockSpec((DEC_SEQ, ATT_W), lambda b, s, pt: (srow + b, 0)),
        pl.BlockSpec((DEC_SEQ, ATT_W), lambda b, s, pt: (srow + b, 1)),
        pl.BlockSpec((DEC_SEQ, ATT_W), lambda b, s, pt: (srow + b, 2)),
        pl.BlockSpec((1, nrow, 1), lambda b, s, pt: (b, 0, 0)),
        pl.BlockSpec((1, H_ATT, LANES), lambda b, s, pt: (b, 0, 0)),
    ]
    in_specs += [pl.BlockSpec((1, 1, PAGE_SIZE, ATT_W), key_map(i)) for i in range(g)]
    in_specs += [pl.BlockSpec((1, 1, PAGE_SIZE, ATT_W), value_map(i)) for i in range(g)]
    in_specs += [pl.BlockSpec((1, 1, H_ATT, PAGE_SIZE), key_map(i)) for i in range(g)]
    grid_spec = pltpu.PrefetchScalarGridSpec(
        num_scalar_prefetch=1,
        grid=(DEC_BATCH, 2 * nstep),
        in_specs=in_specs,
        out_specs=pl.BlockSpec((DEC_SEQ, ATT_W), lambda b, s, pt: (b, 0)),
        scratch_shapes=[
            pltpu.VMEM((nrow, ATT_W), BF16),
            pltpu.VMEM((N_PAGES + 1, nrow, PAGE_SIZE), F32),
            pltpu.VMEM((nrow, 1), F32),
            pltpu.VMEM((nrow, 1), F32),
            pltpu.VMEM((nrow, ATT_W), F32),
            pltpu.VMEM((H_ATT, 1), F32),
        ],
    )
    return pl.pallas_call(
        _attn_sample_kernel,
        grid_spec=grid_spec,
        out_shape=jax.ShapeDtypeStruct((N_SAMPLE, ATT_W), F32),
        compiler_params=_cparams(("parallel", "arbitrary"), VMEM_LIMIT),
        name="attn_sample",
    )(page_table, u, u, u, fq, fn, *([cache_k] * g), *([cache_v] * g), *([wt] * g))


def _rglru_kernel(xr_ref, gr_ref, cs_ref, h0_ref, cw_ref, cb_ref, wg_ref, bg_ref, lam_ref,
                  y_ref, hN_ref, cN_ref, xbuf, abuf, bbuf, hbuf, hcar, *, tc):
    c = pl.program_id(1)
    x = xr_ref[...]

    @pl.when(c == 0)
    def _():
        xbuf[0:8, :] = jnp.zeros((8, REC_W), F32)
        xbuf[8 - (CONV_W - 1):8, :] = cs_ref[0]
        hcar[...] = h0_ref[0]

    xbuf[8:8 + tc, :] = x
    xc = cb_ref[...] + xbuf[5:5 + tc, :] * cw_ref[0:1, :]
    xc = xc + xbuf[6:6 + tc, :] * cw_ref[1:2, :]
    xc = xc + xbuf[7:7 + tc, :] * cw_ref[2:3, :]
    xc = xc + x * cw_ref[3:4, :]
    xbuf[0:8, :] = xbuf[tc:tc + 8, :]

    gates = jnp.dot(xc.astype(BF16), wg_ref[...], preferred_element_type=F32) + bg_ref[...]
    rg = jax.nn.sigmoid(gates[:, :REC_W])
    ig = jax.nn.sigmoid(gates[:, REC_W:])
    nl = -lam_ref[...]
    softplus = jnp.maximum(nl, 0.0) + jnp.log1p(jnp.exp(-jnp.abs(nl)))
    log_a = (-RG_C * softplus) * rg
    a = jnp.exp(log_a)
    abuf[...] = a
    bbuf[...] = jnp.sqrt(-jnp.tanh(log_a) * (a * a + 1.0)) * (ig * xc)

    def group(gi, h):
        base = pl.multiple_of(gi * 8, 8)
        a8 = abuf[pl.ds(base, 8), :]
        b8 = bbuf[pl.ds(base, 8), :]
        rows = []
        for r in range(8):
            h = a8[r:r + 1, :] * h + b8[r:r + 1, :]
            rows.append(h)
        hbuf[pl.ds(base, 8), :] = jnp.concatenate(rows, axis=0)
        return h

    h_last = lax.fori_loop(0, tc // 8, group, hcar[...])
    hcar[...] = h_last
    y_ref[...] = hbuf[...] * jax.nn.gelu(gr_ref[...], approximate=True)

    @pl.when(c == pl.num_programs(1) - 1)
    def _():
        hN_ref[0] = h_last
        cN_ref[0] = x[tc - (CONV_W - 1):tc, :]


def _rglru(u, conv_state, h0, cw, cb, wg, bg, lam, *, nseq, slen, tc, row0):
    nchunk = slen // tc
    blk0 = row0 // tc
    xcol = (3 * ATT_W) // REC_W
    rep = lambda shape: pl.BlockSpec(shape, lambda b, c: (0,) * len(shape))
    return pl.pallas_call(
        functools.partial(_rglru_kernel, tc=tc),
        grid=(nseq, nchunk),
        in_specs=[
            pl.BlockSpec((tc, REC_W), lambda b, c: (blk0 + b * nchunk + c, xcol)),
            pl.BlockSpec((tc, REC_W), lambda b, c: (blk0 + b * nchunk + c, xcol + 1)),
            pl.BlockSpec((1, CONV_W - 1, REC_W), lambda b, c: (b, 0, 0)),
            pl.BlockSpec((1, 1, REC_W), lambda b, c: (b, 0, 0)),
            rep((CONV_W, REC_W)), rep((1, REC_W)), rep((REC_W, 2 * REC_W)),
            rep((1, 2 * REC_W)), rep((1, REC_W)),
        ],
        out_specs=[
            pl.BlockSpec((tc, REC_W), lambda b, c: (b * nchunk + c, 0)),
            pl.BlockSpec((1, 1, REC_W), lambda b, c: (b, 0, 0)),
            pl.BlockSpec((1, CONV_W - 1, REC_W), lambda b, c: (b, 0, 0)),
        ],
        out_shape=[
            jax.ShapeDtypeStruct((nseq * slen, REC_W), F32),
            jax.ShapeDtypeStruct((nseq, 1, REC_W), F32),
            jax.ShapeDtypeStruct((nseq, CONV_W - 1, REC_W), F32),
        ],
        scratch_shapes=[
            pltpu.VMEM((tc + 8, REC_W), F32),
            pltpu.VMEM((tc, REC_W), F32),
            pltpu.VMEM((tc, REC_W), F32),
            pltpu.VMEM((tc, REC_W), F32),
            pltpu.VMEM((1, REC_W), F32),
        ],
        compiler_params=_cparams(("parallel", "arbitrary"), VMEM_LIMIT),
        name="rglru",
    )(u, u, conv_state, h0, cw, cb, wg, bg, lam)


def _layer_norm(x, g, b):
    mu = jnp.mean(x, axis=-1, keepdims=True)
    xc = x - mu
    var = jnp.mean(xc * xc, axis=-1, keepdims=True)
    return xc * lax.rsqrt(var + EPS) * g + b


def _rms_norm(x, g):
    return x * lax.rsqrt(jnp.mean(x * x, axis=-1, keepdims=True) + EPS) * g


def _route(logits, rbias):
    shape = logits.shape
    ninf = -jnp.inf
    lane = lax.broadcasted_iota(jnp.int32, shape, 1)
    lane_f = lane.astype(F32)
    gsz = N_EXPERTS // N_GROUPS
    grp_f = (lane >> 3).astype(F32)
    valid = lane < N_EXPERTS
    scores = jax.nn.sigmoid(logits)
    biased = jnp.where(valid, scores + rbias, ninf)

    def first_max(vals, idx_f):
        m = jnp.max(vals, axis=1, keepdims=True)
        i = jnp.min(jnp.where(vals == m, idx_f, 1e9), axis=1, keepdims=True)
        return m, i

    gs = jnp.full(shape, ninf, F32)
    for g in range(N_GROUPS):
        in_g = (lane >= g * gsz) & (lane < (g + 1) * gsz)
        v = jnp.where(in_g, biased, ninf)
        m1, i1 = first_max(v, lane_f)
        m2 = jnp.max(jnp.where(lane_f == i1, ninf, v), axis=1, keepdims=True)
        gs = jnp.where(in_g, m1 + m2, gs)
    keep = jnp.zeros(shape, F32)
    for _ in range(TOPK_GROUPS):
        _, gi = first_max(gs, grp_f)
        pick = grp_f == gi
        keep = jnp.where(pick, 1.0, keep)
        gs = jnp.where(pick, ninf, gs)
    cand = jnp.where(keep > 0.0, biased, ninf)
    sel = jnp.zeros(shape, F32)
    for _ in range(TOP_K):
        _, ei = first_max(cand, lane_f)
        pick = lane_f == ei
        sel = jnp.where(pick, 1.0, sel)
        cand = jnp.where(pick, ninf, cand)
    picked = jnp.where(sel > 0.0, scores, 0.0)
    gate = picked / jnp.sum(picked, axis=1, keepdims=True) * ROUTED_SCALE
    return jnp.where(lane == N_EXPERTS, 1.0, gate)


def _out_kernel(att_ref, rec_ref, x_ref, ga_ref, gr_ref, wa_ref, wr_ref, g1_ref, b1_ref,
                rw_ref, rb_ref, x1_ref, gate_ref):
    na = _rms_norm(att_ref[...], ga_ref[...]).astype(BF16)
    nr = _rms_norm(rec_ref[...], gr_ref[...]).astype(BF16)
    mix = (jnp.dot(na, wa_ref[...], preferred_element_type=F32)
           + jnp.dot(nr, wr_ref[...], preferred_element_type=F32))
    x1 = _layer_norm(ALPHA * x_ref[...] + mix, g1_ref[...], b1_ref[...])
    x1_ref[...] = x1
    logits = jnp.dot(x1.astype(BF16), rw_ref[...], preferred_element_type=F32)
    gate_ref[...] = _route(logits, rb_ref[...])


def _out_proj(att, rec, x, ga, gr, wa, wr, g1, b1, rw, rb, tm):
    n = x.shape[0]
    rep = lambda shape: pl.BlockSpec(shape, lambda i: (0,) * len(shape))
    return pl.pallas_call(
        _out_kernel,
        grid=(n // tm,),
        in_specs=[
            pl.BlockSpec((tm, ATT_W), lambda i: (i, 0)),
            pl.BlockSpec((tm, REC_W), lambda i: (i, 0)),
            pl.BlockSpec((tm, D_MODEL), lambda i: (i, 0)),
            rep((1, ATT_W)), rep((1, REC_W)),
            rep((ATT_W, D_MODEL)), rep((REC_W, D_MODEL)),
            rep((1, D_MODEL)), rep((1, D_MODEL)),
            rep((D_MODEL, LANES)), rep((1, LANES)),
        ],
        out_specs=[pl.BlockSpec((tm, D_MODEL), lambda i: (i, 0)),
                   pl.BlockSpec((tm, LANES), lambda i: (i, 0))],
        out_shape=[jax.ShapeDtypeStruct((n, D_MODEL), F32),
                   jax.ShapeDtypeStruct((n, LANES), F32)],
        compiler_params=_cparams(("parallel",), VMEM_LIMIT),
        name="out_proj_router",
    )(att, rec, x, ga, gr, wa, wr, g1, b1, rw, rb)


def _moe_kernel(x_ref, gate_ref, w13_ref, w2_ref, g2_ref, b2_ref, o_ref, xb_sc, acc_sc):
    e = pl.program_id(1)

    @pl.when(e == 0)
    def _():
        xb_sc[...] = x_ref[...].astype(BF16)
        acc_sc[...] = jnp.zeros_like(acc_sc)

    gate = gate_ref[...]
    lane = lax.broadcasted_iota(jnp.int32, gate.shape, 1)
    ge = jnp.sum(jnp.where(lane == e, gate, 0.0), axis=1, keepdims=True)
    h13 = jnp.dot(xb_sc[...], w13_ref[0], preferred_element_type=F32)
    h1 = h13[:, :D_EXPERT]
    h = (h1 * jax.nn.sigmoid(h1)) * h13[:, D_EXPERT:]
    acc_sc[...] += jnp.dot(h.astype(BF16), w2_ref[0], preferred_element_type=F32) * ge

    @pl.when(e == pl.num_programs(1) - 1)
    def _():
        o_ref[...] = _layer_norm(ALPHA * x_ref[...] + acc_sc[...], g2_ref[...], b2_ref[...])


def _moe(x1, gate, w13, w2, g2, b2, tm):
    n = x1.shape[0]
    ne = w13.shape[0]
    return pl.pallas_call(
        _moe_kernel,
        grid=(n // tm, ne),
        in_specs=[
            pl.BlockSpec((tm, D_MODEL), lambda i, e: (i, 0)),
            pl.BlockSpec((tm, LANES), lambda i, e: (i, 0)),
            pl.BlockSpec((1, D_MODEL, 2 * D_EXPERT), lambda i, e: (e, 0, 0)),
            pl.BlockSpec((1, D_EXPERT, D_MODEL), lambda i, e: (e, 0, 0)),
            pl.BlockSpec((1, D_MODEL), lambda i, e: (0, 0)),
            pl.BlockSpec((1, D_MODEL), lambda i, e: (0, 0)),
        ],
        out_specs=pl.BlockSpec((tm, D_MODEL), lambda i, e: (i, 0)),
        out_shape=jax.ShapeDtypeStruct((n, D_MODEL), F32),
        scratch_shapes=[pltpu.VMEM((tm, D_MODEL), BF16), pltpu.VMEM((tm, D_MODEL), F32)],
        compiler_params=_cparams(("parallel", "arbitrary"), VMEM_LIMIT),
        name="moe_experts",
    )(x1, gate, w13, w2, g2, b2)


def _block_diag(w):
    nb, k, j = w.shape
    eye = jnp.eye(nb, dtype=w.dtype)
    return jnp.einsum('nkj,nm->nkmj', w, eye).reshape(nb * k, nb * j)


def kernel(x_prompt, x_sample, cache_k, cache_v, cache_logf, state_h, state_conv, page_table, w_in, b_f, conv_w, conv_b, w_gr, b_gr, w_gi, b_gi, lam, g_att, g_rec, w_out, ln1_g, ln1_b, w_router, router_bias, w1, w3, w2, ws1, ws3, ws2, ln2_g, ln2_b):
    n_pool = cache_k.shape[1]
    x = jnp.concatenate([x_prompt.reshape(N_PROMPT, D_MODEL), x_sample.reshape(N_SAMPLE, D_MODEL)], axis=0)
    ck = cache_k.reshape(DEPTH, n_pool, PAGE_SIZE, ATT_W)
    cv = cache_v.reshape(DEPTH, n_pool, PAGE_SIZE, ATT_W)
    lf_t = jnp.swapaxes(cache_logf, -1, -2).reshape(DEPTH * n_pool * H_ATT, PAGE_SIZE)
    wt = _cumsum_lanes(lf_t, 4096).reshape(DEPTH, n_pool, H_ATT, PAGE_SIZE)
    zero_conv = jnp.zeros((BATCH, CONV_W - 1, REC_W), F32)
    zero_h = jnp.zeros((BATCH, 1, REC_W), F32)
    cf = 3 * ATT_W

    outs = {k: [] for k in ("kp", "vp", "fp", "hp", "cp", "ks", "vs", "fs", "hs", "cs")}
    for l in range(DEPTH):
        w_main = jnp.concatenate([w_in[l][:, :cf], w_in[l][:, cf + H_ATT:]], axis=1).astype(BF16)
        w_f = jnp.pad(w_in[l][:, cf:cf + H_ATT], ((0, 0), (0, LANES - H_ATT))).astype(BF16)
        bf_pad = jnp.pad(b_f[l], (0, LANES - H_ATT)).reshape(1, LANES)
        u = _matmul(x, w_main, TM_PROJ, TN_PROJ)
        flog = _forget_logits(x, w_f, bf_pad, TM_PROJ)[:, :H_ATT]
        logf_p = flog[:N_PROMPT].reshape(BATCH, SEQ, H_ATT)
        logf_s = flog[N_PROMPT:].reshape(DEC_BATCH, DEC_SEQ, H_ATT)

        ft = _cumsum_lanes(jnp.swapaxes(logf_p, 1, 2).reshape(BATCH * H_ATT, SEQ), BATCH * H_ATT)
        fr = ft.reshape(BATCH, H_ATT // 2, 2, SEQ)
        fc = jnp.swapaxes(fr, 2, 3)
        att_p = _attn_prompt(u, fc, fr)

        ls_t = jnp.pad(jnp.swapaxes(logf_s, 1, 2), ((0, 0), (0, 0), (0, LANES - DEC_SEQ)))
        fn = _cumsum_lanes(ls_t.reshape(DEC_BATCH * H_ATT, LANES), DEC_BATCH * H_ATT)
        fn = fn.reshape(DEC_BATCH, H_ATT, LANES)
        fq = fn[:, :, :DEC_SEQ].reshape(DEC_BATCH, H_ATT * DEC_SEQ, 1)
        att_s = _attn_sample(l, page_table, u, ck, cv, wt, fq, fn)

        wg = jnp.concatenate([_block_diag(w_gr[l]), _block_diag(w_gi[l])], axis=1).astype(BF16)
        bg = jnp.concatenate([b_gr[l], b_gi[l]]).reshape(1, 2 * REC_W)
        rec_args = (conv_w[l], conv_b[l].reshape(1, REC_W), wg, bg, lam[l].reshape(1, REC_W))
        rec_p, h_p, conv_p = _rglru(u, zero_conv, zero_h, *rec_args,
                                    nseq=BATCH, slen=SEQ, tc=TC_REC, row0=0)
        rec_s, h_s, conv_s = _rglru(u, state_conv[l], state_h[l].reshape(DEC_BATCH, 1, REC_W), *rec_args,
                                    nseq=DEC_BATCH, slen=DEC_SEQ, tc=DEC_SEQ, row0=N_PROMPT)

        att = jnp.concatenate([att_p, att_s], axis=0)
        rec = jnp.concatenate([rec_p, rec_s], axis=0)
        rw = jnp.pad(w_router[l], ((0, 0), (0, LANES - N_EXPERTS))).astype(BF16)
        rb = jnp.pad(router_bias[l], (0, LANES - N_EXPERTS)).reshape(1, LANES)
        x1, gate = _out_proj(att, rec, x, g_att[l].reshape(1, ATT_W), g_rec[l].reshape(1, REC_W),
                             w_out[l][:ATT_W].astype(BF16), w_out[l][ATT_W:].astype(BF16),
                             ln1_g[l].reshape(1, D_MODEL), ln1_b[l].reshape(1, D_MODEL),
                             rw, rb, TM_OUT)

        w13 = jnp.concatenate([jnp.concatenate([w1[l], w3[l]], axis=2),
                               jnp.concatenate([ws1[l], ws3[l]], axis=1)[None]], axis=0).astype(BF16)
        w2c = jnp.concatenate([w2[l], ws2[l][None]], axis=0).astype(BF16)
        x = _moe(x1, gate, w13, w2c, ln2_g[l].reshape(1, D_MODEL), ln2_b[l].reshape(1, D_MODEL), TM_MOE)

        kv = lambda rows, c0, shape: u[rows, c0:c0 + ATT_W].reshape(shape)
        p_rows, s_rows = slice(0, N_PROMPT), slice(N_PROMPT, N_TOK)
        p_shape = (BATCH, SEQ, H_ATT, HEAD_DIM)
        s_shape = (DEC_BATCH, DEC_SEQ, H_ATT, HEAD_DIM)
        outs["kp"].append(kv(p_rows, ATT_W, p_shape))
        outs["vp"].append(kv(p_rows, 2 * ATT_W, p_shape))
        outs["fp"].append(logf_p)
        outs["hp"].append(h_p.reshape(BATCH, REC_W))
        outs["cp"].append(conv_p)
        outs["ks"].append(kv(s_rows, ATT_W, s_shape))
        outs["vs"].append(kv(s_rows, 2 * ATT_W, s_shape))
        outs["fs"].append(logf_s)
        outs["hs"].append(h_s.reshape(DEC_BATCH, REC_W))
        outs["cs"].append(conv_s)

    st = lambda name: jnp.stack(outs[name])
    return (x[:N_PROMPT].reshape(BATCH, SEQ, D_MODEL), x[N_PROMPT:].reshape(DEC_BATCH, DEC_SEQ, D_MODEL),
            st("kp"), st("vp"), st("fp"), st("hp"), st("cp"),
            st("ks"), st("vs"), st("fs"), st("hs"), st("cs"))
```

```python
import functools

import jax
import jax.numpy as jnp
from jax import lax
from jax.experimental import pallas as pl
from jax.experimental.pallas import tpu as pltpu

F32 = jnp.float32
BF16 = jnp.bfloat16

D_MODEL = 1024
BATCH = 4
SEQ = 4096
DEPTH = 2
DEC_BATCH = 32
DEC_SEQ = 8
PAST_LEN = 8192
PAGE_SIZE = 128
ATT_W = 512
REC_W = 512
HEAD_DIM = 64
H_ATT = 8
N_REC_BLOCKS = 8
CONV_W = 4
RG_C = 8.0
N_EXPERTS = 64
TOP_K = 8
N_GROUPS = 8
TOPK_GROUPS = 4
D_EXPERT = 256
ROUTED_SCALE = 2.5
ALPHA = (2 * DEPTH) ** 0.25
EPS = 1e-5
ATT_SCALE = HEAD_DIM ** -0.5

N_PROMPT = BATCH * SEQ
N_SAMPLE = DEC_BATCH * DEC_SEQ
N_TOK = N_PROMPT + N_SAMPLE
N_PAGES = PAST_LEN // PAGE_SIZE
U_W = 3 * ATT_W + 2 * REC_W
LANES = 128
NEG = -1e30

TM_PROJ = 1280
TN_PROJ = 640
TM_OUT = 640
TM_MOE = 1280
TQ = 512
TC_REC = 512
PAGES_PER_STEP = 8
VMEM_LIMIT = 56 * 1024 * 1024


def _cparams(sem, vmem=None):
    return pltpu.CompilerParams(dimension_semantics=sem, vmem_limit_bytes=vmem)


def _mm_kernel(x_ref, w_ref, o_ref):
    o_ref[...] = jnp.dot(x_ref[...].astype(BF16), w_ref[...], preferred_element_type=F32)


def _matmul(x, w, tm, tn):
    m, k = x.shape
    n = w.shape[1]
    return pl.pallas_call(
        _mm_kernel,
        grid=(m // tm, n // tn),
        in_specs=[pl.BlockSpec((tm, k), lambda i, j: (i, 0)),
                  pl.BlockSpec((k, tn), lambda i, j: (0, j))],
        out_specs=pl.BlockSpec((tm, tn), lambda i, j: (i, j)),
        out_shape=jax.ShapeDtypeStruct((m, n), F32),
        compiler_params=_cparams(("parallel", "arbitrary"), VMEM_LIMIT),
        name="in_proj",
    )(x, w)


def _forget_kernel(x_ref, w_ref, b_ref, o_ref):
    z = jnp.dot(x_ref[...].astype(BF16), w_ref[...], preferred_element_type=F32) + b_ref[...]
    o_ref[...] = jnp.minimum(z, 0.0) - jnp.log1p(jnp.exp(-jnp.abs(z)))


def _forget_logits(x, w, b, tm):
    m, k = x.shape
    n = w.shape[1]
    return pl.pallas_call(
        _forget_kernel,
        grid=(m // tm,),
        in_specs=[pl.BlockSpec((tm, k), lambda i: (i, 0)),
                  pl.BlockSpec((k, n), lambda i: (0, 0)),
                  pl.BlockSpec((1, n), lambda i: (0, 0))],
        out_specs=pl.BlockSpec((tm, n), lambda i: (i, 0)),
        out_shape=jax.ShapeDtypeStruct((m, n), F32),
        compiler_params=_cparams(("parallel",), VMEM_LIMIT),
        name="forget_logits",
    )(x, w, b)


def _split3(x):
    hi = x.astype(BF16)
    r1 = x - hi.astype(F32)
    mid = r1.astype(BF16)
    lo = (r1 - mid.astype(F32)).astype(BF16)
    return hi, mid, lo


def _cumsum_kernel(x_ref, o_ref, *, nchunk):
    rows = x_ref.shape[0]
    r = lax.broadcasted_iota(jnp.int32, (LANES, LANES), 0)
    c = lax.broadcasted_iota(jnp.int32, (LANES, LANES), 1)
    upper = jnp.where(r <= c, 1.0, 0.0).astype(BF16)
    carry = jnp.zeros((rows, 1), F32)
    for ch in range(nchunk):
        x = x_ref[:, ch * LANES:(ch + 1) * LANES]
        hi, mid, lo = _split3(x)
        w = (jnp.dot(hi, upper, preferred_element_type=F32)
             + jnp.dot(mid, upper, preferred_element_type=F32)
             + jnp.dot(lo, upper, preferred_element_type=F32)) + carry
        o_ref[:, ch * LANES:(ch + 1) * LANES] = w
        carry = w[:, LANES - 1:LANES]


def _cumsum_lanes(x, tr):
    rows, s = x.shape
    return pl.pallas_call(
        functools.partial(_cumsum_kernel, nchunk=s // LANES),
        grid=(rows // tr,),
        in_specs=[pl.BlockSpec((tr, s), lambda i: (i, 0))],
        out_specs=pl.BlockSpec((tr, s), lambda i: (i, 0)),
        out_shape=jax.ShapeDtypeStruct((rows, s), F32),
        compiler_params=_cparams(("parallel",)),
        name="cumsum_lanes",
    )(x)


def _attn_prompt_kernel(q_ref, k_ref, v_ref, fc_ref, fr_ref, o_ref, s_sc):
    qb = pl.program_id(2)
    lane = lax.broadcasted_iota(jnp.int32, (TQ, LANES), 1)
    row = lax.broadcasted_iota(jnp.int32, (TQ, TQ), 0)
    col = lax.broadcasted_iota(jnp.int32, (TQ, TQ), 1)
    q = q_ref[...] * ATT_SCALE
    fc = fc_ref[0, 0]
    outs = []
    for h in range(2):
        in_head = (lane < HEAD_DIM) if h == 0 else (lane >= HEAD_DIM)
        qm = jnp.where(in_head, q, 0.0).astype(BF16)
        fq = fc[:, h:h + 1]

        def scores(kb, carry, masked):
            m, l = carry
            k0 = pl.multiple_of(kb * TQ, TQ)
            k = k_ref[pl.ds(k0, TQ), :].astype(BF16)
            fk = fr_ref[0, 0, h:h + 1, pl.ds(k0, TQ)]
            s = lax.dot_general(qm, k, (((1,), (1,)), ((), ())), preferred_element_type=F32)
            s = s + (fq - fk)
            if masked:
                s = jnp.where(col <= row, s, NEG)
            s_sc[kb] = s
            m_new = jnp.maximum(m, jnp.max(s, axis=1, keepdims=True))
            l = jnp.exp(m - m_new) * l + jnp.sum(jnp.exp(s - m_new), axis=1, keepdims=True)
            return m_new, l

        init = (jnp.full((TQ, 1), NEG, F32), jnp.zeros((TQ, 1), F32))
        carry = lax.fori_loop(0, qb, lambda kb, c: scores(kb, c, False), init)
        m, l = scores(qb, carry, True)

        def values(kb, acc):
            k0 = pl.multiple_of(kb * TQ, TQ)
            v = v_ref[pl.ds(k0, TQ), :].astype(BF16)
            p = jnp.exp(s_sc[kb] - m) / l
            return acc + jnp.dot(p.astype(BF16), v, preferred_element_type=F32)

        outs.append(lax.fori_loop(0, qb + 1, values, jnp.zeros((TQ, LANES), F32)))
    o_ref[...] = jnp.where(lane < HEAD_DIM, outs[0], outs[1])


def _attn_prompt(u, fc, fr):
    nq = SEQ // TQ
    kblk = ATT_W // LANES
    return pl.pallas_call(
        _attn_prompt_kernel,
        grid=(BATCH, H_ATT // 2, nq),
        in_specs=[
            pl.BlockSpec((TQ, LANES), lambda b, p, i: (b * nq + i, p)),
            pl.BlockSpec((SEQ, LANES), lambda b, p, i: (b, kblk + p)),
            pl.BlockSpec((SEQ, LANES), lambda b, p, i: (b, 2 * kblk + p)),
            pl.BlockSpec((1, 1, TQ, 2), lambda b, p, i: (b, p, i, 0)),
            pl.BlockSpec((1, 1, 2, SEQ), lambda b, p, i: (b, p, 0, 0)),
        ],
        out_specs=pl.BlockSpec((TQ, LANES), lambda b, p, i: (b * nq + i, p)),
        out_shape=jax.ShapeDtypeStruct((N_PROMPT, ATT_W), F32),
        scratch_shapes=[pltpu.VMEM((nq, TQ, TQ), F32)],
        compiler_params=_cparams(("parallel", "parallel", "arbitrary"), VMEM_LIMIT),
        name="attn_prompt",
    )(u, u, u, fc, fr)


def _expand_heads(x):
    n = x.shape[1]
    return jnp.concatenate(
        [jnp.broadcast_to(x[h:h + 1, :], (DEC_SEQ, n)) for h in range(H_ATT)], axis=0)


def _suffix_kernel(x_ref, e_ref, t_ref):
    nchunk = x_ref.shape[1] // LANES
    r = lax.broadcasted_iota(jnp.int32, (LANES, LANES), 0)
    c = lax.broadcasted_iota(jnp.int32, (LANES, LANES), 1)
    same_head = (r & (H_ATT - 1)) == (c & (H_ATT - 1))
    later = jnp.where(same_head & (r > c), 1.0, 0.0).astype(BF16)
    every = jnp.where(same_head, 1.0, 0.0).astype(BF16)

    def dot3(x, w):
        hi, mid, lo = _split3(x)
        return (jnp.dot(hi, w, preferred_element_type=F32) + jnp.dot(mid, w, preferred_element_type=F32)
                + jnp.dot(lo, w, preferred_element_type=F32))

    carry = jnp.zeros((x_ref.shape[0], LANES), F32)
    for ch in range(nchunk - 1, -1, -1):
        x = x_ref[:, ch * LANES:(ch + 1) * LANES]
        e_ref[:, ch * LANES:(ch + 1) * LANES] = dot3(x, later) + carry
        carry = carry + dot3(x, every)
    t_ref[...] = carry


def _page_suffix(x, tr):
    rows, n = x.shape
    return pl.pallas_call(
        _suffix_kernel,
        grid=(rows // tr,),
        in_specs=[pl.BlockSpec((tr, n), lambda i: (i, 0))],
        out_specs=[pl.BlockSpec((tr, n), lambda i: (i, 0)), pl.BlockSpec((tr, LANES), lambda i: (i, 0))],
        out_shape=[jax.ShapeDtypeStruct((rows, n), F32), jax.ShapeDtypeStruct((rows, LANES), F32)],
        compiler_params=_cparams(("parallel",)),
        name="page_suffix",
    )(x)


def _attn_sample_kernel(pt_ref, q_ref, kn_ref, vn_ref, fq_ref, fn_ref, *rest):
    del pt_ref
    g = PAGES_PER_STEP
    nstep = N_PAGES // g
    k_refs, v_refs, e_refs, t_refs = rest[:g], rest[g:2 * g], rest[2 * g:3 * g], rest[3 * g:4 * g]
    o_ref = rest[4 * g]
    qs_sc, qbd_sc, s_sc, sn_sc, m_sc, r_sc, acc_sc, accn_sc, suf_sc = rest[4 * g + 1:]
    step = pl.program_id(1)
    nrow = H_ATT * DEC_SEQ
    nkh = PAGE_SIZE * H_ATT
    nchunk = nkh // LANES
    row = lax.broadcasted_iota(jnp.int32, (nrow, ATT_W), 0)
    colw = lax.broadcasted_iota(jnp.int32, (nrow, ATT_W), 1)
    own_head = (colw >> 6) == (row >> 3)
    rowc = lax.broadcasted_iota(jnp.int32, (nrow, LANES), 0)
    lanec = lax.broadcasted_iota(jnp.int32, (nrow, LANES), 1)
    head_lane = (lanec & (H_ATT - 1)) == (rowc >> 3)
    fq = fq_ref[0]
    pad = jnp.zeros((PAGE_SIZE - DEC_SEQ, ATT_W), F32)
    nt = (((1,), (1,)), ((), ()))

    @pl.when(step == 0)
    def _():
        q = q_ref[...] * ATT_SCALE
        qs_sc[...] = jnp.concatenate(
            [q[:, h * HEAD_DIM:(h + 1) * HEAD_DIM] for h in range(H_ATT)], axis=0).astype(BF16)
        qt = jnp.concatenate([q] * H_ATT, axis=0)
        qbd_sc[...] = jnp.where(own_head, qt, 0.0).astype(BF16)
        suf_sc[...] = jnp.zeros((1, LANES), F32)
        kn = jnp.concatenate([kn_ref[...], pad], axis=0).astype(BF16)
        s = lax.dot_general(qbd_sc[...], kn, nt, preferred_element_type=F32)
        s = s + (fq - _expand_heads(fn_ref[0]))
        s = jnp.where(lanec <= (rowc & (DEC_SEQ - 1)), s, NEG)
        sn_sc[...] = s
        m_sc[...] = jnp.max(s, axis=1, keepdims=True)

    @pl.when(step < nstep)
    def _():
        first = (nstep - 1 - step) * g
        qs = qs_sc[...]
        mrun = jnp.full((nrow, LANES), NEG, F32)
        for i in range(g - 1, -1, -1):
            k2 = k_refs[i][0, 0].reshape(nkh, HEAD_DIM).astype(BF16)
            s = lax.dot_general(qs, k2, nt, preferred_element_type=F32)
            e = e_refs[i][0, 0]
            suf = suf_sc[...]
            for c in range(nchunk):
                ch = s[:, c * LANES:(c + 1) * LANES] + (fq + (e[c:c + 1, :] + suf))
                ch = jnp.where(head_lane, ch, NEG)
                s_sc[first + i, :, c * LANES:(c + 1) * LANES] = ch
                mrun = jnp.maximum(mrun, ch)
            suf_sc[...] = suf + t_refs[i][0, 0]
        m_sc[...] = jnp.maximum(m_sc[...], jnp.max(mrun, axis=1, keepdims=True))

    @pl.when(step == nstep - 1)
    def _():
        m = m_sc[...]
        en = jnp.exp(sn_sc[...] - m)
        sn_sc[...] = en

        def page(p, lacc):
            for c in range(nchunk):
                ex = jnp.exp(s_sc[p, :, c * LANES:(c + 1) * LANES] - m)
                s_sc[p, :, c * LANES:(c + 1) * LANES] = ex
                lacc = lacc + ex
            return lacc

        lacc = lax.fori_loop(0, N_PAGES, page, jnp.zeros((nrow, LANES), F32))
        l = jnp.sum(en, axis=1, keepdims=True) + jnp.sum(lacc, axis=1, keepdims=True)
        r_sc[...] = 1.0 / l

    @pl.when(step >= nstep)
    def _():
        first = (2 * nstep - 1 - step) * g
        r = r_sc[...]

        @pl.when(step == nstep)
        def _():
            vn = jnp.concatenate([vn_ref[...], pad], axis=0).astype(BF16)
            accn_sc[...] = jnp.dot((sn_sc[...] * r).astype(BF16), vn, preferred_element_type=F32)
            acc_sc[...] = jnp.zeros((nrow, HEAD_DIM), F32)

        acc = acc_sc[...]
        for i in range(g - 1, -1, -1):
            v2 = v_refs[i][0, 0].reshape(nkh, HEAD_DIM).astype(BF16)
            p = (s_sc[first + i] * r).astype(BF16)
            acc = acc + jnp.dot(p, v2, preferred_element_type=F32)
        acc_sc[...] = acc

    @pl.when(step == 2 * nstep - 1)
    def _():
        o = jnp.where(own_head, accn_sc[...], 0.0)
        out = o[0:DEC_SEQ, :]
        for h in range(1, H_ATT):
            out = out + o[h * DEC_SEQ:(h + 1) * DEC_SEQ, :]
        acc = acc_sc[...]
        out = out + jnp.concatenate([acc[h * DEC_SEQ:(h + 1) * DEC_SEQ, :] for h in range(H_ATT)], axis=1)
        o_ref[...] = out


def _attn_sample(layer, page_table, u, cache_k, cache_v, esuf, etot, fq, fn):
    g = PAGES_PER_STEP
    nstep = N_PAGES // g
    srow = N_PROMPT // DEC_SEQ
    nrow = H_ATT * DEC_SEQ

    def key_map(i, nd):
        return lambda b, s, pt: (layer, pt[b, (nstep - 1 - jnp.minimum(s, nstep - 1)) * g + i]) + (0,) * nd

    def value_map(i):
        return lambda b, s, pt: (layer, pt[b, (nstep - 1 - jnp.maximum(s - nstep, 0)) * g + i], 0, 0, 0)

    page = (1, 1, PAGE_SIZE, H_ATT, HEAD_DIM)
    in_specs = [
        pl.BlockSpec((DEC_SEQ, ATT_W), lambda b, s, pt: (srow + b, 0)),
        pl.BlockSpec((DEC_SEQ, ATT_W), lambda b, s, pt: (srow + b, 1)),
        pl.BlockSpec((DEC_SEQ, ATT_W), lambda b, s, pt: (srow + b, 2)),
        pl.BlockSpec((1, nrow, 1), lambda b, s, pt: (b, 0, 0)),
        pl.BlockSpec((1, H_ATT, LANES), lambda b, s, pt: (b, 0, 0)),
    ]
    in_specs += [pl.BlockSpec(page, key_map(i, 3)) for i in range(g)]
    in_specs += [pl.BlockSpec(page, value_map(i)) for i in range(g)]
    in_specs += [pl.BlockSpec((1, 1, H_ATT, LANES), key_map(i, 2)) for i in range(g)]
    in_specs += [pl.BlockSpec((1, 1, 1, LANES), key_map(i, 2)) for i in range(g)]
    grid_spec = pltpu.PrefetchScalarGridSpec(
        num_scalar_prefetch=1,
        grid=(DEC_BATCH, 2 * nstep),
        in_specs=in_specs,
        out_specs=pl.BlockSpec((DEC_SEQ, ATT_W), lambda b, s, pt: (b, 0)),
        scratch_shapes=[
            pltpu.VMEM((nrow, HEAD_DIM), BF16),
            pltpu.VMEM((nrow, ATT_W), BF16),
            pltpu.VMEM((N_PAGES, nrow, PAGE_SIZE * H_ATT), F32),
            pltpu.VMEM((nrow, PAGE_SIZE), F32),
            pltpu.VMEM((nrow, 1), F32),
            pltpu.VMEM((nrow, 1), F32),
            pltpu.VMEM((nrow, HEAD_DIM), F32),
            pltpu.VMEM((nrow, ATT_W), F32),
            pltpu.VMEM((1, LANES), F32),
        ],
    )
    return pl.pallas_call(
        _attn_sample_kernel,
        grid_spec=grid_spec,
        out_shape=jax.ShapeDtypeStruct((N_SAMPLE, ATT_W), F32),
        compiler_params=_cparams(("parallel", "arbitrary"), VMEM_LIMIT),
        name="attn_sample",
    )(page_table, u, u, u, fq, fn, *([cache_k] * g), *([cache_v] * g), *([esuf] * g), *([etot] * g))


def _rglru_kernel(xr_ref, gr_ref, cs_ref, h0_ref, cw_ref, cb_ref, wg_ref, bg_ref, lam_ref,
                  y_ref, hN_ref, cN_ref, xbuf, abuf, bbuf, hbuf, hcar, *, tc):
    c = pl.program_id(1)
    x = xr_ref[...]

    @pl.when(c == 0)
    def _():
        xbuf[0:8, :] = jnp.zeros((8, REC_W), F32)
        xbuf[8 - (CONV_W - 1):8, :] = cs_ref[0]
        hcar[...] = h0_ref[0]

    xbuf[8:8 + tc, :] = x
    xc = cb_ref[...] + xbuf[5:5 + tc, :] * cw_ref[0:1, :]
    xc = xc + xbuf[6:6 + tc, :] * cw_ref[1:2, :]
    xc = xc + xbuf[7:7 + tc, :] * cw_ref[2:3, :]
    xc = xc + x * cw_ref[3:4, :]
    xbuf[0:8, :] = xbuf[tc:tc + 8, :]

    gates = jnp.dot(xc.astype(BF16), wg_ref[...], preferred_element_type=F32) + bg_ref[...]
    rg = jax.nn.sigmoid(gates[:, :REC_W])
    ig = jax.nn.sigmoid(gates[:, REC_W:])
    nl = -lam_ref[...]
    softplus = jnp.maximum(nl, 0.0) + jnp.log1p(jnp.exp(-jnp.abs(nl)))
    log_a = (-RG_C * softplus) * rg
    a = jnp.exp(log_a)
    abuf[...] = a
    bbuf[...] = jnp.sqrt(-jnp.tanh(log_a) * (a * a + 1.0)) * (ig * xc)

    def group(gi, h):
        base = pl.multiple_of(gi * 8, 8)
        a8 = abuf[pl.ds(base, 8), :]
        b8 = bbuf[pl.ds(base, 8), :]
        rows = []
        for r in range(8):
            h = a8[r:r + 1, :] * h + b8[r:r + 1, :]
            rows.append(h)
        hbuf[pl.ds(base, 8), :] = jnp.concatenate(rows, axis=0)
        return h

    h_last = lax.fori_loop(0, tc // 8, group, hcar[...])
    hcar[...] = h_last
    y_ref[...] = hbuf[...] * jax.nn.gelu(gr_ref[...], approximate=True)

    @pl.when(c == pl.num_programs(1) - 1)
    def _():
        hN_ref[0] = h_last
        cN_ref[0] = x[tc - (CONV_W - 1):tc, :]


def _rglru(u, conv_state, h0, cw, cb, wg, bg, lam, *, nseq, slen, tc, row0):
    nchunk = slen // tc
    blk0 = row0 // tc
    xcol = (3 * ATT_W) // REC_W
    rep = lambda shape: pl.BlockSpec(shape, lambda b, c: (0,) * len(shape))
    return pl.pallas_call(
        functools.partial(_rglru_kernel, tc=tc),
        grid=(nseq, nchunk),
        in_specs=[
            pl.BlockSpec((tc, REC_W), lambda b, c: (blk0 + b * nchunk + c, xcol)),
            pl.BlockSpec((tc, REC_W), lambda b, c: (blk0 + b * nchunk + c, xcol + 1)),
            pl.BlockSpec((1, CONV_W - 1, REC_W), lambda b, c: (b, 0, 0)),
            pl.BlockSpec((1, 1, REC_W), lambda b, c: (b, 0, 0)),
            rep((CONV_W, REC_W)), rep((1, REC_W)), rep((REC_W, 2 * REC_W)),
            rep((1, 2 * REC_W)), rep((1, REC_W)),
        ],
        out_specs=[
            pl.BlockSpec((tc, REC_W), lambda b, c: (b * nchunk + c, 0)),
            pl.BlockSpec((1, 1, REC_W), lambda b, c: (b, 0, 0)),
            pl.BlockSpec((1, CONV_W - 1, REC_W), lambda b, c: (b, 0, 0)),
        ],
        out_shape=[
            jax.ShapeDtypeStruct((nseq * slen, REC_W), F32),
            jax.ShapeDtypeStruct((nseq, 1, REC_W), F32),
            jax.ShapeDtypeStruct((nseq, CONV_W - 1, REC_W), F32),
        ],
        scratch_shapes=[
            pltpu.VMEM((tc + 8, REC_W), F32),
            pltpu.VMEM((tc, REC_W), F32),
            pltpu.VMEM((tc, REC_W), F32),
            pltpu.VMEM((tc, REC_W), F32),
            pltpu.VMEM((1, REC_W), F32),
        ],
        compiler_params=_cparams(("parallel", "arbitrary"), VMEM_LIMIT),
        name="rglru",
    )(u, u, conv_state, h0, cw, cb, wg, bg, lam)


def _layer_norm(x, g, b):
    mu = jnp.mean(x, axis=-1, keepdims=True)
    xc = x - mu
    var = jnp.mean(xc * xc, axis=-1, keepdims=True)
    return xc * lax.rsqrt(var + EPS) * g + b


def _rms_norm(x, g):
    return x * lax.rsqrt(jnp.mean(x * x, axis=-1, keepdims=True) + EPS) * g


def _route(logits, rbias):
    shape = logits.shape
    ninf = -jnp.inf
    lane = lax.broadcasted_iota(jnp.int32, shape, 1)
    lane_f = lane.astype(F32)
    gsz = N_EXPERTS // N_GROUPS
    grp_f = (lane >> 3).astype(F32)
    valid = lane < N_EXPERTS
    scores = jax.nn.sigmoid(logits)
    biased = jnp.where(valid, scores + rbias, ninf)

    def first_max(vals, idx_f):
        m = jnp.max(vals, axis=1, keepdims=True)
        i = jnp.min(jnp.where(vals == m, idx_f, 1e9), axis=1, keepdims=True)
        return m, i

    gs = jnp.full(shape, ninf, F32)
    for g in range(N_GROUPS):
        in_g = (lane >= g * gsz) & (lane < (g + 1) * gsz)
        v = jnp.where(in_g, biased, ninf)
        m1, i1 = first_max(v, lane_f)
        m2 = jnp.max(jnp.where(lane_f == i1, ninf, v), axis=1, keepdims=True)
        gs = jnp.where(in_g, m1 + m2, gs)
    keep = jnp.zeros(shape, F32)
    for _ in range(TOPK_GROUPS):
        _, gi = first_max(gs, grp_f)
        pick = grp_f == gi
        keep = jnp.where(pick, 1.0, keep)
        gs = jnp.where(pick, ninf, gs)
    cand = jnp.where(keep > 0.0, biased, ninf)
    sel = jnp.zeros(shape, F32)
    for _ in range(TOP_K):
        _, ei = first_max(cand, lane_f)
        pick = lane_f == ei
        sel = jnp.where(pick, 1.0, sel)
        cand = jnp.where(pick, ninf, cand)
    picked = jnp.where(sel > 0.0, scores, 0.0)
    gate = picked / jnp.sum(picked, axis=1, keepdims=True) * ROUTED_SCALE
    return jnp.where(lane == N_EXPERTS, 1.0, gate)


def _out_kernel(att_ref, rec_ref, x_ref, ga_ref, gr_ref, wa_ref, wr_ref, g1_ref, b1_ref,
                rw_ref, rb_ref, x1_ref, gate_ref):
    na = _rms_norm(att_ref[...], ga_ref[...]).astype(BF16)
    nr = _rms_norm(rec_ref[...], gr_ref[...]).astype(BF16)
    mix = (jnp.dot(na, wa_ref[...], preferred_element_type=F32)
           + jnp.dot(nr, wr_ref[...], preferred_element_type=F32))
    x1 = _layer_norm(ALPHA * x_ref[...] + mix, g1_ref[...], b1_ref[...])
    x1_ref[...] = x1
    logits = jnp.dot(x1.astype(BF16), rw_ref[...], preferred_element_type=F32)
    gate_ref[...] = _route(logits, rb_ref[...])


def _out_proj(att, rec, x, ga, gr, wa, wr, g1, b1, rw, rb, tm):
    n = x.shape[0]
    rep = lambda shape: pl.BlockSpec(shape, lambda i: (0,) * len(shape))
    return pl.pallas_call(
        _out_kernel,
        grid=(n // tm,),
        in_specs=[
            pl.BlockSpec((tm, ATT_W), lambda i: (i, 0)),
            pl.BlockSpec((tm, REC_W), lambda i: (i, 0)),
            pl.BlockSpec((tm, D_MODEL), lambda i: (i, 0)),
            rep((1, ATT_W)), rep((1, REC_W)),
            rep((ATT_W, D_MODEL)), rep((REC_W, D_MODEL)),
            rep((1, D_MODEL)), rep((1, D_MODEL)),
            rep((D_MODEL, LANES)), rep((1, LANES)),
        ],
        out_specs=[pl.BlockSpec((tm, D_MODEL), lambda i: (i, 0)),
                   pl.BlockSpec((tm, LANES), lambda i: (i, 0))],
        out_shape=[jax.ShapeDtypeStruct((n, D_MODEL), F32),
                   jax.ShapeDtypeStruct((n, LANES), F32)],
        compiler_params=_cparams(("parallel",), VMEM_LIMIT),
        name="out_proj_router",
    )(att, rec, x, ga, gr, wa, wr, g1, b1, rw, rb)


def _moe_kernel(x_ref, gate_ref, w13_ref, w2_ref, g2_ref, b2_ref, o_ref, xb_sc, acc_sc):
    e = pl.program_id(1)

    @pl.when(e == 0)
    def _():
        xb_sc[...] = x_ref[...].astype(BF16)
        acc_sc[...] = jnp.zeros_like(acc_sc)

    gate = gate_ref[...]
    lane = lax.broadcasted_iota(jnp.int32, gate.shape, 1)
    ge = jnp.sum(jnp.where(lane == e, gate, 0.0), axis=1, keepdims=True)
    h13 = jnp.dot(xb_sc[...], w13_ref[0], preferred_element_type=F32)
    h1 = h13[:, :D_EXPERT]
    h = (h1 * jax.nn.sigmoid(h1)) * h13[:, D_EXPERT:]
    acc_sc[...] += jnp.dot(h.astype(BF16), w2_ref[0], preferred_element_type=F32) * ge

    @pl.when(e == pl.num_programs(1) - 1)
    def _():
        o_ref[...] = _layer_norm(ALPHA * x_ref[...] + acc_sc[...], g2_ref[...], b2_ref[...])


def _moe(x1, gate, w13, w2, g2, b2, tm):
    n = x1.shape[0]
    ne = w13.shape[0]
    return pl.pallas_call(
        _moe_kernel,
        grid=(n // tm, ne),
        in_specs=[
            pl.BlockSpec((tm, D_MODEL), lambda i, e: (i, 0)),
            pl.BlockSpec((tm, LANES), lambda i, e: (i, 0)),
            pl.BlockSpec((1, D_MODEL, 2 * D_EXPERT), lambda i, e: (e, 0, 0)),
            pl.BlockSpec((1, D_EXPERT, D_MODEL), lambda i, e: (e, 0, 0)),
            pl.BlockSpec((1, D_MODEL), lambda i, e: (0, 0)),
            pl.BlockSpec((1, D_MODEL), lambda i, e: (0, 0)),
        ],
        out_specs=pl.BlockSpec((tm, D_MODEL), lambda i, e: (i, 0)),
        out_shape=jax.ShapeDtypeStruct((n, D_MODEL), F32),
        scratch_shapes=[pltpu.VMEM((tm, D_MODEL), BF16), pltpu.VMEM((tm, D_MODEL), F32)],
        compiler_params=_cparams(("parallel", "arbitrary"), VMEM_LIMIT),
        name="moe_experts",
    )(x1, gate, w13, w2, g2, b2)


def _block_diag(w):
    nb, k, j = w.shape
    eye = jnp.eye(nb, dtype=w.dtype)
    return jnp.einsum('nkj,nm->nkmj', w, eye).reshape(nb * k, nb * j)


def kernel(x_prompt, x_sample, cache_k, cache_v, cache_logf, state_h, state_conv, page_table, w_in, b_f, conv_w, conv_b, w_gr, b_gr, w_gi, b_gi, lam, g_att, g_rec, w_out, ln1_g, ln1_b, w_router, router_bias, w1, w3, w2, ws1, ws3, ws2, ln2_g, ln2_b):
    n_pool = cache_k.shape[1]
    x = jnp.concatenate([x_prompt.reshape(N_PROMPT, D_MODEL), x_sample.reshape(N_SAMPLE, D_MODEL)], axis=0)
    esuf, etot = _page_suffix(cache_logf.reshape(DEPTH * n_pool, PAGE_SIZE * H_ATT), 512)
    esuf = esuf.reshape(DEPTH, n_pool, H_ATT, LANES)
    etot = etot.reshape(DEPTH, n_pool, 1, LANES)
    zero_conv = jnp.zeros((BATCH, CONV_W - 1, REC_W), F32)
    zero_h = jnp.zeros((BATCH, 1, REC_W), F32)
    cf = 3 * ATT_W

    outs = {k: [] for k in ("kp", "vp", "fp", "hp", "cp", "ks", "vs", "fs", "hs", "cs")}
    for l in range(DEPTH):
        w_main = jnp.concatenate([w_in[l][:, :cf], w_in[l][:, cf + H_ATT:]], axis=1).astype(BF16)
        w_f = jnp.pad(w_in[l][:, cf:cf + H_ATT], ((0, 0), (0, LANES - H_ATT))).astype(BF16)
        bf_pad = jnp.pad(b_f[l], (0, LANES - H_ATT)).reshape(1, LANES)
        u = _matmul(x, w_main, TM_PROJ, TN_PROJ)
        flog = _forget_logits(x, w_f, bf_pad, TM_PROJ)[:, :H_ATT]
        logf_p = flog[:N_PROMPT].reshape(BATCH, SEQ, H_ATT)
        logf_s = flog[N_PROMPT:].reshape(DEC_BATCH, DEC_SEQ, H_ATT)

        ft = _cumsum_lanes(jnp.swapaxes(logf_p, 1, 2).reshape(BATCH * H_ATT, SEQ), BATCH * H_ATT)
        fr = ft.reshape(BATCH, H_ATT // 2, 2, SEQ)
        fc = jnp.swapaxes(fr, 2, 3)
        att_p = _attn_prompt(u, fc, fr)

        ls_t = jnp.pad(jnp.swapaxes(logf_s, 1, 2), ((0, 0), (0, 0), (0, LANES - DEC_SEQ)))
        fn = _cumsum_lanes(ls_t.reshape(DEC_BATCH * H_ATT, LANES), DEC_BATCH * H_ATT)
        fn = fn.reshape(DEC_BATCH, H_ATT, LANES)
        fq = fn[:, :, :DEC_SEQ].reshape(DEC_BATCH, H_ATT * DEC_SEQ, 1)
        att_s = _attn_sample(l, page_table, u, cache_k, cache_v, esuf, etot, fq, fn)

        wg = jnp.concatenate([_block_diag(w_gr[l]), _block_diag(w_gi[l])], axis=1).astype(BF16)
        bg = jnp.concatenate([b_gr[l], b_gi[l]]).reshape(1, 2 * REC_W)
        rec_args = (conv_w[l], conv_b[l].reshape(1, REC_W), wg, bg, lam[l].reshape(1, REC_W))
        rec_p, h_p, conv_p = _rglru(u, zero_conv, zero_h, *rec_args,
                                    nseq=BATCH, slen=SEQ, tc=TC_REC, row0=0)
        rec_s, h_s, conv_s = _rglru(u, state_conv[l], state_h[l].reshape(DEC_BATCH, 1, REC_W), *rec_args,
                                    nseq=DEC_BATCH, slen=DEC_SEQ, tc=DEC_SEQ, row0=N_PROMPT)

        att = jnp.concatenate([att_p, att_s], axis=0)
        rec = jnp.concatenate([rec_p, rec_s], axis=0)
        rw = jnp.pad(w_router[l], ((0, 0), (0, LANES - N_EXPERTS))).astype(BF16)
        rb = jnp.pad(router_bias[l], (0, LANES - N_EXPERTS)).reshape(1, LANES)
        x1, gate = _out_proj(att, rec, x, g_att[l].reshape(1, ATT_W), g_rec[l].reshape(1, REC_W),
                             w_out[l][:ATT_W].astype(BF16), w_out[l][ATT_W:].astype(BF16),
                             ln1_g[l].reshape(1, D_MODEL), ln1_b[l].reshape(1, D_MODEL),
                             rw, rb, TM_OUT)

        w13 = jnp.concatenate([jnp.concatenate([w1[l], w3[l]], axis=2),
                               jnp.concatenate([ws1[l], ws3[l]], axis=1)[None]], axis=0).astype(BF16)
        w2c = jnp.concatenate([w2[l], ws2[l][None]], axis=0).astype(BF16)
        x = _moe(x1, gate, w13, w2c, ln2_g[l].reshape(1, D_MODEL), ln2_b[l].reshape(1, D_MODEL), TM_MOE)

        kv = lambda rows, c0, shape: u[rows, c0:c0 + ATT_W].reshape(shape)
        p_rows, s_rows = slice(0, N_PROMPT), slice(N_PROMPT, N_TOK)
        p_shape = (BATCH, SEQ, H_ATT, HEAD_DIM)
        s_shape = (DEC_BATCH, DEC_SEQ, H_ATT, HEAD_DIM)
        outs["kp"].append(kv(p_rows, ATT_W, p_shape))
        outs["vp"].append(kv(p_rows, 2 * ATT_W, p_shape))
        outs["fp"].append(logf_p)
        outs["hp"].append(h_p.reshape(BATCH, REC_W))
        outs["cp"].append(conv_p)
        outs["ks"].append(kv(s_rows, ATT_W, s_shape))
        outs["vs"].append(kv(s_rows, 2 * ATT_W, s_shape))
        outs["fs"].append(logf_s)
        outs["hs"].append(h_s.reshape(DEC_BATCH, REC_W))
        outs["cs"].append(conv_s)

    st = lambda name: jnp.stack(outs[name])
    return (x[:N_PROMPT].reshape(BATCH, SEQ, D_MODEL), x[N_PROMPT:].reshape(DEC_BATCH, DEC_SEQ, D_MODEL),
            st("kp"), st("vp"), st("fp"), st("hp"), st("cp"),
            st("ks"), st("vs"), st("fs"), st("hs"), st("cs"))
```

```python
import functools

import jax
import jax.numpy as jnp
from jax import lax
from jax.experimental import pallas as pl
from jax.experimental.pallas import tpu as pltpu

F32 = jnp.float32
BF16 = jnp.bfloat16

D_MODEL = 1024
BATCH = 4
SEQ = 4096
DEPTH = 2
DEC_BATCH = 32
DEC_SEQ = 8
PAST_LEN = 8192
PAGE_SIZE = 128
ATT_W = 512
REC_W = 512
HEAD_DIM = 64
H_ATT = 8
N_REC_BLOCKS = 8
CONV_W = 4
RG_C = 8.0
N_EXPERTS = 64
TOP_K = 8
N_GROUPS = 8
TOPK_GROUPS = 4
D_EXPERT = 256
ROUTED_SCALE = 2.5
ALPHA = (2 * DEPTH) ** 0.25
EPS = 1e-5
ATT_SCALE = HEAD_DIM ** -0.5

N_PROMPT = BATCH * SEQ
N_SAMPLE = DEC_BATCH * DEC_SEQ
N_TOK = N_PROMPT + N_SAMPLE
N_PAGES = PAST_LEN // PAGE_SIZE
U_W = 3 * ATT_W + 2 * REC_W
LANES = 128
NEG = -1e30

TM_PROJ = 1280
TN_PROJ = 640
TM_OUT = 640
TM_MOE = 1280
TQ = 512
TC_REC = 512
PAGES_PER_STEP = 16
VMEM_LIMIT = 56 * 1024 * 1024


def _cparams(sem, vmem=None):
    return pltpu.CompilerParams(dimension_semantics=sem, vmem_limit_bytes=vmem)


def _mm_kernel(x_ref, w_ref, o_ref):
    o_ref[...] = jnp.dot(x_ref[...].astype(BF16), w_ref[...], preferred_element_type=F32)


def _matmul(x, w, tm, tn):
    m, k = x.shape
    n = w.shape[1]
    return pl.pallas_call(
        _mm_kernel,
        grid=(m // tm, n // tn),
        in_specs=[pl.BlockSpec((tm, k), lambda i, j: (i, 0)),
                  pl.BlockSpec((k, tn), lambda i, j: (0, j))],
        out_specs=pl.BlockSpec((tm, tn), lambda i, j: (i, j)),
        out_shape=jax.ShapeDtypeStruct((m, n), F32),
        compiler_params=_cparams(("parallel", "arbitrary"), VMEM_LIMIT),
        name="in_proj",
    )(x, w)


def _forget_kernel(x_ref, w_ref, b_ref, o_ref):
    z = jnp.dot(x_ref[...].astype(BF16), w_ref[...], preferred_element_type=F32) + b_ref[...]
    o_ref[...] = jnp.minimum(z, 0.0) - jnp.log1p(jnp.exp(-jnp.abs(z)))


def _forget_logits(x, w, b, tm):
    m, k = x.shape
    n = w.shape[1]
    return pl.pallas_call(
        _forget_kernel,
        grid=(m // tm,),
        in_specs=[pl.BlockSpec((tm, k), lambda i: (i, 0)),
                  pl.BlockSpec((k, n), lambda i: (0, 0)),
                  pl.BlockSpec((1, n), lambda i: (0, 0))],
        out_specs=pl.BlockSpec((tm, n), lambda i: (i, 0)),
        out_shape=jax.ShapeDtypeStruct((m, n), F32),
        compiler_params=_cparams(("parallel",), VMEM_LIMIT),
        name="forget_logits",
    )(x, w, b)


def _split3(x):
    hi = x.astype(BF16)
    r1 = x - hi.astype(F32)
    mid = r1.astype(BF16)
    lo = (r1 - mid.astype(F32)).astype(BF16)
    return hi, mid, lo


def _cumsum_kernel(x_ref, o_ref, *, nchunk):
    rows = x_ref.shape[0]
    r = lax.broadcasted_iota(jnp.int32, (LANES, LANES), 0)
    c = lax.broadcasted_iota(jnp.int32, (LANES, LANES), 1)
    upper = jnp.where(r <= c, 1.0, 0.0).astype(BF16)
    carry = jnp.zeros((rows, 1), F32)
    for ch in range(nchunk):
        x = x_ref[:, ch * LANES:(ch + 1) * LANES]
        hi, mid, lo = _split3(x)
        w = (jnp.dot(hi, upper, preferred_element_type=F32)
             + jnp.dot(mid, upper, preferred_element_type=F32)
             + jnp.dot(lo, upper, preferred_element_type=F32)) + carry
        o_ref[:, ch * LANES:(ch + 1) * LANES] = w
        carry = w[:, LANES - 1:LANES]


def _cumsum_lanes(x, tr):
    rows, s = x.shape
    return pl.pallas_call(
        functools.partial(_cumsum_kernel, nchunk=s // LANES),
        grid=(rows // tr,),
        in_specs=[pl.BlockSpec((tr, s), lambda i: (i, 0))],
        out_specs=pl.BlockSpec((tr, s), lambda i: (i, 0)),
        out_shape=jax.ShapeDtypeStruct((rows, s), F32),
        compiler_params=_cparams(("parallel",)),
        name="cumsum_lanes",
    )(x)


def _attn_prompt_kernel(q_ref, k_ref, v_ref, fc_ref, fr_ref, o_ref, s_sc):
    qb = pl.program_id(2)
    lane = lax.broadcasted_iota(jnp.int32, (TQ, LANES), 1)
    row = lax.broadcasted_iota(jnp.int32, (TQ, TQ), 0)
    col = lax.broadcasted_iota(jnp.int32, (TQ, TQ), 1)
    q = q_ref[...] * ATT_SCALE
    fc = fc_ref[0, 0]
    outs = []
    for h in range(2):
        in_head = (lane < HEAD_DIM) if h == 0 else (lane >= HEAD_DIM)
        qm = jnp.where(in_head, q, 0.0).astype(BF16)
        fq = fc[:, h:h + 1]

        def scores(kb, carry, masked):
            m, l = carry
            k0 = pl.multiple_of(kb * TQ, TQ)
            k = k_ref[pl.ds(k0, TQ), :].astype(BF16)
            fk = fr_ref[0, 0, h:h + 1, pl.ds(k0, TQ)]
            s = lax.dot_general(qm, k, (((1,), (1,)), ((), ())), preferred_element_type=F32)
            s = s + (fq - fk)
            if masked:
                s = jnp.where(col <= row, s, NEG)
            s_sc[kb] = s
            m_new = jnp.maximum(m, jnp.max(s, axis=1, keepdims=True))
            l = jnp.exp(m - m_new) * l + jnp.sum(jnp.exp(s - m_new), axis=1, keepdims=True)
            return m_new, l

        init = (jnp.full((TQ, 1), NEG, F32), jnp.zeros((TQ, 1), F32))
        carry = lax.fori_loop(0, qb, lambda kb, c: scores(kb, c, False), init)
        m, l = scores(qb, carry, True)

        def values(kb, acc):
            k0 = pl.multiple_of(kb * TQ, TQ)
            v = v_ref[pl.ds(k0, TQ), :].astype(BF16)
            p = jnp.exp(s_sc[kb] - m) / l
            return acc + jnp.dot(p.astype(BF16), v, preferred_element_type=F32)

        outs.append(lax.fori_loop(0, qb + 1, values, jnp.zeros((TQ, LANES), F32)))
    o_ref[...] = jnp.where(lane < HEAD_DIM, outs[0], outs[1])


def _attn_prompt(u, fc, fr):
    nq = SEQ // TQ
    kblk = ATT_W // LANES
    return pl.pallas_call(
        _attn_prompt_kernel,
        grid=(BATCH, H_ATT // 2, nq),
        in_specs=[
            pl.BlockSpec((TQ, LANES), lambda b, p, i: (b * nq + i, p)),
            pl.BlockSpec((SEQ, LANES), lambda b, p, i: (b, kblk + p)),
            pl.BlockSpec((SEQ, LANES), lambda b, p, i: (b, 2 * kblk + p)),
            pl.BlockSpec((1, 1, TQ, 2), lambda b, p, i: (b, p, i, 0)),
            pl.BlockSpec((1, 1, 2, SEQ), lambda b, p, i: (b, p, 0, 0)),
        ],
        out_specs=pl.BlockSpec((TQ, LANES), lambda b, p, i: (b * nq + i, p)),
        out_shape=jax.ShapeDtypeStruct((N_PROMPT, ATT_W), F32),
        scratch_shapes=[pltpu.VMEM((nq, TQ, TQ), F32)],
        compiler_params=_cparams(("parallel", "parallel", "arbitrary"), VMEM_LIMIT),
        name="attn_prompt",
    )(u, u, u, fc, fr)


def _expand_heads(x):
    n = x.shape[1]
    return jnp.concatenate(
        [jnp.broadcast_to(x[h:h + 1, :], (DEC_SEQ, n)) for h in range(H_ATT)], axis=0)


def _attn_sample_kernel(pt_ref, q_ref, kn_ref, vn_ref, fq_ref, fn_ref, *rest):
    del pt_ref
    g = PAGES_PER_STEP
    nstep = N_PAGES // g
    k_refs, v_refs, w_refs = rest[:g], rest[g:2 * g], rest[2 * g:3 * g]
    o_ref = rest[3 * g]
    qbd_sc, s_sc, m_sc, r_sc, acc_sc, suf_sc = rest[3 * g + 1:]
    step = pl.program_id(1)
    nrow = H_ATT * DEC_SEQ
    row = lax.broadcasted_iota(jnp.int32, (nrow, ATT_W), 0)
    colw = lax.broadcasted_iota(jnp.int32, (nrow, ATT_W), 1)
    own_head = (colw >> 6) == (row >> 3)
    fq = fq_ref[0]
    pad = jnp.zeros((PAGE_SIZE - DEC_SEQ, ATT_W), F32)
    nt = (((1,), (1,)), ((), ()))

    @pl.when(step == 0)
    def _():
        q = q_ref[...] * ATT_SCALE
        qt = jnp.concatenate([q] * H_ATT, axis=0)
        qbd_sc[...] = jnp.where(own_head, qt, 0.0).astype(BF16)
        suf_sc[...] = jnp.zeros((H_ATT, 1), F32)
        kn = jnp.concatenate([kn_ref[...], pad], axis=0).astype(BF16)
        s = lax.dot_general(qbd_sc[...], kn, nt, preferred_element_type=F32)
        s = s + (fq - _expand_heads(fn_ref[0]))
        r = lax.broadcasted_iota(jnp.int32, (nrow, PAGE_SIZE), 0)
        c = lax.broadcasted_iota(jnp.int32, (nrow, PAGE_SIZE), 1)
        s = jnp.where(c <= (r & (DEC_SEQ - 1)), s, NEG)
        s_sc[N_PAGES] = s
        m_sc[...] = jnp.max(s, axis=1, keepdims=True)

    @pl.when(step < nstep)
    def _():
        first = (nstep - 1 - step) * g
        qbd = qbd_sc[...]
        suf = suf_sc[...]
        mrun = jnp.full((nrow, PAGE_SIZE), NEG, F32)
        for i in range(g - 1, -1, -1):
            w = w_refs[i][0, 0]
            suf = suf + w[:, PAGE_SIZE - 1:PAGE_SIZE]
            kt = k_refs[i][0, 0].astype(BF16)
            s = jnp.dot(qbd, kt, preferred_element_type=F32)
            s = s + (fq + _expand_heads(suf - w))
            s_sc[first + i] = s
            mrun = jnp.maximum(mrun, s)
        suf_sc[...] = suf
        m_sc[...] = jnp.maximum(m_sc[...], jnp.max(mrun, axis=1, keepdims=True))

    @pl.when(step == nstep - 1)
    def _():
        m = m_sc[...]

        def slot(p, lacc):
            ex = jnp.exp(s_sc[p] - m)
            s_sc[p] = ex
            return lacc + ex

        lacc = lax.fori_loop(0, N_PAGES + 1, slot, jnp.zeros((nrow, PAGE_SIZE), F32))
        r_sc[...] = 1.0 / jnp.sum(lacc, axis=1, keepdims=True)

    @pl.when(step >= nstep)
    def _():
        first = (2 * nstep - 1 - step) * g
        r = r_sc[...]

        @pl.when(step == nstep)
        def _():
            vn = jnp.concatenate([vn_ref[...], pad], axis=0).astype(BF16)
            acc_sc[...] = jnp.dot((s_sc[N_PAGES] * r).astype(BF16), vn, preferred_element_type=F32)

        acc = acc_sc[...]
        for i in range(g - 1, -1, -1):
            vt = v_refs[i][0, 0].astype(BF16)
            p = (s_sc[first + i] * r).astype(BF16)
            acc = acc + lax.dot_general(p, vt, nt, preferred_element_type=F32)
        acc_sc[...] = acc

    @pl.when(step == 2 * nstep - 1)
    def _():
        o = jnp.where(own_head, acc_sc[...], 0.0)
        out = o[0:DEC_SEQ, :]
        for h in range(1, H_ATT):
            out = out + o[h * DEC_SEQ:(h + 1) * DEC_SEQ, :]
        o_ref[...] = out


def _attn_sample(layer, page_table, u, cache_kt, cache_vt, wt, fq, fn):
    g = PAGES_PER_STEP
    nstep = N_PAGES // g
    srow = N_PROMPT // DEC_SEQ
    nrow = H_ATT * DEC_SEQ

    def key_map(i):
        return lambda b, s, pt: (layer, pt[b, (nstep - 1 - jnp.minimum(s, nstep - 1)) * g + i], 0, 0)

    def value_map(i):
        return lambda b, s, pt: (layer, pt[b, (nstep - 1 - jnp.maximum(s - nstep, 0)) * g + i], 0, 0)

    in_specs = [
        pl.BlockSpec((DEC_SEQ, ATT_W), lambda b, s, pt: (srow + b, 0)),
        pl.BlockSpec((DEC_SEQ, ATT_W), lambda b, s, pt: (srow + b, 1)),
        pl.BlockSpec((DEC_SEQ, ATT_W), lambda b, s, pt: (srow + b, 2)),
        pl.BlockSpec((1, nrow, 1), lambda b, s, pt: (b, 0, 0)),
        pl.BlockSpec((1, H_ATT, LANES), lambda b, s, pt: (b, 0, 0)),
    ]
    in_specs += [pl.BlockSpec((1, 1, ATT_W, PAGE_SIZE), key_map(i)) for i in range(g)]
    in_specs += [pl.BlockSpec((1, 1, ATT_W, PAGE_SIZE), value_map(i)) for i in range(g)]
    in_specs += [pl.BlockSpec((1, 1, H_ATT, PAGE_SIZE), key_map(i)) for i in range(g)]
    grid_spec = pltpu.PrefetchScalarGridSpec(
        num_scalar_prefetch=1,
        grid=(DEC_BATCH, 2 * nstep),
        in_specs=in_specs,
        out_specs=pl.BlockSpec((DEC_SEQ, ATT_W), lambda b, s, pt: (b, 0)),
        scratch_shapes=[
            pltpu.VMEM((nrow, ATT_W), BF16),
            pltpu.VMEM((N_PAGES + 1, nrow, PAGE_SIZE), F32),
            pltpu.VMEM((nrow, 1), F32),
            pltpu.VMEM((nrow, 1), F32),
            pltpu.VMEM((nrow, ATT_W), F32),
            pltpu.VMEM((H_ATT, 1), F32),
        ],
    )
    return pl.pallas_call(
        _attn_sample_kernel,
        grid_spec=grid_spec,
        out_shape=jax.ShapeDtypeStruct((N_SAMPLE, ATT_W), F32),
        compiler_params=_cparams(("parallel", "arbitrary"), VMEM_LIMIT),
        name="attn_sample",
    )(page_table, u, u, u, fq, fn, *([cache_kt] * g), *([cache_vt] * g), *([wt] * g))


def _rglru_kernel(xr_ref, gr_ref, cs_ref, h0_ref, cw_ref, cb_ref, wg_ref, bg_ref, lam_ref,
                  y_ref, hN_ref, cN_ref, xbuf, abuf, bbuf, hbuf, hcar, *, tc):
    c = pl.program_id(1)
    x = xr_ref[...]

    @pl.when(c == 0)
    def _():
        xbuf[0:8, :] = jnp.zeros((8, REC_W), F32)
        xbuf[8 - (CONV_W - 1):8, :] = cs_ref[0]
        hcar[...] = h0_ref[0]

    xbuf[8:8 + tc, :] = x
    xc = cb_ref[...] + xbuf[5:5 + tc, :] * cw_ref[0:1, :]
    xc = xc + xbuf[6:6 + tc, :] * cw_ref[1:2, :]
    xc = xc + xbuf[7:7 + tc, :] * cw_ref[2:3, :]
    xc = xc + x * cw_ref[3:4, :]
    xbuf[0:8, :] = xbuf[tc:tc + 8, :]

    gates = jnp.dot(xc.astype(BF16), wg_ref[...], preferred_element_type=F32) + bg_ref[...]
    rg = jax.nn.sigmoid(gates[:, :REC_W])
    ig = jax.nn.sigmoid(gates[:, REC_W:])
    nl = -lam_ref[...]
    softplus = jnp.maximum(nl, 0.0) + jnp.log1p(jnp.exp(-jnp.abs(nl)))
    log_a = (-RG_C * softplus) * rg
    a = jnp.exp(log_a)
    abuf[...] = a
    bbuf[...] = jnp.sqrt(-jnp.tanh(log_a) * (a * a + 1.0)) * (ig * xc)

    def group(gi, h):
        base = pl.multiple_of(gi * 8, 8)
        a8 = abuf[pl.ds(base, 8), :]
        b8 = bbuf[pl.ds(base, 8), :]
        rows = []
        for r in range(8):
            h = a8[r:r + 1, :] * h + b8[r:r + 1, :]
            rows.append(h)
        hbuf[pl.ds(base, 8), :] = jnp.concatenate(rows, axis=0)
        return h

    h_last = lax.fori_loop(0, tc // 8, group, hcar[...])
    hcar[...] = h_last
    y_ref[...] = hbuf[...] * jax.nn.gelu(gr_ref[...], approximate=True)

    @pl.when(c == pl.num_programs(1) - 1)
    def _():
        hN_ref[0] = h_last
        cN_ref[0] = x[tc - (CONV_W - 1):tc, :]


def _rglru(u, conv_state, h0, cw, cb, wg, bg, lam, *, nseq, slen, tc, row0):
    nchunk = slen // tc
    blk0 = row0 // tc
    xcol = (3 * ATT_W) // REC_W
    rep = lambda shape: pl.BlockSpec(shape, lambda b, c: (0,) * len(shape))
    return pl.pallas_call(
        functools.partial(_rglru_kernel, tc=tc),
        grid=(nseq, nchunk),
        in_specs=[
            pl.BlockSpec((tc, REC_W), lambda b, c: (blk0 + b * nchunk + c, xcol)),
            pl.BlockSpec((tc, REC_W), lambda b, c: (blk0 + b * nchunk + c, xcol + 1)),
            pl.BlockSpec((1, CONV_W - 1, REC_W), lambda b, c: (b, 0, 0)),
            pl.BlockSpec((1, 1, REC_W), lambda b, c: (b, 0, 0)),
            rep((CONV_W, REC_W)), rep((1, REC_W)), rep((REC_W, 2 * REC_W)),
            rep((1, 2 * REC_W)), rep((1, REC_W)),
        ],
        out_specs=[
            pl.BlockSpec((tc, REC_W), lambda b, c: (b * nchunk + c, 0)),
            pl.BlockSpec((1, 1, REC_W), lambda b, c: (b, 0, 0)),
            pl.BlockSpec((1, CONV_W - 1, REC_W), lambda b, c: (b, 0, 0)),
        ],
        out_shape=[
            jax.ShapeDtypeStruct((nseq * slen, REC_W), F32),
            jax.ShapeDtypeStruct((nseq, 1, REC_W), F32),
            jax.ShapeDtypeStruct((nseq, CONV_W - 1, REC_W), F32),
        ],
        scratch_shapes=[
            pltpu.VMEM((tc + 8, REC_W), F32),
            pltpu.VMEM((tc, REC_W), F32),
            pltpu.VMEM((tc, REC_W), F32),
            pltpu.VMEM((tc, REC_W), F32),
            pltpu.VMEM((1, REC_W), F32),
        ],
        compiler_params=_cparams(("parallel", "arbitrary"), VMEM_LIMIT),
        name="rglru",
    )(u, u, conv_state, h0, cw, cb, wg, bg, lam)


def _layer_norm(x, g, b):
    mu = jnp.mean(x, axis=-1, keepdims=True)
    xc = x - mu
    var = jnp.mean(xc * xc, axis=-1, keepdims=True)
    return xc * lax.rsqrt(var + EPS) * g + b


def _rms_norm(x, g):
    return x * lax.rsqrt(jnp.mean(x * x, axis=-1, keepdims=True) + EPS) * g


def _route(logits, rbias):
    shape = logits.shape
    ninf = -jnp.inf
    lane = lax.broadcasted_iota(jnp.int32, shape, 1)
    lane_f = lane.astype(F32)
    gsz = N_EXPERTS // N_GROUPS
    grp_f = (lane >> 3).astype(F32)
    valid = lane < N_EXPERTS
    scores = jax.nn.sigmoid(logits)
    biased = jnp.where(valid, scores + rbias, ninf)

    def first_max(vals, idx_f):
        m = jnp.max(vals, axis=1, keepdims=True)
        i = jnp.min(jnp.where(vals == m, idx_f, 1e9), axis=1, keepdims=True)
        return m, i

    gs = jnp.full(shape, ninf, F32)
    for g in range(N_GROUPS):
        in_g = (lane >= g * gsz) & (lane < (g + 1) * gsz)
        v = jnp.where(in_g, biased, ninf)
        m1, i1 = first_max(v, lane_f)
        m2 = jnp.max(jnp.where(lane_f == i1, ninf, v), axis=1, keepdims=True)
        gs = jnp.where(in_g, m1 + m2, gs)
    keep = jnp.zeros(shape, F32)
    for _ in range(TOPK_GROUPS):
        _, gi = first_max(gs, grp_f)
        pick = grp_f == gi
        keep = jnp.where(pick, 1.0, keep)
        gs = jnp.where(pick, ninf, gs)
    cand = jnp.where(keep > 0.0, biased, ninf)
    sel = jnp.zeros(shape, F32)
    for _ in range(TOP_K):
        _, ei = first_max(cand, lane_f)
        pick = lane_f == ei
        sel = jnp.where(pick, 1.0, sel)
        cand = jnp.where(pick, ninf, cand)
    picked = jnp.where(sel > 0.0, scores, 0.0)
    gate = picked / jnp.sum(picked, axis=1, keepdims=True) * ROUTED_SCALE
    return jnp.where(lane == N_EXPERTS, 1.0, gate)


def _out_kernel(att_ref, rec_ref, x_ref, ga_ref, gr_ref, wa_ref, wr_ref, g1_ref, b1_ref,
                rw_ref, rb_ref, x1_ref, gate_ref):
    na = _rms_norm(att_ref[...], ga_ref[...]).astype(BF16)
    nr = _rms_norm(rec_ref[...], gr_ref[...]).astype(BF16)
    mix = (jnp.dot(na, wa_ref[...], preferred_element_type=F32)
           + jnp.dot(nr, wr_ref[...], preferred_element_type=F32))
    x1 = _layer_norm(ALPHA * x_ref[...] + mix, g1_ref[...], b1_ref[...])
    x1_ref[...] = x1
    logits = jnp.dot(x1.astype(BF16), rw_ref[...], preferred_element_type=F32)
    gate_ref[...] = _route(logits, rb_ref[...])


def _out_proj(att, rec, x, ga, gr, wa, wr, g1, b1, rw, rb, tm):
    n = x.shape[0]
    rep = lambda shape: pl.BlockSpec(shape, lambda i: (0,) * len(shape))
    return pl.pallas_call(
        _out_kernel,
        grid=(n // tm,),
        in_specs=[
            pl.BlockSpec((tm, ATT_W), lambda i: (i, 0)),
            pl.BlockSpec((tm, REC_W), lambda i: (i, 0)),
            pl.BlockSpec((tm, D_MODEL), lambda i: (i, 0)),
            rep((1, ATT_W)), rep((1, REC_W)),
            rep((ATT_W, D_MODEL)), rep((REC_W, D_MODEL)),
            rep((1, D_MODEL)), rep((1, D_MODEL)),
            rep((D_MODEL, LANES)), rep((1, LANES)),
        ],
        out_specs=[pl.BlockSpec((tm, D_MODEL), lambda i: (i, 0)),
                   pl.BlockSpec((tm, LANES), lambda i: (i, 0))],
        out_shape=[jax.ShapeDtypeStruct((n, D_MODEL), F32),
                   jax.ShapeDtypeStruct((n, LANES), F32)],
        compiler_params=_cparams(("parallel",), VMEM_LIMIT),
        name="out_proj_router",
    )(att, rec, x, ga, gr, wa, wr, g1, b1, rw, rb)


def _moe_kernel(x_ref, gate_ref, w13_ref, w2_ref, g2_ref, b2_ref, o_ref, xb_sc, acc_sc):
    e = pl.program_id(1)

    @pl.when(e == 0)
    def _():
        xb_sc[...] = x_ref[...].astype(BF16)
        acc_sc[...] = jnp.zeros_like(acc_sc)

    gate = gate_ref[...]
    lane = lax.broadcasted_iota(jnp.int32, gate.shape, 1)
    ge = jnp.sum(jnp.where(lane == e, gate, 0.0), axis=1, keepdims=True)
    h13 = jnp.dot(xb_sc[...], w13_ref[0], preferred_element_type=F32)
    h1 = h13[:, :D_EXPERT]
    h = (h1 * jax.nn.sigmoid(h1)) * h13[:, D_EXPERT:]
    acc_sc[...] += jnp.dot(h.astype(BF16), w2_ref[0], preferred_element_type=F32) * ge

    @pl.when(e == pl.num_programs(1) - 1)
    def _():
        o_ref[...] = _layer_norm(ALPHA * x_ref[...] + acc_sc[...], g2_ref[...], b2_ref[...])


def _moe(x1, gate, w13, w2, g2, b2, tm):
    n = x1.shape[0]
    ne = w13.shape[0]
    return pl.pallas_call(
        _moe_kernel,
        grid=(n // tm, ne),
        in_specs=[
            pl.BlockSpec((tm, D_MODEL), lambda i, e: (i, 0)),
            pl.BlockSpec((tm, LANES), lambda i, e: (i, 0)),
            pl.BlockSpec((1, D_MODEL, 2 * D_EXPERT), lambda i, e: (e, 0, 0)),
            pl.BlockSpec((1, D_EXPERT, D_MODEL), lambda i, e: (e, 0, 0)),
            pl.BlockSpec((1, D_MODEL), lambda i, e: (0, 0)),
            pl.BlockSpec((1, D_MODEL), lambda i, e: (0, 0)),
        ],
        out_specs=pl.BlockSpec((tm, D_MODEL), lambda i, e: (i, 0)),
        out_shape=jax.ShapeDtypeStruct((n, D_MODEL), F32),
        scratch_shapes=[pltpu.VMEM((tm, D_MODEL), BF16), pltpu.VMEM((tm, D_MODEL), F32)],
        compiler_params=_cparams(("parallel", "arbitrary"), VMEM_LIMIT),
        name="moe_experts",
    )(x1, gate, w13, w2, g2, b2)


def _block_diag(w):
    nb, k, j = w.shape
    eye = jnp.eye(nb, dtype=w.dtype)
    return jnp.einsum('nkj,nm->nkmj', w, eye).reshape(nb * k, nb * j)


def kernel(x_prompt, x_sample, cache_k, cache_v, cache_logf, state_h, state_conv, page_table, w_in, b_f, conv_w, conv_b, w_gr, b_gr, w_gi, b_gi, lam, g_att, g_rec, w_out, ln1_g, ln1_b, w_router, router_bias, w1, w3, w2, ws1, ws3, ws2, ln2_g, ln2_b):
    n_pool = cache_k.shape[1]
    x = jnp.concatenate([x_prompt.reshape(N_PROMPT, D_MODEL), x_sample.reshape(N_SAMPLE, D_MODEL)], axis=0)
    ckt = jnp.transpose(cache_k, (0, 1, 3, 4, 2)).reshape(DEPTH, n_pool, ATT_W, PAGE_SIZE)
    cvt = jnp.transpose(cache_v, (0, 1, 3, 4, 2)).reshape(DEPTH, n_pool, ATT_W, PAGE_SIZE)
    lf_t = jnp.swapaxes(cache_logf, -1, -2).reshape(DEPTH * n_pool * H_ATT, PAGE_SIZE)
    wt = _cumsum_lanes(lf_t, 4096).reshape(DEPTH, n_pool, H_ATT, PAGE_SIZE)
    zero_conv = jnp.zeros((BATCH, CONV_W - 1, REC_W), F32)
    zero_h = jnp.zeros((BATCH, 1, REC_W), F32)
    cf = 3 * ATT_W

    outs = {k: [] for k in ("kp", "vp", "fp", "hp", "cp", "ks", "vs", "fs", "hs", "cs")}
    for l in range(DEPTH):
        w_main = jnp.concatenate([w_in[l][:, :cf], w_in[l][:, cf + H_ATT:]], axis=1).astype(BF16)
        w_f = jnp.pad(w_in[l][:, cf:cf + H_ATT], ((0, 0), (0, LANES - H_ATT))).astype(BF16)
        bf_pad = jnp.pad(b_f[l], (0, LANES - H_ATT)).reshape(1, LANES)
        u = _matmul(x, w_main, TM_PROJ, TN_PROJ)
        flog = _forget_logits(x, w_f, bf_pad, TM_PROJ)[:, :H_ATT]
        logf_p = flog[:N_PROMPT].reshape(BATCH, SEQ, H_ATT)
        logf_s = flog[N_PROMPT:].reshape(DEC_BATCH, DEC_SEQ, H_ATT)

        ft = _cumsum_lanes(jnp.swapaxes(logf_p, 1, 2).reshape(BATCH * H_ATT, SEQ), BATCH * H_ATT)
        fr = ft.reshape(BATCH, H_ATT // 2, 2, SEQ)
        fc = jnp.swapaxes(fr, 2, 3)
        att_p = _attn_prompt(u, fc, fr)

        ls_t = jnp.pad(jnp.swapaxes(logf_s, 1, 2), ((0, 0), (0, 0), (0, LANES - DEC_SEQ)))
        fn = _cumsum_lanes(ls_t.reshape(DEC_BATCH * H_ATT, LANES), DEC_BATCH * H_ATT)
        fn = fn.reshape(DEC_BATCH, H_ATT, LANES)
        fq = fn[:, :, :DEC_SEQ].reshape(DEC_BATCH, H_ATT * DEC_SEQ, 1)
        att_s = _attn_sample(l, page_table, u, ckt, cvt, wt, fq, fn)

        wg = jnp.concatenate([_block_diag(w_gr[l]), _block_diag(w_gi[l])], axis=1).astype(BF16)
        bg = jnp.concatenate([b_gr[l], b_gi[l]]).reshape(1, 2 * REC_W)
        rec_args = (conv_w[l], conv_b[l].reshape(1, REC_W), wg, bg, lam[l].reshape(1, REC_W))
        rec_p, h_p, conv_p = _rglru(u, zero_conv, zero_h, *rec_args,
                                    nseq=BATCH, slen=SEQ, tc=TC_REC, row0=0)
        rec_s, h_s, conv_s = _rglru(u, state_conv[l], state_h[l].reshape(DEC_BATCH, 1, REC_W), *rec_args,
                                    nseq=DEC_BATCH, slen=DEC_SEQ, tc=DEC_SEQ, row0=N_PROMPT)

        att = jnp.concatenate([att_p, att_s], axis=0)
        rec = jnp.concatenate([rec_p, rec_s], axis=0)
        rw = jnp.pad(w_router[l], ((0, 0), (0, LANES - N_EXPERTS))).astype(BF16)
        rb = jnp.pad(router_bias[l], (0, LANES - N_EXPERTS)).reshape(1, LANES)
        x1, gate = _out_proj(att, rec, x, g_att[l].reshape(1, ATT_W), g_rec[l].reshape(1, REC_W),
                             w_out[l][:ATT_W].astype(BF16), w_out[l][ATT_W:].astype(BF16),
                             ln1_g[l].reshape(1, D_MODEL), ln1_b[l].reshape(1, D_MODEL),
                             rw, rb, TM_OUT)

        w13 = jnp.concatenate([jnp.concatenate([w1[l], w3[l]], axis=2),
                               jnp.concatenate([ws1[l], ws3[l]], axis=1)[None]], axis=0).astype(BF16)
        w2c = jnp.concatenate([w2[l], ws2[l][None]], axis=0).astype(BF16)
        x = _moe(x1, gate, w13, w2c, ln2_g[l].reshape(1, D_MODEL), ln2_b[l].reshape(1, D_MODEL), TM_MOE)

        kv = lambda rows, c0, shape: u[rows, c0:c0 + ATT_W].reshape(shape)
        p_rows, s_rows = slice(0, N_PROMPT), slice(N_PROMPT, N_TOK)
        p_shape = (BATCH, SEQ, H_ATT, HEAD_DIM)
        s_shape = (DEC_BATCH, DEC_SEQ, H_ATT, HEAD_DIM)
        outs["kp"].append(kv(p_rows, ATT_W, p_shape))
        outs["vp"].append(kv(p_rows, 2 * ATT_W, p_shape))
        outs["fp"].append(logf_p)
        outs["hp"].append(h_p.reshape(BATCH, REC_W))
        outs["cp"].append(conv_p)
        outs["ks"].append(kv(s_rows, ATT_W, s_shape))
        outs["vs"].append(kv(s_rows, 2 * ATT_W, s_shape))
        outs["fs"].append(logf_s)
        outs["hs"].append(h_s.reshape(DEC_BATCH, REC_W))
        outs["cs"].append(conv_s)

    st = lambda name: jnp.stack(outs[name])
    return (x[:N_PROMPT].reshape(BATCH, SEQ, D_MODEL), x[N_PROMPT:].reshape(DEC_BATCH, DEC_SEQ, D_MODEL),
            st("kp"), st("vp"), st("fp"), st("hp"), st("cp"),
            st("ks"), st("vs"), st("fs"), st("hs"), st("cs"))
```

```python
import functools

import jax
import jax.numpy as jnp
from jax import lax
from jax.experimental import pallas as pl
from jax.experimental.pallas import tpu as pltpu

F32 = jnp.float32
BF16 = jnp.bfloat16

D_MODEL = 1024
BATCH = 4
SEQ = 4096
DEPTH = 2
DEC_BATCH = 32
DEC_SEQ = 8
PAST_LEN = 8192
PAGE_SIZE = 128
ATT_W = 512
REC_W = 512
HEAD_DIM = 64
H_ATT = 8
N_REC_BLOCKS = 8
CONV_W = 4
RG_C = 8.0
N_EXPERTS = 64
TOP_K = 8
N_GROUPS = 8
TOPK_GROUPS = 4
D_EXPERT = 256
ROUTED_SCALE = 2.5
ALPHA = (2 * DEPTH) ** 0.25
EPS = 1e-5
ATT_SCALE = HEAD_DIM ** -0.5

N_PROMPT = BATCH * SEQ
N_SAMPLE = DEC_BATCH * DEC_SEQ
N_TOK = N_PROMPT + N_SAMPLE
N_PAGES = PAST_LEN // PAGE_SIZE
U_W = 3 * ATT_W + 2 * REC_W
LANES = 128
NEG = -1e30

TM_PROJ = 1280
TN_PROJ = 640
TT = 256
ROW_Q = 16
SEG_MAX = TT
LC = TT * TOP_K + N_EXPERTS * ROW_Q
LCP = LC + SEG_MAX
TG = 512
CHUNK_BITS = (256, 128, 64, 32, 16)
N_TILES = N_TOK // TT
P_ROWS = -(-(N_TOK * TOP_K + N_TILES * N_EXPERTS * (ROW_Q - 1) + N_EXPERTS * (TG - 1)) // TG) * TG
TQ = 512
TC_REC = 512
PAGES_PER_STEP = 16
VMEM_LIMIT = 56 * 1024 * 1024


def _cparams(sem, vmem=None):
    return pltpu.CompilerParams(dimension_semantics=sem, vmem_limit_bytes=vmem)


def _mm_kernel(x_ref, w_ref, o_ref):
    o_ref[...] = jnp.dot(x_ref[...].astype(BF16), w_ref[...], preferred_element_type=F32)


def _matmul(x, w, tm, tn):
    m, k = x.shape
    n = w.shape[1]
    return pl.pallas_call(
        _mm_kernel,
        grid=(m // tm, n // tn),
        in_specs=[pl.BlockSpec((tm, k), lambda i, j: (i, 0)),
                  pl.BlockSpec((k, tn), lambda i, j: (0, j))],
        out_specs=pl.BlockSpec((tm, tn), lambda i, j: (i, j)),
        out_shape=jax.ShapeDtypeStruct((m, n), F32),
        compiler_params=_cparams(("parallel", "arbitrary"), VMEM_LIMIT),
        name="in_proj",
    )(x, w)


def _forget_kernel(x_ref, w_ref, b_ref, o_ref):
    z = jnp.dot(x_ref[...].astype(BF16), w_ref[...], preferred_element_type=F32) + b_ref[...]
    o_ref[...] = jnp.minimum(z, 0.0) - jnp.log1p(jnp.exp(-jnp.abs(z)))


def _forget_logits(x, w, b, tm):
    m, k = x.shape
    n = w.shape[1]
    return pl.pallas_call(
        _forget_kernel,
        grid=(m // tm,),
        in_specs=[pl.BlockSpec((tm, k), lambda i: (i, 0)),
                  pl.BlockSpec((k, n), lambda i: (0, 0)),
                  pl.BlockSpec((1, n), lambda i: (0, 0))],
        out_specs=pl.BlockSpec((tm, n), lambda i: (i, 0)),
        out_shape=jax.ShapeDtypeStruct((m, n), F32),
        compiler_params=_cparams(("parallel",), VMEM_LIMIT),
        name="forget_logits",
    )(x, w, b)


def _split3(x):
    hi = x.astype(BF16)
    r1 = x - hi.astype(F32)
    mid = r1.astype(BF16)
    lo = (r1 - mid.astype(F32)).astype(BF16)
    return hi, mid, lo


def _cumsum_kernel(x_ref, o_ref, *, nchunk):
    rows = x_ref.shape[0]
    r = lax.broadcasted_iota(jnp.int32, (LANES, LANES), 0)
    c = lax.broadcasted_iota(jnp.int32, (LANES, LANES), 1)
    upper = jnp.where(r <= c, 1.0, 0.0).astype(BF16)
    carry = jnp.zeros((rows, 1), F32)
    for ch in range(nchunk):
        x = x_ref[:, ch * LANES:(ch + 1) * LANES]
        hi, mid, lo = _split3(x)
        w = (jnp.dot(hi, upper, preferred_element_type=F32)
             + jnp.dot(mid, upper, preferred_element_type=F32)
             + jnp.dot(lo, upper, preferred_element_type=F32)) + carry
        o_ref[:, ch * LANES:(ch + 1) * LANES] = w
        carry = w[:, LANES - 1:LANES]


def _cumsum_lanes(x, tr):
    rows, s = x.shape
    return pl.pallas_call(
        functools.partial(_cumsum_kernel, nchunk=s // LANES),
        grid=(rows // tr,),
        in_specs=[pl.BlockSpec((tr, s), lambda i: (i, 0))],
        out_specs=pl.BlockSpec((tr, s), lambda i: (i, 0)),
        out_shape=jax.ShapeDtypeStruct((rows, s), F32),
        compiler_params=_cparams(("parallel",)),
        name="cumsum_lanes",
    )(x)


def _attn_prompt_kernel(q_ref, k_ref, v_ref, fc_ref, fr_ref, o_ref, s_sc):
    qb = pl.program_id(2)
    lane = lax.broadcasted_iota(jnp.int32, (TQ, LANES), 1)
    row = lax.broadcasted_iota(jnp.int32, (TQ, TQ), 0)
    col = lax.broadcasted_iota(jnp.int32, (TQ, TQ), 1)
    q = q_ref[...] * ATT_SCALE
    fc = fc_ref[0, 0]
    outs = []
    for h in range(2):
        in_head = (lane < HEAD_DIM) if h == 0 else (lane >= HEAD_DIM)
        qm = jnp.where(in_head, q, 0.0).astype(BF16)
        fq = fc[:, h:h + 1]

        def scores(kb, carry, masked):
            m, l = carry
            k0 = pl.multiple_of(kb * TQ, TQ)
            k = k_ref[pl.ds(k0, TQ), :].astype(BF16)
            fk = fr_ref[0, 0, h:h + 1, pl.ds(k0, TQ)]
            s = lax.dot_general(qm, k, (((1,), (1,)), ((), ())), preferred_element_type=F32)
            s = s + (fq - fk)
            if masked:
                s = jnp.where(col <= row, s, NEG)
            s_sc[kb] = s
            m_new = jnp.maximum(m, jnp.max(s, axis=1, keepdims=True))
            l = jnp.exp(m - m_new) * l + jnp.sum(jnp.exp(s - m_new), axis=1, keepdims=True)
            return m_new, l

        init = (jnp.full((TQ, 1), NEG, F32), jnp.zeros((TQ, 1), F32))
        carry = lax.fori_loop(0, qb, lambda kb, c: scores(kb, c, False), init)
        m, l = scores(qb, carry, True)

        def values(kb, acc):
            k0 = pl.multiple_of(kb * TQ, TQ)
            v = v_ref[pl.ds(k0, TQ), :].astype(BF16)
            p = jnp.exp(s_sc[kb] - m) / l
            return acc + jnp.dot(p.astype(BF16), v, preferred_element_type=F32)

        outs.append(lax.fori_loop(0, qb + 1, values, jnp.zeros((TQ, LANES), F32)))
    o_ref[...] = jnp.where(lane < HEAD_DIM, outs[0], outs[1])


def _attn_prompt(u, fc, fr):
    nq = SEQ // TQ
    kblk = ATT_W // LANES
    return pl.pallas_call(
        _attn_prompt_kernel,
        grid=(BATCH, H_ATT // 2, nq),
        in_specs=[
            pl.BlockSpec((TQ, LANES), lambda b, p, i: (b * nq + i, p)),
            pl.BlockSpec((SEQ, LANES), lambda b, p, i: (b, kblk + p)),
            pl.BlockSpec((SEQ, LANES), lambda b, p, i: (b, 2 * kblk + p)),
            pl.BlockSpec((1, 1, TQ, 2), lambda b, p, i: (b, p, i, 0)),
            pl.BlockSpec((1, 1, 2, SEQ), lambda b, p, i: (b, p, 0, 0)),
        ],
        out_specs=pl.BlockSpec((TQ, LANES), lambda b, p, i: (b * nq + i, p)),
        out_shape=jax.ShapeDtypeStruct((N_PROMPT, ATT_W), F32),
        scratch_shapes=[pltpu.VMEM((nq, TQ, TQ), F32)],
        compiler_params=_cparams(("parallel", "parallel", "arbitrary"), VMEM_LIMIT),
        name="attn_prompt",
    )(u, u, u, fc, fr)


def _expand_heads(x):
    n = x.shape[1]
    return jnp.concatenate(
        [jnp.broadcast_to(x[h:h + 1, :], (DEC_SEQ, n)) for h in range(H_ATT)], axis=0)


def _attn_sample_kernel(pt_ref, q_ref, kn_ref, vn_ref, fq_ref, fn_ref, *rest):
    del pt_ref
    g = PAGES_PER_STEP
    nstep = N_PAGES // g
    k_refs, v_refs, w_refs = rest[:g], rest[g:2 * g], rest[2 * g:3 * g]
    o_ref = rest[3 * g]
    qbd_sc, s_sc, m_sc, r_sc, acc_sc, suf_sc = rest[3 * g + 1:]
    step = pl.program_id(1)
    nrow = H_ATT * DEC_SEQ
    row = lax.broadcasted_iota(jnp.int32, (nrow, ATT_W), 0)
    colw = lax.broadcasted_iota(jnp.int32, (nrow, ATT_W), 1)
    own_head = (colw >> 6) == (row >> 3)
    fq = fq_ref[0]
    pad = jnp.zeros((PAGE_SIZE - DEC_SEQ, ATT_W), F32)
    nt = (((1,), (1,)), ((), ()))

    @pl.when(step == 0)
    def _():
        q = q_ref[...] * ATT_SCALE
        qt = jnp.concatenate([q] * H_ATT, axis=0)
        qbd_sc[...] = jnp.where(own_head, qt, 0.0).astype(BF16)
        suf_sc[...] = jnp.zeros((H_ATT, 1), F32)
        kn = jnp.concatenate([kn_ref[...], pad], axis=0).astype(BF16)
        s = lax.dot_general(qbd_sc[...], kn, nt, preferred_element_type=F32)
        s = s + (fq - _expand_heads(fn_ref[0]))
        r = lax.broadcasted_iota(jnp.int32, (nrow, PAGE_SIZE), 0)
        c = lax.broadcasted_iota(jnp.int32, (nrow, PAGE_SIZE), 1)
        s = jnp.where(c <= (r & (DEC_SEQ - 1)), s, NEG)
        s_sc[N_PAGES] = s
        m_sc[...] = jnp.max(s, axis=1, keepdims=True)

    @pl.when(step < nstep)
    def _():
        first = (nstep - 1 - step) * g
        qbd = qbd_sc[...]
        suf = suf_sc[...]
        mrun = jnp.full((nrow, PAGE_SIZE), NEG, F32)
        for i in range(g - 1, -1, -1):
            w = w_refs[i][0, 0]
            suf = suf + w[:, PAGE_SIZE - 1:PAGE_SIZE]
            kt = k_refs[i][0, 0].astype(BF16)
            s = jnp.dot(qbd, kt, preferred_element_type=F32)
            s = s + (fq + _expand_heads(suf - w))
            s_sc[first + i] = s
            mrun = jnp.maximum(mrun, s)
        suf_sc[...] = suf
        m_sc[...] = jnp.maximum(m_sc[...], jnp.max(mrun, axis=1, keepdims=True))

    @pl.when(step == nstep - 1)
    def _():
        m = m_sc[...]

        def slot(p, lacc):
            ex = jnp.exp(s_sc[p] - m)
            s_sc[p] = ex
            return lacc + ex

        lacc = lax.fori_loop(0, N_PAGES + 1, slot, jnp.zeros((nrow, PAGE_SIZE), F32))
        r_sc[...] = 1.0 / jnp.sum(lacc, axis=1, keepdims=True)

    @pl.when(step >= nstep)
    def _():
        first = (2 * nstep - 1 - step) * g
        r = r_sc[...]

        @pl.when(step == nstep)
        def _():
            vn = jnp.concatenate([vn_ref[...], pad], axis=0).astype(BF16)
            acc_sc[...] = jnp.dot((s_sc[N_PAGES] * r).astype(BF16), vn, preferred_element_type=F32)

        acc = acc_sc[...]
        for i in range(g - 1, -1, -1):
            vt = v_refs[i][0, 0].astype(BF16)
            p = (s_sc[first + i] * r).astype(BF16)
            acc = acc + lax.dot_general(p, vt, nt, preferred_element_type=F32)
        acc_sc[...] = acc

    @pl.when(step == 2 * nstep - 1)
    def _():
        o = jnp.where(own_head, acc_sc[...], 0.0)
        out = o[0:DEC_SEQ, :]
        for h in range(1, H_ATT):
            out = out + o[h * DEC_SEQ:(h + 1) * DEC_SEQ, :]
        o_ref[...] = out


def _attn_sample(layer, page_table, u, cache_kt, cache_vt, wt, fq, fn):
    g = PAGES_PER_STEP
    nstep = N_PAGES // g
    srow = N_PROMPT // DEC_SEQ
    nrow = H_ATT * DEC_SEQ

    def key_map(i):
        return lambda b, s, pt: (layer, pt[b, (nstep - 1 - jnp.minimum(s, nstep - 1)) * g + i], 0, 0)

    def value_map(i):
        return lambda b, s, pt: (layer, pt[b, (nstep - 1 - jnp.maximum(s - nstep, 0)) * g + i], 0, 0)

    in_specs = [
        pl.BlockSpec((DEC_SEQ, ATT_W), lambda b, s, pt: (srow + b, 0)),
        pl.BlockSpec((DEC_SEQ, ATT_W), lambda b, s, pt: (srow + b, 1)),
        pl.BlockSpec((DEC_SEQ, ATT_W), lambda b, s, pt: (srow + b, 2)),
        pl.BlockSpec((1, nrow, 1), lambda b, s, pt: (b, 0, 0)),
        pl.BlockSpec((1, H_ATT, LANES), lambda b, s, pt: (b, 0, 0)),
    ]
    in_specs += [pl.BlockSpec((1, 1, ATT_W, PAGE_SIZE), key_map(i)) for i in range(g)]
    in_specs += [pl.BlockSpec((1, 1, ATT_W, PAGE_SIZE), value_map(i)) for i in range(g)]
    in_specs += [pl.BlockSpec((1, 1, H_ATT, PAGE_SIZE), key_map(i)) for i in range(g)]
    grid_spec = pltpu.PrefetchScalarGridSpec(
        num_scalar_prefetch=1,
        grid=(DEC_BATCH, 2 * nstep),
        in_specs=in_specs,
        out_specs=pl.BlockSpec((DEC_SEQ, ATT_W), lambda b, s, pt: (b, 0)),
        scratch_shapes=[
            pltpu.VMEM((nrow, ATT_W), BF16),
            pltpu.VMEM((N_PAGES + 1, nrow, PAGE_SIZE), F32),
            pltpu.VMEM((nrow, 1), F32),
            pltpu.VMEM((nrow, 1), F32),
            pltpu.VMEM((nrow, ATT_W), F32),
            pltpu.VMEM((H_ATT, 1), F32),
        ],
    )
    return pl.pallas_call(
        _attn_sample_kernel,
        grid_spec=grid_spec,
        out_shape=jax.ShapeDtypeStruct((N_SAMPLE, ATT_W), F32),
        compiler_params=_cparams(("parallel", "arbitrary"), VMEM_LIMIT),
        name="attn_sample",
    )(page_table, u, u, u, fq, fn, *([cache_kt] * g), *([cache_vt] * g), *([wt] * g))


def _rglru_kernel(xr_ref, gr_ref, cs_ref, h0_ref, cw_ref, cb_ref, wg_ref, bg_ref, lam_ref,
                  y_ref, hN_ref, cN_ref, xbuf, abuf, bbuf, hbuf, hcar, *, tc):
    c = pl.program_id(1)
    x = xr_ref[...]

    @pl.when(c == 0)
    def _():
        xbuf[0:8, :] = jnp.zeros((8, REC_W), F32)
        xbuf[8 - (CONV_W - 1):8, :] = cs_ref[0]
        hcar[...] = h0_ref[0]

    xbuf[8:8 + tc, :] = x
    xc = cb_ref[...] + xbuf[5:5 + tc, :] * cw_ref[0:1, :]
    xc = xc + xbuf[6:6 + tc, :] * cw_ref[1:2, :]
    xc = xc + xbuf[7:7 + tc, :] * cw_ref[2:3, :]
    xc = xc + x * cw_ref[3:4, :]
    xbuf[0:8, :] = xbuf[tc:tc + 8, :]

    gates = jnp.dot(xc.astype(BF16), wg_ref[...], preferred_element_type=F32) + bg_ref[...]
    rg = jax.nn.sigmoid(gates[:, :REC_W])
    ig = jax.nn.sigmoid(gates[:, REC_W:])
    nl = -lam_ref[...]
    softplus = jnp.maximum(nl, 0.0) + jnp.log1p(jnp.exp(-jnp.abs(nl)))
    log_a = (-RG_C * softplus) * rg
    a = jnp.exp(log_a)
    abuf[...] = a
    bbuf[...] = jnp.sqrt(-jnp.tanh(log_a) * (a * a + 1.0)) * (ig * xc)

    def group(gi, h):
        base = pl.multiple_of(gi * 8, 8)
        a8 = abuf[pl.ds(base, 8), :]
        b8 = bbuf[pl.ds(base, 8), :]
        rows = []
        for r in range(8):
            h = a8[r:r + 1, :] * h + b8[r:r + 1, :]
            rows.append(h)
        hbuf[pl.ds(base, 8), :] = jnp.concatenate(rows, axis=0)
        return h

    h_last = lax.fori_loop(0, tc // 8, group, hcar[...])
    hcar[...] = h_last
    y_ref[...] = hbuf[...] * jax.nn.gelu(gr_ref[...], approximate=True)

    @pl.when(c == pl.num_programs(1) - 1)
    def _():
        hN_ref[0] = h_last
        cN_ref[0] = x[tc - (CONV_W - 1):tc, :]


def _rglru(u, conv_state, h0, cw, cb, wg, bg, lam, *, nseq, slen, tc, row0):
    nchunk = slen // tc
    blk0 = row0 // tc
    xcol = (3 * ATT_W) // REC_W
    rep = lambda shape: pl.BlockSpec(shape, lambda b, c: (0,) * len(shape))
    return pl.pallas_call(
        functools.partial(_rglru_kernel, tc=tc),
        grid=(nseq, nchunk),
        in_specs=[
            pl.BlockSpec((tc, REC_W), lambda b, c: (blk0 + b * nchunk + c, xcol)),
            pl.BlockSpec((tc, REC_W), lambda b, c: (blk0 + b * nchunk + c, xcol + 1)),
            pl.BlockSpec((1, CONV_W - 1, REC_W), lambda b, c: (b, 0, 0)),
            pl.BlockSpec((1, 1, REC_W), lambda b, c: (b, 0, 0)),
            rep((CONV_W, REC_W)), rep((1, REC_W)), rep((REC_W, 2 * REC_W)),
            rep((1, 2 * REC_W)), rep((1, REC_W)),
        ],
        out_specs=[
            pl.BlockSpec((tc, REC_W), lambda b, c: (b * nchunk + c, 0)),
            pl.BlockSpec((1, 1, REC_W), lambda b, c: (b, 0, 0)),
            pl.BlockSpec((1, CONV_W - 1, REC_W), lambda b, c: (b, 0, 0)),
        ],
        out_shape=[
            jax.ShapeDtypeStruct((nseq * slen, REC_W), F32),
            jax.ShapeDtypeStruct((nseq, 1, REC_W), F32),
            jax.ShapeDtypeStruct((nseq, CONV_W - 1, REC_W), F32),
        ],
        scratch_shapes=[
            pltpu.VMEM((tc + 8, REC_W), F32),
            pltpu.VMEM((tc, REC_W), F32),
            pltpu.VMEM((tc, REC_W), F32),
            pltpu.VMEM((tc, REC_W), F32),
            pltpu.VMEM((1, REC_W), F32),
        ],
        compiler_params=_cparams(("parallel", "arbitrary"), VMEM_LIMIT),
        name="rglru",
    )(u, u, conv_state, h0, cw, cb, wg, bg, lam)


def _layer_norm(x, g, b):
    mu = jnp.mean(x, axis=-1, keepdims=True)
    xc = x - mu
    var = jnp.mean(xc * xc, axis=-1, keepdims=True)
    return xc * lax.rsqrt(var + EPS) * g + b


def _rms_norm(x, g):
    return x * lax.rsqrt(jnp.mean(x * x, axis=-1, keepdims=True) + EPS) * g


def _route(logits, rbias):
    shape = logits.shape
    ninf = -jnp.inf
    lane = lax.broadcasted_iota(jnp.int32, shape, 1)
    lane_f = lane.astype(F32)
    gsz = N_EXPERTS // N_GROUPS
    grp_f = (lane >> 3).astype(F32)
    valid = lane < N_EXPERTS
    scores = jax.nn.sigmoid(logits)
    biased = jnp.where(valid, scores + rbias, ninf)

    def first_max(vals, idx_f):
        m = jnp.max(vals, axis=1, keepdims=True)
        i = jnp.min(jnp.where(vals == m, idx_f, 1e9), axis=1, keepdims=True)
        return m, i

    gs = jnp.full(shape, ninf, F32)
    for g in range(N_GROUPS):
        in_g = (lane >= g * gsz) & (lane < (g + 1) * gsz)
        v = jnp.where(in_g, biased, ninf)
        m1, i1 = first_max(v, lane_f)
        m2 = jnp.max(jnp.where(lane_f == i1, ninf, v), axis=1, keepdims=True)
        gs = jnp.where(in_g, m1 + m2, gs)
    keep = jnp.zeros(shape, F32)
    for _ in range(TOPK_GROUPS):
        _, gi = first_max(gs, grp_f)
        pick = grp_f == gi
        keep = jnp.where(pick, 1.0, keep)
        gs = jnp.where(pick, ninf, gs)
    cand = jnp.where(keep > 0.0, biased, ninf)
    sel = jnp.zeros(shape, F32)
    for _ in range(TOP_K):
        _, ei = first_max(cand, lane_f)
        pick = lane_f == ei
        sel = jnp.where(pick, 1.0, sel)
        cand = jnp.where(pick, ninf, cand)
    picked = jnp.where(sel > 0.0, scores, 0.0)
    gate = picked / jnp.sum(picked, axis=1, keepdims=True) * ROUTED_SCALE
    return jnp.where(lane == N_EXPERTS, 1.0, gate)


def _out_kernel(att_ref, rec_ref, x_ref, ga_ref, gr_ref, wa_ref, wr_ref, g1_ref, b1_ref,
                rw_ref, rb_ref, x1_ref, gate_ref, cnt_ref):
    na = _rms_norm(att_ref[...], ga_ref[...]).astype(BF16)
    nr = _rms_norm(rec_ref[...], gr_ref[...]).astype(BF16)
    mix = (jnp.dot(na, wa_ref[...], preferred_element_type=F32)
           + jnp.dot(nr, wr_ref[...], preferred_element_type=F32))
    x1 = _layer_norm(ALPHA * x_ref[...] + mix, g1_ref[...], b1_ref[...])
    x1_ref[...] = x1
    logits = jnp.dot(x1.astype(BF16), rw_ref[...], preferred_element_type=F32)
    gate = _route(logits, rb_ref[...])
    gate_ref[...] = gate
    lane = lax.broadcasted_iota(jnp.int32, gate.shape, 1)
    routed = (gate > 0.0) & (lane < N_EXPERTS)
    cnt_ref[0] = jnp.sum(jnp.where(routed, 1.0, 0.0), axis=0, keepdims=True)


def _out_proj(att, rec, x, ga, gr, wa, wr, g1, b1, rw, rb, tm):
    n = x.shape[0]
    rep = lambda shape: pl.BlockSpec(shape, lambda i: (0,) * len(shape))
    return pl.pallas_call(
        _out_kernel,
        grid=(n // tm,),
        in_specs=[
            pl.BlockSpec((tm, ATT_W), lambda i: (i, 0)),
            pl.BlockSpec((tm, REC_W), lambda i: (i, 0)),
            pl.BlockSpec((tm, D_MODEL), lambda i: (i, 0)),
            rep((1, ATT_W)), rep((1, REC_W)),
            rep((ATT_W, D_MODEL)), rep((REC_W, D_MODEL)),
            rep((1, D_MODEL)), rep((1, D_MODEL)),
            rep((D_MODEL, LANES)), rep((1, LANES)),
        ],
        out_specs=[pl.BlockSpec((tm, D_MODEL), lambda i: (i, 0)),
                   pl.BlockSpec((tm, LANES), lambda i: (i, 0)),
                   pl.BlockSpec((1, 1, LANES), lambda i: (i, 0, 0))],
        out_shape=[jax.ShapeDtypeStruct((n, D_MODEL), F32),
                   jax.ShapeDtypeStruct((n, LANES), F32),
                   jax.ShapeDtypeStruct((n // tm, 1, LANES), F32)],
        compiler_params=_cparams(("parallel",), VMEM_LIMIT),
        name="out_proj_router",
    )(att, rec, x, ga, gr, wa, wr, g1, b1, rw, rb)


def _moe_plan(cnt):
    c = (cnt + ROW_Q - 1) // ROW_Q * ROW_Q
    lo = jnp.cumsum(c, axis=1) - c
    tot = jnp.sum(c, axis=0)
    cap = (tot + TG - 1) // TG * TG
    ends = jnp.cumsum(cap)
    base = ends - cap
    go = base[None, :] + jnp.cumsum(c, axis=0) - c
    nused = (ends[-1] // TG).astype(jnp.int32).reshape(1)
    nt = N_TILES
    nt3 = P_ROWS // TG
    te = jnp.minimum(jnp.searchsorted(ends, jnp.arange(nt3, dtype=jnp.int32) * TG, side='right'),
                     N_EXPERTS - 1).astype(jnp.int32)
    tail = (cap - tot).astype(jnp.int32)
    tdst = (base + tot).astype(jnp.int32)
    pad = lambda a: jnp.pad(a.astype(F32), ((0, 0), (0, LANES - N_EXPERTS)))
    lo_col = pad(lo).reshape(nt, LANES, 1)
    lo_row = pad(lo).reshape(nt, 1, LANES)
    c_row = pad(c).reshape(nt, 1, LANES)
    return dict(c=c.astype(jnp.int32), lo=lo.astype(jnp.int32), go=go.astype(jnp.int32), tail=tail, tdst=tdst,
                te=te, nused=nused, lo_col=lo_col, lo_row=lo_row, c_row=c_row)


def _tile_plan(t, gate, lo_col, lo_s, dest_sc, p_sc):
    gt = gate.T
    sub = lax.broadcasted_iota(jnp.int32, (LANES, TT), 0)
    sel = (gt > 0.0) & (sub < N_EXPERTS)
    r = lax.broadcasted_iota(jnp.int32, (TT, TT), 0)
    c = lax.broadcasted_iota(jnp.int32, (TT, TT), 1)
    before = jnp.where(r < c, 1.0, 0.0).astype(BF16)
    rank = jnp.dot(jnp.where(sel, 1.0, 0.0).astype(BF16), before, preferred_element_type=F32)
    dest_sc[...] = jnp.where(sel, lo_col + rank, -1.0)
    p_sc[...] = jnp.zeros(p_sc.shape, p_sc.dtype)
    rows = lax.broadcasted_iota(jnp.int32, (SEG_MAX, TT), 0).astype(F32)

    def window(e, carry):
        lo_e = pl.multiple_of(lo_s[t, e], ROW_Q)
        d = dest_sc[pl.ds(e, 1), :]
        p_sc[pl.ds(lo_e, SEG_MAX), :] = jnp.where(rows + lo_e.astype(F32) == d, 1.0, 0.0).astype(p_sc.dtype)
        return carry

    lax.fori_loop(0, N_EXPERTS, window, 0)


def _for_chunks(count, src, dst, fn):
    off = jnp.int32(0)
    for bit in CHUNK_BITS:
        hit = (count & bit) != 0

        @pl.when(hit)
        def _(off=off, bit=bit):
            fn(src + off, dst + off, bit)

        off = off + (count & bit)


def _dispatch_kernel(c_s, lo_s, go_s, tail_s, tdst_s, x_ref, gate_ref, locol_ref, xg_ref,
                     dest_sc, p_sc, xs_sc, z_sc, sem):
    t = pl.program_id(0)
    _tile_plan(t, gate_ref[...], locol_ref[0], lo_s, dest_sc, p_sc)
    xs_sc[...] = jnp.dot(p_sc[...], x_ref[...].astype(BF16), preferred_element_type=F32).astype(BF16)

    def seg_copy(src, dst, n):
        return pltpu.make_async_copy(xs_sc.at[pl.ds(pl.multiple_of(src, ROW_Q), n), :],
                                     xg_ref.at[pl.ds(pl.multiple_of(dst, ROW_Q), n), :], sem)

    def start(e, carry):
        _for_chunks(c_s[t, e], lo_s[t, e], go_s[t, e], lambda s, d, n: seg_copy(s, d, n).start())
        return carry

    def wait(e, carry):
        _for_chunks(c_s[t, e], lo_s[t, e], go_s[t, e], lambda s, d, n: seg_copy(s, d, n).wait())
        return carry

    lax.fori_loop(0, N_EXPERTS, start, 0)

    def zero_copy(src, dst, n):
        return pltpu.make_async_copy(z_sc.at[pl.ds(0, n), :],
                                     xg_ref.at[pl.ds(pl.multiple_of(dst, ROW_Q), n), :], sem)

    last = t == pl.num_programs(0) - 1

    @pl.when(last)
    def _():
        z_sc[...] = jnp.zeros(z_sc.shape, z_sc.dtype)

        def zstart(e, carry):
            _for_chunks(tail_s[e], 0, tdst_s[e], lambda s, d, n: zero_copy(s, d, n).start())
            return carry

        lax.fori_loop(0, N_EXPERTS, zstart, 0)

    lax.fori_loop(0, N_EXPERTS, wait, 0)

    @pl.when(last)
    def _():
        def zwait(e, carry):
            _for_chunks(tail_s[e], 0, tdst_s[e], lambda s, d, n: zero_copy(s, d, n).wait())
            return carry

        lax.fori_loop(0, N_EXPERTS, zwait, 0)


def _moe_dispatch(pl_, x1, gate):
    nt, pmax = N_TILES, P_ROWS
    grid_spec = pltpu.PrefetchScalarGridSpec(
        num_scalar_prefetch=5,
        grid=(nt,),
        in_specs=[
            pl.BlockSpec((TT, D_MODEL), lambda t, *_: (t, 0)),
            pl.BlockSpec((TT, LANES), lambda t, *_: (t, 0)),
            pl.BlockSpec((1, LANES, 1), lambda t, *_: (t, 0, 0)),
        ],
        out_specs=pl.BlockSpec(memory_space=pl.ANY),
        scratch_shapes=[
            pltpu.VMEM((LANES, TT), F32),
            pltpu.VMEM((LCP, TT), BF16),
            pltpu.VMEM((LCP, D_MODEL), BF16),
            pltpu.VMEM((CHUNK_BITS[0], D_MODEL), BF16),
            pltpu.SemaphoreType.DMA,
        ],
    )
    return pl.pallas_call(
        _dispatch_kernel,
        grid_spec=grid_spec,
        out_shape=jax.ShapeDtypeStruct((pmax, D_MODEL), BF16),
        compiler_params=_cparams(("arbitrary",), VMEM_LIMIT),
        name="moe_dispatch",
    )(pl_["c"], pl_["lo"], pl_["go"], pl_["tail"], pl_["tdst"], x1, gate, pl_["lo_col"])


def _grouped_kernel(te_s, nused_s, x_ref, w1_ref, w3_ref, w2_ref, y_ref):
    i = pl.program_id(0)

    @pl.when(i < nused_s[0])
    def _():
        x = x_ref[...]
        h1 = jnp.dot(x, w1_ref[0].astype(BF16), preferred_element_type=F32)
        h3 = jnp.dot(x, w3_ref[0].astype(BF16), preferred_element_type=F32)
        h = (h1 * jax.nn.sigmoid(h1)) * h3
        y_ref[...] = jnp.dot(h.astype(BF16), w2_ref[0].astype(BF16), preferred_element_type=F32)


def _moe_grouped(pl_, xg, w1, w3, w2):
    pmax = xg.shape[0]
    row = lambda i, te, nu: (jnp.minimum(i, nu[0] - 1), 0)
    wmap = lambda i, te, nu: (te[i], 0, 0)
    grid_spec = pltpu.PrefetchScalarGridSpec(
        num_scalar_prefetch=2,
        grid=(pmax // TG,),
        in_specs=[
            pl.BlockSpec((TG, D_MODEL), row),
            pl.BlockSpec((1, D_MODEL, D_EXPERT), wmap),
            pl.BlockSpec((1, D_MODEL, D_EXPERT), wmap),
            pl.BlockSpec((1, D_EXPERT, D_MODEL), wmap),
        ],
        out_specs=pl.BlockSpec((TG, D_MODEL), row),
    )
    return pl.pallas_call(
        _grouped_kernel,
        grid_spec=grid_spec,
        out_shape=jax.ShapeDtypeStruct((pmax, D_MODEL), F32),
        compiler_params=_cparams(("arbitrary",), VMEM_LIMIT),
        name="moe_grouped",
    )(pl_["te"], pl_["nused"], xg, w1, w3, w2)


def _combine_kernel(c_s, lo_s, go_s, x_ref, gate_ref, locol_ref, lorow_ref, crow_ref, yg_ref,
                    ws1_ref, ws3_ref, ws2_ref, g2_ref, b2_ref, o_ref, dest_sc, p_sc, ys_sc, sem):
    t = pl.program_id(0)

    @pl.when(t == 0)
    def _():
        ys_sc[...] = jnp.zeros(ys_sc.shape, ys_sc.dtype)

    def seg_copy(src, dst, n):
        return pltpu.make_async_copy(yg_ref.at[pl.ds(pl.multiple_of(src, ROW_Q), n), :],
                                     ys_sc.at[pl.ds(pl.multiple_of(dst, ROW_Q), n), :], sem)

    def start(e, carry):
        _for_chunks(c_s[t, e], go_s[t, e], lo_s[t, e], lambda s, d, n: seg_copy(s, d, n).start())
        return carry

    def wait(e, carry):
        _for_chunks(c_s[t, e], go_s[t, e], lo_s[t, e], lambda s, d, n: seg_copy(s, d, n).wait())
        return carry

    lax.fori_loop(0, N_EXPERTS, start, 0)

    gate = gate_ref[...]
    _tile_plan(t, gate, locol_ref[0], lo_s, dest_sc, p_sc)
    x = x_ref[...]
    xb = x.astype(BF16)
    s1 = jnp.dot(xb, ws1_ref[...], preferred_element_type=F32)
    s3 = jnp.dot(xb, ws3_ref[...], preferred_element_type=F32)
    shared = jnp.dot(((s1 * jax.nn.sigmoid(s1)) * s3).astype(BF16), ws2_ref[...], preferred_element_type=F32)
    ghi, gmid, glo = _split3(gate)

    lax.fori_loop(0, N_EXPERTS, wait, 0)

    lo_row = lorow_ref[0]
    c_row = crow_ref[0]
    routed = jnp.zeros((TT, D_MODEL), F32)
    ck = 256
    for k0 in range(0, LCP, ck):
        p = p_sc[k0:k0 + ck, :]
        pg = (jnp.dot(p, ghi, preferred_element_type=F32) + jnp.dot(p, gmid, preferred_element_type=F32)
              + jnp.dot(p, glo, preferred_element_type=F32))
        rowi = (lax.broadcasted_iota(jnp.int32, (ck, LANES), 0) + k0).astype(F32)
        mine = (rowi >= lo_row) & (rowi < lo_row + c_row)
        gs = jnp.sum(jnp.where(mine, pg, 0.0), axis=1, keepdims=True)
        inside = jnp.sum(jnp.where(mine, 1.0, 0.0), axis=1, keepdims=True) > 0.0
        y = jnp.where(inside, ys_sc[k0:k0 + ck, :] * gs, 0.0)
        yh = y.astype(BF16)
        yl = (y - yh.astype(F32)).astype(BF16)
        tn = (((0,), (0,)), ((), ()))
        routed = routed + lax.dot_general(p, yh, tn, preferred_element_type=F32)
        routed = routed + lax.dot_general(p, yl, tn, preferred_element_type=F32)
    o_ref[...] = _layer_norm(ALPHA * x + (routed + shared), g2_ref[...], b2_ref[...])


def _moe_combine(pl_, x1, gate, yg, ws1, ws3, ws2, g2, b2):
    nt = N_TILES
    rep = lambda shape: pl.BlockSpec(shape, lambda t, *_: (0,) * len(shape))
    grid_spec = pltpu.PrefetchScalarGridSpec(
        num_scalar_prefetch=3,
        grid=(nt,),
        in_specs=[
            pl.BlockSpec((TT, D_MODEL), lambda t, *_: (t, 0)),
            pl.BlockSpec((TT, LANES), lambda t, *_: (t, 0)),
            pl.BlockSpec((1, LANES, 1), lambda t, *_: (t, 0, 0)),
            pl.BlockSpec((1, 1, LANES), lambda t, *_: (t, 0, 0)),
            pl.BlockSpec((1, 1, LANES), lambda t, *_: (t, 0, 0)),
            pl.BlockSpec(memory_space=pl.ANY),
            rep((D_MODEL, D_EXPERT)), rep((D_MODEL, D_EXPERT)), rep((D_EXPERT, D_MODEL)),
            rep((1, D_MODEL)), rep((1, D_MODEL)),
        ],
        out_specs=pl.BlockSpec((TT, D_MODEL), lambda t, *_: (t, 0)),
        scratch_shapes=[
            pltpu.VMEM((LANES, TT), F32),
            pltpu.VMEM((LCP, TT), BF16),
            pltpu.VMEM((LCP, D_MODEL), F32),
            pltpu.SemaphoreType.DMA,
        ],
    )
    return pl.pallas_call(
        _combine_kernel,
        grid_spec=grid_spec,
        out_shape=jax.ShapeDtypeStruct((nt * TT, D_MODEL), F32),
        compiler_params=_cparams(("arbitrary",), VMEM_LIMIT),
        name="moe_combine",
    )(pl_["c"], pl_["lo"], pl_["go"], x1, gate, pl_["lo_col"], pl_["lo_row"], pl_["c_row"], yg,
      ws1, ws3, ws2, g2, b2)


def _block_diag(w):
    nb, k, j = w.shape
    eye = jnp.eye(nb, dtype=w.dtype)
    return jnp.einsum('nkj,nm->nkmj', w, eye).reshape(nb * k, nb * j)


def kernel(x_prompt, x_sample, cache_k, cache_v, cache_logf, state_h, state_conv, page_table, w_in, b_f, conv_w, conv_b, w_gr, b_gr, w_gi, b_gi, lam, g_att, g_rec, w_out, ln1_g, ln1_b, w_router, router_bias, w1, w3, w2, ws1, ws3, ws2, ln2_g, ln2_b):
    n_pool = cache_k.shape[1]
    x = jnp.concatenate([x_prompt.reshape(N_PROMPT, D_MODEL), x_sample.reshape(N_SAMPLE, D_MODEL)], axis=0)
    ckt = jnp.transpose(cache_k, (0, 1, 3, 4, 2)).reshape(DEPTH, n_pool, ATT_W, PAGE_SIZE)
    cvt = jnp.transpose(cache_v, (0, 1, 3, 4, 2)).reshape(DEPTH, n_pool, ATT_W, PAGE_SIZE)
    lf_t = jnp.swapaxes(cache_logf, -1, -2).reshape(DEPTH * n_pool * H_ATT, PAGE_SIZE)
    wt = _cumsum_lanes(lf_t, 4096).reshape(DEPTH, n_pool, H_ATT, PAGE_SIZE)
    zero_conv = jnp.zeros((BATCH, CONV_W - 1, REC_W), F32)
    zero_h = jnp.zeros((BATCH, 1, REC_W), F32)
    cf = 3 * ATT_W

    outs = {k: [] for k in ("kp", "vp", "fp", "hp", "cp", "ks", "vs", "fs", "hs", "cs")}
    for l in range(DEPTH):
        w_main = jnp.concatenate([w_in[l][:, :cf], w_in[l][:, cf + H_ATT:]], axis=1).astype(BF16)
        w_f = jnp.pad(w_in[l][:, cf:cf + H_ATT], ((0, 0), (0, LANES - H_ATT))).astype(BF16)
        bf_pad = jnp.pad(b_f[l], (0, LANES - H_ATT)).reshape(1, LANES)
        u = _matmul(x, w_main, TM_PROJ, TN_PROJ)
        flog = _forget_logits(x, w_f, bf_pad, TM_PROJ)[:, :H_ATT]
        logf_p = flog[:N_PROMPT].reshape(BATCH, SEQ, H_ATT)
        logf_s = flog[N_PROMPT:].reshape(DEC_BATCH, DEC_SEQ, H_ATT)

        ft = _cumsum_lanes(jnp.swapaxes(logf_p, 1, 2).reshape(BATCH * H_ATT, SEQ), BATCH * H_ATT)
        fr = ft.reshape(BATCH, H_ATT // 2, 2, SEQ)
        fc = jnp.swapaxes(fr, 2, 3)
        att_p = _attn_prompt(u, fc, fr)

        ls_t = jnp.pad(jnp.swapaxes(logf_s, 1, 2), ((0, 0), (0, 0), (0, LANES - DEC_SEQ)))
        fn = _cumsum_lanes(ls_t.reshape(DEC_BATCH * H_ATT, LANES), DEC_BATCH * H_ATT)
        fn = fn.reshape(DEC_BATCH, H_ATT, LANES)
        fq = fn[:, :, :DEC_SEQ].reshape(DEC_BATCH, H_ATT * DEC_SEQ, 1)
        att_s = _attn_sample(l, page_table, u, ckt, cvt, wt, fq, fn)

        wg = jnp.concatenate([_block_diag(w_gr[l]), _block_diag(w_gi[l])], axis=1).astype(BF16)
        bg = jnp.concatenate([b_gr[l], b_gi[l]]).reshape(1, 2 * REC_W)
        rec_args = (conv_w[l], conv_b[l].reshape(1, REC_W), wg, bg, lam[l].reshape(1, REC_W))
        rec_p, h_p, conv_p = _rglru(u, zero_conv, zero_h, *rec_args,
                                    nseq=BATCH, slen=SEQ, tc=TC_REC, row0=0)
        rec_s, h_s, conv_s = _rglru(u, state_conv[l], state_h[l].reshape(DEC_BATCH, 1, REC_W), *rec_args,
                                    nseq=DEC_BATCH, slen=DEC_SEQ, tc=DEC_SEQ, row0=N_PROMPT)

        att = jnp.concatenate([att_p, att_s], axis=0)
        rec = jnp.concatenate([rec_p, rec_s], axis=0)
        rw = jnp.pad(w_router[l], ((0, 0), (0, LANES - N_EXPERTS))).astype(BF16)
        rb = jnp.pad(router_bias[l], (0, LANES - N_EXPERTS)).reshape(1, LANES)
        x1, gate, cnt = _out_proj(att, rec, x, g_att[l].reshape(1, ATT_W), g_rec[l].reshape(1, REC_W),
                             w_out[l][:ATT_W].astype(BF16), w_out[l][ATT_W:].astype(BF16),
                             ln1_g[l].reshape(1, D_MODEL), ln1_b[l].reshape(1, D_MODEL),
                             rw, rb, TT)

        mp = _moe_plan(cnt[:, 0, :N_EXPERTS].astype(jnp.int32))
        xg = _moe_dispatch(mp, x1, gate)
        yg = _moe_grouped(mp, xg, w1[l], w3[l], w2[l])
        x = _moe_combine(mp, x1, gate, yg, ws1[l].astype(BF16), ws3[l].astype(BF16), ws2[l].astype(BF16),
                         ln2_g[l].reshape(1, D_MODEL), ln2_b[l].reshape(1, D_MODEL))

        kv = lambda rows, c0, shape: u[rows, c0:c0 + ATT_W].reshape(shape)
        p_rows, s_rows = slice(0, N_PROMPT), slice(N_PROMPT, N_TOK)
        p_shape = (BATCH, SEQ, H_ATT, HEAD_DIM)
        s_shape = (DEC_BATCH, DEC_SEQ, H_ATT, HEAD_DIM)
        outs["kp"].append(kv(p_rows, ATT_W, p_shape))
        outs["vp"].append(kv(p_rows, 2 * ATT_W, p_shape))
        outs["fp"].append(logf_p)
        outs["hp"].append(h_p.reshape(BATCH, REC_W))
        outs["cp"].append(conv_p)
        outs["ks"].append(kv(s_rows, ATT_W, s_shape))
        outs["vs"].append(kv(s_rows, 2 * ATT_W, s_shape))
        outs["fs"].append(logf_s)
        outs["hs"].append(h_s.reshape(DEC_BATCH, REC_W))
        outs["cs"].append(conv_s)

    st = lambda name: jnp.stack(outs[name])
    return (x[:N_PROMPT].reshape(BATCH, SEQ, D_MODEL), x[N_PROMPT:].reshape(DEC_BATCH, DEC_SEQ, D_MODEL),
            st("kp"), st("vp"), st("fp"), st("hp"), st("cp"),
            st("ks"), st("vs"), st("fs"), st("hs"), st("cs"))
```

```python
import functools

import jax
import jax.numpy as jnp
from jax import lax
from jax.experimental import pallas as pl
from jax.experimental.pallas import tpu as pltpu

F32 = jnp.float32
BF16 = jnp.bfloat16

D_MODEL = 1024
BATCH = 4
SEQ = 4096
DEPTH = 2
DEC_BATCH = 32
DEC_SEQ = 8
PAST_LEN = 8192
PAGE_SIZE = 128
ATT_W = 512
REC_W = 512
HEAD_DIM = 64
H_ATT = 8
N_REC_BLOCKS = 8
CONV_W = 4
RG_C = 8.0
N_EXPERTS = 64
TOP_K = 8
N_GROUPS = 8
TOPK_GROUPS = 4
D_EXPERT = 256
ROUTED_SCALE = 2.5
ALPHA = (2 * DEPTH) ** 0.25
EPS = 1e-5
ATT_SCALE = HEAD_DIM ** -0.5

N_PROMPT = BATCH * SEQ
N_SAMPLE = DEC_BATCH * DEC_SEQ
N_TOK = N_PROMPT + N_SAMPLE
N_PAGES = PAST_LEN // PAGE_SIZE
U_W = 3 * ATT_W + 2 * REC_W
LANES = 128
NEG = -1e30

TM_PROJ = 1280
TN_PROJ = 640
TT = 256
ROW_Q = 16
SEG_MAX = TT
LC = TT * TOP_K + N_EXPERTS * ROW_Q
LCP = LC + SEG_MAX
TG = 512
CHUNK_BITS = (256, 128, 64, 32, 16)
ROWS_BITS = (2048, 1024, 512) + CHUNK_BITS
N_TILES = N_TOK // TT
P_ROWS = -(-(N_TOK * TOP_K + N_TILES * N_EXPERTS * (ROW_Q - 1) + N_EXPERTS * (TG - 1)) // TG) * TG
TQ = 512
TC_REC = 512
PAGES_PER_STEP = 16
VMEM_LIMIT = 56 * 1024 * 1024


def _cparams(sem, vmem=None):
    return pltpu.CompilerParams(dimension_semantics=sem, vmem_limit_bytes=vmem)


def _mm_kernel(x_ref, w_ref, o_ref):
    o_ref[...] = jnp.dot(x_ref[...].astype(BF16), w_ref[...], preferred_element_type=F32)


def _matmul(x, w, tm, tn):
    m, k = x.shape
    n = w.shape[1]
    return pl.pallas_call(
        _mm_kernel,
        grid=(m // tm, n // tn),
        in_specs=[pl.BlockSpec((tm, k), lambda i, j: (i, 0)),
                  pl.BlockSpec((k, tn), lambda i, j: (0, j))],
        out_specs=pl.BlockSpec((tm, tn), lambda i, j: (i, j)),
        out_shape=jax.ShapeDtypeStruct((m, n), F32),
        compiler_params=_cparams(("parallel", "arbitrary"), VMEM_LIMIT),
        name="in_proj",
    )(x, w)


def _forget_kernel(x_ref, w_ref, b_ref, o_ref):
    z = jnp.dot(x_ref[...].astype(BF16), w_ref[...], preferred_element_type=F32) + b_ref[...]
    o_ref[...] = jnp.minimum(z, 0.0) - jnp.log1p(jnp.exp(-jnp.abs(z)))


def _forget_logits(x, w, b, tm):
    m, k = x.shape
    n = w.shape[1]
    return pl.pallas_call(
        _forget_kernel,
        grid=(m // tm,),
        in_specs=[pl.BlockSpec((tm, k), lambda i: (i, 0)),
                  pl.BlockSpec((k, n), lambda i: (0, 0)),
                  pl.BlockSpec((1, n), lambda i: (0, 0))],
        out_specs=pl.BlockSpec((tm, n), lambda i: (i, 0)),
        out_shape=jax.ShapeDtypeStruct((m, n), F32),
        compiler_params=_cparams(("parallel",), VMEM_LIMIT),
        name="forget_logits",
    )(x, w, b)


def _split3(x):
    hi = x.astype(BF16)
    r1 = x - hi.astype(F32)
    mid = r1.astype(BF16)
    lo = (r1 - mid.astype(F32)).astype(BF16)
    return hi, mid, lo


def _cumsum_kernel(x_ref, o_ref, *, nchunk):
    rows = x_ref.shape[0]
    r = lax.broadcasted_iota(jnp.int32, (LANES, LANES), 0)
    c = lax.broadcasted_iota(jnp.int32, (LANES, LANES), 1)
    upper = jnp.where(r <= c, 1.0, 0.0).astype(BF16)
    carry = jnp.zeros((rows, 1), F32)
    for ch in range(nchunk):
        x = x_ref[:, ch * LANES:(ch + 1) * LANES]
        hi, mid, lo = _split3(x)
        w = (jnp.dot(hi, upper, preferred_element_type=F32)
             + jnp.dot(mid, upper, preferred_element_type=F32)
             + jnp.dot(lo, upper, preferred_element_type=F32)) + carry
        o_ref[:, ch * LANES:(ch + 1) * LANES] = w
        carry = w[:, LANES - 1:LANES]


def _cumsum_lanes(x, tr):
    rows, s = x.shape
    return pl.pallas_call(
        functools.partial(_cumsum_kernel, nchunk=s // LANES),
        grid=(rows // tr,),
        in_specs=[pl.BlockSpec((tr, s), lambda i: (i, 0))],
        out_specs=pl.BlockSpec((tr, s), lambda i: (i, 0)),
        out_shape=jax.ShapeDtypeStruct((rows, s), F32),
        compiler_params=_cparams(("parallel",)),
        name="cumsum_lanes",
    )(x)


def _attn_prompt_kernel(q_ref, k_ref, v_ref, fc_ref, fr_ref, o_ref, s_sc):
    qb = pl.program_id(2)
    lane = lax.broadcasted_iota(jnp.int32, (TQ, LANES), 1)
    row = lax.broadcasted_iota(jnp.int32, (TQ, TQ), 0)
    col = lax.broadcasted_iota(jnp.int32, (TQ, TQ), 1)
    q = q_ref[...] * ATT_SCALE
    fc = fc_ref[0, 0]
    outs = []
    for h in range(2):
        in_head = (lane < HEAD_DIM) if h == 0 else (lane >= HEAD_DIM)
        qm = jnp.where(in_head, q, 0.0).astype(BF16)
        fq = fc[:, h:h + 1]

        def scores(kb, carry, masked):
            m, l = carry
            k0 = pl.multiple_of(kb * TQ, TQ)
            k = k_ref[pl.ds(k0, TQ), :].astype(BF16)
            fk = fr_ref[0, 0, h:h + 1, pl.ds(k0, TQ)]
            s = lax.dot_general(qm, k, (((1,), (1,)), ((), ())), preferred_element_type=F32)
            s = s + (fq - fk)
            if masked:
                s = jnp.where(col <= row, s, NEG)
            s_sc[kb] = s
            m_new = jnp.maximum(m, jnp.max(s, axis=1, keepdims=True))
            l = jnp.exp(m - m_new) * l + jnp.sum(jnp.exp(s - m_new), axis=1, keepdims=True)
            return m_new, l

        init = (jnp.full((TQ, 1), NEG, F32), jnp.zeros((TQ, 1), F32))
        carry = lax.fori_loop(0, qb, lambda kb, c: scores(kb, c, False), init)
        m, l = scores(qb, carry, True)

        def values(kb, acc):
            k0 = pl.multiple_of(kb * TQ, TQ)
            v = v_ref[pl.ds(k0, TQ), :].astype(BF16)
            p = jnp.exp(s_sc[kb] - m) / l
            return acc + jnp.dot(p.astype(BF16), v, preferred_element_type=F32)

        outs.append(lax.fori_loop(0, qb + 1, values, jnp.zeros((TQ, LANES), F32)))
    o_ref[...] = jnp.where(lane < HEAD_DIM, outs[0], outs[1])


def _attn_prompt(u, fc, fr):
    nq = SEQ // TQ
    kblk = ATT_W // LANES
    return pl.pallas_call(
        _attn_prompt_kernel,
        grid=(BATCH, H_ATT // 2, nq),
        in_specs=[
            pl.BlockSpec((TQ, LANES), lambda b, p, i: (b * nq + i, p)),
            pl.BlockSpec((SEQ, LANES), lambda b, p, i: (b, kblk + p)),
            pl.BlockSpec((SEQ, LANES), lambda b, p, i: (b, 2 * kblk + p)),
            pl.BlockSpec((1, 1, TQ, 2), lambda b, p, i: (b, p, i, 0)),
            pl.BlockSpec((1, 1, 2, SEQ), lambda b, p, i: (b, p, 0, 0)),
        ],
        out_specs=pl.BlockSpec((TQ, LANES), lambda b, p, i: (b * nq + i, p)),
        out_shape=jax.ShapeDtypeStruct((N_PROMPT, ATT_W), F32),
        scratch_shapes=[pltpu.VMEM((nq, TQ, TQ), F32)],
        compiler_params=_cparams(("parallel", "parallel", "arbitrary"), VMEM_LIMIT),
        name="attn_prompt",
    )(u, u, u, fc, fr)


def _expand_heads(x):
    n = x.shape[1]
    return jnp.concatenate(
        [jnp.broadcast_to(x[h:h + 1, :], (DEC_SEQ, n)) for h in range(H_ATT)], axis=0)


def _attn_sample_kernel(pt_ref, q_ref, kn_ref, vn_ref, fq_ref, fn_ref, *rest):
    del pt_ref
    g = PAGES_PER_STEP
    nstep = N_PAGES // g
    k_refs, v_refs, w_refs = rest[:g], rest[g:2 * g], rest[2 * g:3 * g]
    o_ref = rest[3 * g]
    qbd_sc, s_sc, m_sc, r_sc, acc_sc, suf_sc = rest[3 * g + 1:]
    step = pl.program_id(1)
    nrow = H_ATT * DEC_SEQ
    row = lax.broadcasted_iota(jnp.int32, (nrow, ATT_W), 0)
    colw = lax.broadcasted_iota(jnp.int32, (nrow, ATT_W), 1)
    own_head = (colw >> 6) == (row >> 3)
    fq = fq_ref[0]
    pad = jnp.zeros((PAGE_SIZE - DEC_SEQ, ATT_W), F32)
    nt = (((1,), (1,)), ((), ()))

    @pl.when(step == 0)
    def _():
        q = q_ref[...] * ATT_SCALE
        qt = jnp.concatenate([q] * H_ATT, axis=0)
        qbd_sc[...] = jnp.where(own_head, qt, 0.0).astype(BF16)
        suf_sc[...] = jnp.zeros((H_ATT, 1), F32)
        kn = jnp.concatenate([kn_ref[...], pad], axis=0).astype(BF16)
        s = lax.dot_general(qbd_sc[...], kn, nt, preferred_element_type=F32)
        s = s + (fq - _expand_heads(fn_ref[0]))
        r = lax.broadcasted_iota(jnp.int32, (nrow, PAGE_SIZE), 0)
        c = lax.broadcasted_iota(jnp.int32, (nrow, PAGE_SIZE), 1)
        s = jnp.where(c <= (r & (DEC_SEQ - 1)), s, NEG)
        s_sc[N_PAGES] = s
        m_sc[...] = jnp.max(s, axis=1, keepdims=True)

    @pl.when(step < nstep)
    def _():
        first = (nstep - 1 - step) * g
        qbd = qbd_sc[...]
        suf = suf_sc[...]
        mrun = jnp.full((nrow, PAGE_SIZE), NEG, F32)
        for i in range(g - 1, -1, -1):
            w = w_refs[i][0, 0]
            suf = suf + w[:, PAGE_SIZE - 1:PAGE_SIZE]
            kt = k_refs[i][0, 0].astype(BF16)
            s = jnp.dot(qbd, kt, preferred_element_type=F32)
            s = s + (fq + _expand_heads(suf - w))
            s_sc[first + i] = s
            mrun = jnp.maximum(mrun, s)
        suf_sc[...] = suf
        m_sc[...] = jnp.maximum(m_sc[...], jnp.max(mrun, axis=1, keepdims=True))

    @pl.when(step == nstep - 1)
    def _():
        m = m_sc[...]

        def slot(p, lacc):
            ex = jnp.exp(s_sc[p] - m)
            s_sc[p] = ex
            return lacc + ex

        lacc = lax.fori_loop(0, N_PAGES + 1, slot, jnp.zeros((nrow, PAGE_SIZE), F32))
        r_sc[...] = 1.0 / jnp.sum(lacc, axis=1, keepdims=True)

    @pl.when(step >= nstep)
    def _():
        first = (2 * nstep - 1 - step) * g
        r = r_sc[...]

        @pl.when(step == nstep)
        def _():
            vn = jnp.concatenate([vn_ref[...], pad], axis=0).astype(BF16)
            acc_sc[...] = jnp.dot((s_sc[N_PAGES] * r).astype(BF16), vn, preferred_element_type=F32)

        acc = acc_sc[...]
        for i in range(g - 1, -1, -1):
            vt = v_refs[i][0, 0].astype(BF16)
            p = (s_sc[first + i] * r).astype(BF16)
            acc = acc + lax.dot_general(p, vt, nt, preferred_element_type=F32)
        acc_sc[...] = acc

    @pl.when(step == 2 * nstep - 1)
    def _():
        o = jnp.where(own_head, acc_sc[...], 0.0)
        out = o[0:DEC_SEQ, :]
        for h in range(1, H_ATT):
            out = out + o[h * DEC_SEQ:(h + 1) * DEC_SEQ, :]
        o_ref[...] = out


def _attn_sample(layer, page_table, u, cache_kt, cache_vt, wt, fq, fn):
    g = PAGES_PER_STEP
    nstep = N_PAGES // g
    srow = N_PROMPT // DEC_SEQ
    nrow = H_ATT * DEC_SEQ

    def key_map(i):
        return lambda b, s, pt: (layer, pt[b, (nstep - 1 - jnp.minimum(s, nstep - 1)) * g + i], 0, 0)

    def value_map(i):
        return lambda b, s, pt: (layer, pt[b, (nstep - 1 - jnp.maximum(s - nstep, 0)) * g + i], 0, 0)

    in_specs = [
        pl.BlockSpec((DEC_SEQ, ATT_W), lambda b, s, pt: (srow + b, 0)),
        pl.BlockSpec((DEC_SEQ, ATT_W), lambda b, s, pt: (srow + b, 1)),
        pl.BlockSpec((DEC_SEQ, ATT_W), lambda b, s, pt: (srow + b, 2)),
        pl.BlockSpec((1, nrow, 1), lambda b, s, pt: (b, 0, 0)),
        pl.BlockSpec((1, H_ATT, LANES), lambda b, s, pt: (b, 0, 0)),
    ]
    in_specs += [pl.BlockSpec((1, 1, ATT_W, PAGE_SIZE), key_map(i)) for i in range(g)]
    in_specs += [pl.BlockSpec((1, 1, ATT_W, PAGE_SIZE), value_map(i)) for i in range(g)]
    in_specs += [pl.BlockSpec((1, 1, H_ATT, PAGE_SIZE), key_map(i)) for i in range(g)]
    grid_spec = pltpu.PrefetchScalarGridSpec(
        num_scalar_prefetch=1,
        grid=(DEC_BATCH, 2 * nstep),
        in_specs=in_specs,
        out_specs=pl.BlockSpec((DEC_SEQ, ATT_W), lambda b, s, pt: (b, 0)),
        scratch_shapes=[
            pltpu.VMEM((nrow, ATT_W), BF16),
            pltpu.VMEM((N_PAGES + 1, nrow, PAGE_SIZE), F32),
            pltpu.VMEM((nrow, 1), F32),
            pltpu.VMEM((nrow, 1), F32),
            pltpu.VMEM((nrow, ATT_W), F32),
            pltpu.VMEM((H_ATT, 1), F32),
        ],
    )
    return pl.pallas_call(
        _attn_sample_kernel,
        grid_spec=grid_spec,
        out_shape=jax.ShapeDtypeStruct((N_SAMPLE, ATT_W), F32),
        compiler_params=_cparams(("parallel", "arbitrary"), VMEM_LIMIT),
        name="attn_sample",
    )(page_table, u, u, u, fq, fn, *([cache_kt] * g), *([cache_vt] * g), *([wt] * g))


def _rglru_kernel(xr_ref, gr_ref, cs_ref, h0_ref, cw_ref, cb_ref, wg_ref, bg_ref, lam_ref,
                  y_ref, hN_ref, cN_ref, xbuf, abuf, bbuf, hbuf, hcar, *, tc):
    c = pl.program_id(1)
    x = xr_ref[...]

    @pl.when(c == 0)
    def _():
        xbuf[0:8, :] = jnp.zeros((8, REC_W), F32)
        xbuf[8 - (CONV_W - 1):8, :] = cs_ref[0]
        hcar[...] = h0_ref[0]

    xbuf[8:8 + tc, :] = x
    xc = cb_ref[...] + xbuf[5:5 + tc, :] * cw_ref[0:1, :]
    xc = xc + xbuf[6:6 + tc, :] * cw_ref[1:2, :]
    xc = xc + xbuf[7:7 + tc, :] * cw_ref[2:3, :]
    xc = xc + x * cw_ref[3:4, :]
    xbuf[0:8, :] = xbuf[tc:tc + 8, :]

    gates = jnp.dot(xc.astype(BF16), wg_ref[...], preferred_element_type=F32) + bg_ref[...]
    rg = jax.nn.sigmoid(gates[:, :REC_W])
    ig = jax.nn.sigmoid(gates[:, REC_W:])
    nl = -lam_ref[...]
    softplus = jnp.maximum(nl, 0.0) + jnp.log1p(jnp.exp(-jnp.abs(nl)))
    log_a = (-RG_C * softplus) * rg
    a = jnp.exp(log_a)
    abuf[...] = a
    bbuf[...] = jnp.sqrt(-jnp.tanh(log_a) * (a * a + 1.0)) * (ig * xc)

    def group(gi, h):
        base = pl.multiple_of(gi * 8, 8)
        a8 = abuf[pl.ds(base, 8), :]
        b8 = bbuf[pl.ds(base, 8), :]
        rows = []
        for r in range(8):
            h = a8[r:r + 1, :] * h + b8[r:r + 1, :]
            rows.append(h)
        hbuf[pl.ds(base, 8), :] = jnp.concatenate(rows, axis=0)
        return h

    h_last = lax.fori_loop(0, tc // 8, group, hcar[...])
    hcar[...] = h_last
    y_ref[...] = hbuf[...] * jax.nn.gelu(gr_ref[...], approximate=True)

    @pl.when(c == pl.num_programs(1) - 1)
    def _():
        hN_ref[0] = h_last
        cN_ref[0] = x[tc - (CONV_W - 1):tc, :]


def _rglru(u, conv_state, h0, cw, cb, wg, bg, lam, *, nseq, slen, tc, row0):
    nchunk = slen // tc
    blk0 = row0 // tc
    xcol = (3 * ATT_W) // REC_W
    rep = lambda shape: pl.BlockSpec(shape, lambda b, c: (0,) * len(shape))
    return pl.pallas_call(
        functools.partial(_rglru_kernel, tc=tc),
        grid=(nseq, nchunk),
        in_specs=[
            pl.BlockSpec((tc, REC_W), lambda b, c: (blk0 + b * nchunk + c, xcol)),
            pl.BlockSpec((tc, REC_W), lambda b, c: (blk0 + b * nchunk + c, xcol + 1)),
            pl.BlockSpec((1, CONV_W - 1, REC_W), lambda b, c: (b, 0, 0)),
            pl.BlockSpec((1, 1, REC_W), lambda b, c: (b, 0, 0)),
            rep((CONV_W, REC_W)), rep((1, REC_W)), rep((REC_W, 2 * REC_W)),
            rep((1, 2 * REC_W)), rep((1, REC_W)),
        ],
        out_specs=[
            pl.BlockSpec((tc, REC_W), lambda b, c: (b * nchunk + c, 0)),
            pl.BlockSpec((1, 1, REC_W), lambda b, c: (b, 0, 0)),
            pl.BlockSpec((1, CONV_W - 1, REC_W), lambda b, c: (b, 0, 0)),
        ],
        out_shape=[
            jax.ShapeDtypeStruct((nseq * slen, REC_W), F32),
            jax.ShapeDtypeStruct((nseq, 1, REC_W), F32),
            jax.ShapeDtypeStruct((nseq, CONV_W - 1, REC_W), F32),
        ],
        scratch_shapes=[
            pltpu.VMEM((tc + 8, REC_W), F32),
            pltpu.VMEM((tc, REC_W), F32),
            pltpu.VMEM((tc, REC_W), F32),
            pltpu.VMEM((tc, REC_W), F32),
            pltpu.VMEM((1, REC_W), F32),
        ],
        compiler_params=_cparams(("parallel", "arbitrary"), VMEM_LIMIT),
        name="rglru",
    )(u, u, conv_state, h0, cw, cb, wg, bg, lam)


def _layer_norm(x, g, b):
    mu = jnp.mean(x, axis=-1, keepdims=True)
    xc = x - mu
    var = jnp.mean(xc * xc, axis=-1, keepdims=True)
    return xc * lax.rsqrt(var + EPS) * g + b


def _rms_norm(x, g):
    return x * lax.rsqrt(jnp.mean(x * x, axis=-1, keepdims=True) + EPS) * g


def _route(logits, rbias):
    shape = logits.shape
    ninf = -jnp.inf
    lane = lax.broadcasted_iota(jnp.int32, shape, 1)
    lane_f = lane.astype(F32)
    gsz = N_EXPERTS // N_GROUPS
    grp_f = (lane >> 3).astype(F32)
    valid = lane < N_EXPERTS
    scores = jax.nn.sigmoid(logits)
    biased = jnp.where(valid, scores + rbias, ninf)

    def first_max(vals, idx_f):
        m = jnp.max(vals, axis=1, keepdims=True)
        i = jnp.min(jnp.where(vals == m, idx_f, 1e9), axis=1, keepdims=True)
        return m, i

    gs = jnp.full(shape, ninf, F32)
    for g in range(N_GROUPS):
        in_g = (lane >= g * gsz) & (lane < (g + 1) * gsz)
        v = jnp.where(in_g, biased, ninf)
        m1, i1 = first_max(v, lane_f)
        m2 = jnp.max(jnp.where(lane_f == i1, ninf, v), axis=1, keepdims=True)
        gs = jnp.where(in_g, m1 + m2, gs)
    keep = jnp.zeros(shape, F32)
    for _ in range(TOPK_GROUPS):
        _, gi = first_max(gs, grp_f)
        pick = grp_f == gi
        keep = jnp.where(pick, 1.0, keep)
        gs = jnp.where(pick, ninf, gs)
    cand = jnp.where(keep > 0.0, biased, ninf)
    sel = jnp.zeros(shape, F32)
    for _ in range(TOP_K):
        _, ei = first_max(cand, lane_f)
        pick = lane_f == ei
        sel = jnp.where(pick, 1.0, sel)
        cand = jnp.where(pick, ninf, cand)
    picked = jnp.where(sel > 0.0, scores, 0.0)
    gate = picked / jnp.sum(picked, axis=1, keepdims=True) * ROUTED_SCALE
    return jnp.where(lane == N_EXPERTS, 1.0, gate)


def _out_kernel(att_ref, rec_ref, x_ref, ga_ref, gr_ref, wa_ref, wr_ref, g1_ref, b1_ref,
                rw_ref, rb_ref, x1_ref, gate_ref, cnt_ref):
    na = _rms_norm(att_ref[...], ga_ref[...]).astype(BF16)
    nr = _rms_norm(rec_ref[...], gr_ref[...]).astype(BF16)
    mix = (jnp.dot(na, wa_ref[...], preferred_element_type=F32)
           + jnp.dot(nr, wr_ref[...], preferred_element_type=F32))
    x1 = _layer_norm(ALPHA * x_ref[...] + mix, g1_ref[...], b1_ref[...])
    x1_ref[...] = x1
    logits = jnp.dot(x1.astype(BF16), rw_ref[...], preferred_element_type=F32)
    gate = _route(logits, rb_ref[...])
    gate_ref[...] = gate
    lane = lax.broadcasted_iota(jnp.int32, gate.shape, 1)
    routed = (gate > 0.0) & (lane < N_EXPERTS)
    cnt_ref[0] = jnp.sum(jnp.where(routed, 1.0, 0.0), axis=0, keepdims=True)


def _out_proj(att, rec, x, ga, gr, wa, wr, g1, b1, rw, rb, tm):
    n = x.shape[0]
    rep = lambda shape: pl.BlockSpec(shape, lambda i: (0,) * len(shape))
    return pl.pallas_call(
        _out_kernel,
        grid=(n // tm,),
        in_specs=[
            pl.BlockSpec((tm, ATT_W), lambda i: (i, 0)),
            pl.BlockSpec((tm, REC_W), lambda i: (i, 0)),
            pl.BlockSpec((tm, D_MODEL), lambda i: (i, 0)),
            rep((1, ATT_W)), rep((1, REC_W)),
            rep((ATT_W, D_MODEL)), rep((REC_W, D_MODEL)),
            rep((1, D_MODEL)), rep((1, D_MODEL)),
            rep((D_MODEL, LANES)), rep((1, LANES)),
        ],
        out_specs=[pl.BlockSpec((tm, D_MODEL), lambda i: (i, 0)),
                   pl.BlockSpec((tm, LANES), lambda i: (i, 0)),
                   pl.BlockSpec((1, 1, LANES), lambda i: (i, 0, 0))],
        out_shape=[jax.ShapeDtypeStruct((n, D_MODEL), F32),
                   jax.ShapeDtypeStruct((n, LANES), F32),
                   jax.ShapeDtypeStruct((n // tm, 1, LANES), F32)],
        compiler_params=_cparams(("parallel",), VMEM_LIMIT),
        name="out_proj_router",
    )(att, rec, x, ga, gr, wa, wr, g1, b1, rw, rb)


def _moe_plan(cnt):
    c = (cnt + ROW_Q - 1) // ROW_Q * ROW_Q
    lo = jnp.cumsum(c, axis=1) - c
    tot = jnp.sum(c, axis=0)
    cap = (tot + TG - 1) // TG * TG
    ends = jnp.cumsum(cap)
    base = ends - cap
    go = base[None, :] + jnp.cumsum(c, axis=0) - c
    nused = (ends[-1] // TG).astype(jnp.int32).reshape(1)
    nt = N_TILES
    nt3 = P_ROWS // TG
    starts = jnp.arange(nt3, dtype=jnp.int32) * TG
    te = jnp.minimum(jnp.sum((ends[None, :] <= starts[:, None]).astype(jnp.int32), axis=1), N_EXPERTS - 1)
    rows = jnp.sum(c, axis=1).astype(jnp.int32)
    tail = (cap - tot).astype(jnp.int32)
    tdst = (base + tot).astype(jnp.int32)
    pad = lambda a: jnp.pad(a.astype(F32), ((0, 0), (0, LANES - N_EXPERTS)))
    lo_col = pad(lo).reshape(nt, LANES, 1)
    lo_row = pad(lo).reshape(nt, 1, LANES)
    c_row = pad(c).reshape(nt, 1, LANES)
    return dict(c=c.astype(jnp.int32), lo=lo.astype(jnp.int32), go=go.astype(jnp.int32), rows=rows, tail=tail,
                tdst=tdst, te=te, nused=nused, lo_col=lo_col, lo_row=lo_row, c_row=c_row)


def _tile_plan(t, gate, lo_col, lo_s, dest_sc, p_sc):
    gt = gate.T
    sub = lax.broadcasted_iota(jnp.int32, (LANES, TT), 0)
    sel = (gt > 0.0) & (sub < N_EXPERTS)
    r = lax.broadcasted_iota(jnp.int32, (TT, TT), 0)
    c = lax.broadcasted_iota(jnp.int32, (TT, TT), 1)
    before = jnp.where(r < c, 1.0, 0.0).astype(BF16)
    rank = jnp.dot(jnp.where(sel, 1.0, 0.0).astype(BF16), before, preferred_element_type=F32)
    dest_sc[...] = jnp.where(sel, lo_col + rank, -1.0)
    p_sc[...] = jnp.zeros(p_sc.shape, p_sc.dtype)
    rows = lax.broadcasted_iota(jnp.int32, (SEG_MAX, TT), 0).astype(F32)

    def window(e, carry):
        lo_e = pl.multiple_of(lo_s[t, e], ROW_Q)
        d = dest_sc[pl.ds(e, 1), :]
        p_sc[pl.ds(lo_e, SEG_MAX), :] = jnp.where(rows + lo_e.astype(F32) == d, 1.0, 0.0).astype(p_sc.dtype)
        return carry

    lax.fori_loop(0, N_EXPERTS, window, 0)


def _for_chunks(count, src, dst, fn, bits=CHUNK_BITS):
    off = jnp.int32(0)
    for bit in bits:
        hit = (count & bit) != 0

        @pl.when(hit)
        def _(off=off, bit=bit):
            fn(src + off, dst + off, bit)

        off = off + (count & bit)


def _dispatch_kernel(c_s, lo_s, go_s, rows_s, tail_s, tdst_s, x_ref, gate_ref, locol_ref, xg_ref,
                     dest_sc, p_sc, xs_sc, z_sc, sem):
    t = pl.program_id(0)
    _tile_plan(t, gate_ref[...], locol_ref[0], lo_s, dest_sc, p_sc)
    xs_sc[...] = jnp.dot(p_sc[...], x_ref[...].astype(BF16), preferred_element_type=F32).astype(BF16)

    def seg_copy(src, dst, n):
        return pltpu.make_async_copy(xs_sc.at[pl.ds(pl.multiple_of(src, ROW_Q), n), :],
                                     xg_ref.at[pl.ds(pl.multiple_of(dst, ROW_Q), n), :], sem)

    def start(e, carry):
        _for_chunks(c_s[t, e], lo_s[t, e], go_s[t, e], lambda s, d, n: seg_copy(s, d, n).start())
        return carry

    lax.fori_loop(0, N_EXPERTS, start, 0)

    def zero_copy(src, dst, n):
        return pltpu.make_async_copy(z_sc.at[pl.ds(0, n), :],
                                     xg_ref.at[pl.ds(pl.multiple_of(dst, ROW_Q), n), :], sem)

    last = t == pl.num_programs(0) - 1

    @pl.when(last)
    def _():
        z_sc[...] = jnp.zeros(z_sc.shape, z_sc.dtype)

        def zstart(e, carry):
            _for_chunks(tail_s[e], 0, tdst_s[e], lambda s, d, n: zero_copy(s, d, n).start())
            return carry

        lax.fori_loop(0, N_EXPERTS, zstart, 0)

    _for_chunks(rows_s[t], 0, 0, lambda s, d, n: seg_copy(s, d, n).wait(), ROWS_BITS)

    @pl.when(last)
    def _():
        def zwait(e, carry):
            _for_chunks(tail_s[e], 0, tdst_s[e], lambda s, d, n: zero_copy(s, d, n).wait())
            return carry

        lax.fori_loop(0, N_EXPERTS, zwait, 0)


def _moe_dispatch(pl_, x1, gate):
    nt, pmax = N_TILES, P_ROWS
    grid_spec = pltpu.PrefetchScalarGridSpec(
        num_scalar_prefetch=6,
        grid=(nt,),
        in_specs=[
            pl.BlockSpec((TT, D_MODEL), lambda t, *_: (t, 0)),
            pl.BlockSpec((TT, LANES), lambda t, *_: (t, 0)),
            pl.BlockSpec((1, LANES, 1), lambda t, *_: (t, 0, 0)),
        ],
        out_specs=pl.BlockSpec(memory_space=pl.ANY),
        scratch_shapes=[
            pltpu.VMEM((LANES, TT), F32),
            pltpu.VMEM((LCP, TT), BF16),
            pltpu.VMEM((LCP, D_MODEL), BF16),
            pltpu.VMEM((CHUNK_BITS[0], D_MODEL), BF16),
            pltpu.SemaphoreType.DMA,
        ],
    )
    return pl.pallas_call(
        _dispatch_kernel,
        grid_spec=grid_spec,
        out_shape=jax.ShapeDtypeStruct((pmax, D_MODEL), BF16),
        compiler_params=_cparams(("arbitrary",), VMEM_LIMIT),
        name="moe_dispatch",
    )(pl_["c"], pl_["lo"], pl_["go"], pl_["rows"], pl_["tail"], pl_["tdst"], x1, gate, pl_["lo_col"])


def _grouped_kernel(te_s, nused_s, x_ref, w1_ref, w3_ref, w2_ref, y_ref):
    i = pl.program_id(0)

    @pl.when(i < nused_s[0])
    def _():
        x = x_ref[...]
        h1 = jnp.dot(x, w1_ref[0].astype(BF16), preferred_element_type=F32)
        h3 = jnp.dot(x, w3_ref[0].astype(BF16), preferred_element_type=F32)
        h = (h1 * jax.nn.sigmoid(h1)) * h3
        y_ref[...] = jnp.dot(h.astype(BF16), w2_ref[0].astype(BF16), preferred_element_type=F32)


def _moe_grouped(pl_, xg, w1, w3, w2):
    pmax = xg.shape[0]
    row = lambda i, te, nu: (jnp.minimum(i, nu[0] - 1), 0)
    wmap = lambda i, te, nu: (te[i], 0, 0)
    grid_spec = pltpu.PrefetchScalarGridSpec(
        num_scalar_prefetch=2,
        grid=(pmax // TG,),
        in_specs=[
            pl.BlockSpec((TG, D_MODEL), row),
            pl.BlockSpec((1, D_MODEL, D_EXPERT), wmap),
            pl.BlockSpec((1, D_MODEL, D_EXPERT), wmap),
            pl.BlockSpec((1, D_EXPERT, D_MODEL), wmap),
        ],
        out_specs=pl.BlockSpec((TG, D_MODEL), row),
    )
    return pl.pallas_call(
        _grouped_kernel,
        grid_spec=grid_spec,
        out_shape=jax.ShapeDtypeStruct((pmax, D_MODEL), F32),
        compiler_params=_cparams(("arbitrary",), VMEM_LIMIT),
        name="moe_grouped",
    )(pl_["te"], pl_["nused"], xg, w1, w3, w2)


def _combine_kernel(c_s, lo_s, go_s, rows_s, x_ref, gate_ref, locol_ref, lorow_ref, crow_ref, yg_ref,
                    ws1_ref, ws3_ref, ws2_ref, g2_ref, b2_ref, o_ref, dest_sc, p_sc, ys_sc, acc_sc, sem):
    t = pl.program_id(0)

    @pl.when(t == 0)
    def _():
        ys_sc[...] = jnp.zeros(ys_sc.shape, ys_sc.dtype)

    def seg_copy(src, dst, n):
        return pltpu.make_async_copy(yg_ref.at[pl.ds(pl.multiple_of(src, ROW_Q), n), :],
                                     ys_sc.at[pl.ds(pl.multiple_of(dst, ROW_Q), n), :], sem)

    def start(e, carry):
        _for_chunks(c_s[t, e], go_s[t, e], lo_s[t, e], lambda s, d, n: seg_copy(s, d, n).start())
        return carry

    lax.fori_loop(0, N_EXPERTS, start, 0)

    gate = gate_ref[...]
    _tile_plan(t, gate, locol_ref[0], lo_s, dest_sc, p_sc)
    x = x_ref[...]
    xb = x.astype(BF16)
    s1 = jnp.dot(xb, ws1_ref[...], preferred_element_type=F32)
    s3 = jnp.dot(xb, ws3_ref[...], preferred_element_type=F32)
    shared = jnp.dot(((s1 * jax.nn.sigmoid(s1)) * s3).astype(BF16), ws2_ref[...], preferred_element_type=F32)
    ghi, gmid, glo = _split3(gate)

    _for_chunks(rows_s[t], 0, 0, lambda s, d, n: seg_copy(s, d, n).wait(), ROWS_BITS)

    lo_row = lorow_ref[0]
    c_row = crow_ref[0]
    acc_sc[...] = jnp.zeros(acc_sc.shape, acc_sc.dtype)
    ck = 256
    for k0 in range(0, LCP, ck):
        @pl.when(k0 < rows_s[t])
        def _(k0=k0):
            p = p_sc[k0:k0 + ck, :]
            pg = (jnp.dot(p, ghi, preferred_element_type=F32) + jnp.dot(p, gmid, preferred_element_type=F32)
                  + jnp.dot(p, glo, preferred_element_type=F32))
            rowi = (lax.broadcasted_iota(jnp.int32, (ck, LANES), 0) + k0).astype(F32)
            mine = (rowi >= lo_row) & (rowi < lo_row + c_row)
            gs = jnp.sum(jnp.where(mine, pg, 0.0), axis=1, keepdims=True)
            inside = jnp.sum(jnp.where(mine, 1.0, 0.0), axis=1, keepdims=True) > 0.0
            y = jnp.where(inside, ys_sc[k0:k0 + ck, :] * gs, 0.0)
            yh = y.astype(BF16)
            yl = (y - yh.astype(F32)).astype(BF16)
            tn = (((0,), (0,)), ((), ()))
            acc_sc[...] += (lax.dot_general(p, yh, tn, preferred_element_type=F32)
                            + lax.dot_general(p, yl, tn, preferred_element_type=F32))
    o_ref[...] = _layer_norm(ALPHA * x + (acc_sc[...] + shared), g2_ref[...], b2_ref[...])


def _moe_combine(pl_, x1, gate, yg, ws1, ws3, ws2, g2, b2):
    nt = N_TILES
    rep = lambda shape: pl.BlockSpec(shape, lambda t, *_: (0,) * len(shape))
    grid_spec = pltpu.PrefetchScalarGridSpec(
        num_scalar_prefetch=4,
        grid=(nt,),
        in_specs=[
            pl.BlockSpec((TT, D_MODEL), lambda t, *_: (t, 0)),
            pl.BlockSpec((TT, LANES), lambda t, *_: (t, 0)),
            pl.BlockSpec((1, LANES, 1), lambda t, *_: (t, 0, 0)),
            pl.BlockSpec((1, 1, LANES), lambda t, *_: (t, 0, 0)),
            pl.BlockSpec((1, 1, LANES), lambda t, *_: (t, 0, 0)),
            pl.BlockSpec(memory_space=pl.ANY),
            rep((D_MODEL, D_EXPERT)), rep((D_MODEL, D_EXPERT)), rep((D_EXPERT, D_MODEL)),
            rep((1, D_MODEL)), rep((1, D_MODEL)),
        ],
        out_specs=pl.BlockSpec((TT, D_MODEL), lambda t, *_: (t, 0)),
        scratch_shapes=[
            pltpu.VMEM((LANES, TT), F32),
            pltpu.VMEM((LCP, TT), BF16),
            pltpu.VMEM((LCP, D_MODEL), F32),
            pltpu.VMEM((TT, D_MODEL), F32),
            pltpu.SemaphoreType.DMA,
        ],
    )
    return pl.pallas_call(
        _combine_kernel,
        grid_spec=grid_spec,
        out_shape=jax.ShapeDtypeStruct((nt * TT, D_MODEL), F32),
        compiler_params=_cparams(("arbitrary",), VMEM_LIMIT),
        name="moe_combine",
    )(pl_["c"], pl_["lo"], pl_["go"], pl_["rows"], x1, gate, pl_["lo_col"], pl_["lo_row"], pl_["c_row"], yg,
      ws1, ws3, ws2, g2, b2)


def _block_diag(w):
    nb, k, j = w.shape
    eye = jnp.eye(nb, dtype=w.dtype)
    return jnp.einsum('nkj,nm->nkmj', w, eye).reshape(nb * k, nb * j)


def kernel(x_prompt, x_sample, cache_k, cache_v, cache_logf, state_h, state_conv, page_table, w_in, b_f, conv_w, conv_b, w_gr, b_gr, w_gi, b_gi, lam, g_att, g_rec, w_out, ln1_g, ln1_b, w_router, router_bias, w1, w3, w2, ws1, ws3, ws2, ln2_g, ln2_b):
    n_pool = cache_k.shape[1]
    x = jnp.concatenate([x_prompt.reshape(N_PROMPT, D_MODEL), x_sample.reshape(N_SAMPLE, D_MODEL)], axis=0)
    ckt = jnp.transpose(cache_k, (0, 1, 3, 4, 2)).reshape(DEPTH, n_pool, ATT_W, PAGE_SIZE)
    cvt = jnp.transpose(cache_v, (0, 1, 3, 4, 2)).reshape(DEPTH, n_pool, ATT_W, PAGE_SIZE)
    lf_t = jnp.swapaxes(cache_logf, -1, -2).reshape(DEPTH * n_pool * H_ATT, PAGE_SIZE)
    wt = _cumsum_lanes(lf_t, 4096).reshape(DEPTH, n_pool, H_ATT, PAGE_SIZE)
    zero_conv = jnp.zeros((BATCH, CONV_W - 1, REC_W), F32)
    zero_h = jnp.zeros((BATCH, 1, REC_W), F32)
    cf = 3 * ATT_W

    outs = {k: [] for k in ("kp", "vp", "fp", "hp", "cp", "ks", "vs", "fs", "hs", "cs")}
    for l in range(DEPTH):
        w_main = jnp.concatenate([w_in[l][:, :cf], w_in[l][:, cf + H_ATT:]], axis=1).astype(BF16)
        w_f = jnp.pad(w_in[l][:, cf:cf + H_ATT], ((0, 0), (0, LANES - H_ATT))).astype(BF16)
        bf_pad = jnp.pad(b_f[l], (0, LANES - H_ATT)).reshape(1, LANES)
        u = _matmul(x, w_main, TM_PROJ, TN_PROJ)
        flog = _forget_logits(x, w_f, bf_pad, TM_PROJ)[:, :H_ATT]
        logf_p = flog[:N_PROMPT].reshape(BATCH, SEQ, H_ATT)
        logf_s = flog[N_PROMPT:].reshape(DEC_BATCH, DEC_SEQ, H_ATT)

        ft = _cumsum_lanes(jnp.swapaxes(logf_p, 1, 2).reshape(BATCH * H_ATT, SEQ), BATCH * H_ATT)
        fr = ft.reshape(BATCH, H_ATT // 2, 2, SEQ)
        fc = jnp.swapaxes(fr, 2, 3)
        att_p = _attn_prompt(u, fc, fr)

        ls_t = jnp.pad(jnp.swapaxes(logf_s, 1, 2), ((0, 0), (0, 0), (0, LANES - DEC_SEQ)))
        fn = _cumsum_lanes(ls_t.reshape(DEC_BATCH * H_ATT, LANES), DEC_BATCH * H_ATT)
        fn = fn.reshape(DEC_BATCH, H_ATT, LANES)
        fq = fn[:, :, :DEC_SEQ].reshape(DEC_BATCH, H_ATT * DEC_SEQ, 1)
        att_s = _attn_sample(l, page_table, u, ckt, cvt, wt, fq, fn)

        wg = jnp.concatenate([_block_diag(w_gr[l]), _block_diag(w_gi[l])], axis=1).astype(BF16)
        bg = jnp.concatenate([b_gr[l], b_gi[l]]).reshape(1, 2 * REC_W)
        rec_args = (conv_w[l], conv_b[l].reshape(1, REC_W), wg, bg, lam[l].reshape(1, REC_W))
        rec_p, h_p, conv_p = _rglru(u, zero_conv, zero_h, *rec_args,
                                    nseq=BATCH, slen=SEQ, tc=TC_REC, row0=0)
        rec_s, h_s, conv_s = _rglru(u, state_conv[l], state_h[l].reshape(DEC_BATCH, 1, REC_W), *rec_args,
                                    nseq=DEC_BATCH, slen=DEC_SEQ, tc=DEC_SEQ, row0=N_PROMPT)

        att = jnp.concatenate([att_p, att_s], axis=0)
        rec = jnp.concatenate([rec_p, rec_s], axis=0)
        rw = jnp.pad(w_router[l], ((0, 0), (0, LANES - N_EXPERTS))).astype(BF16)
        rb = jnp.pad(router_bias[l], (0, LANES - N_EXPERTS)).reshape(1, LANES)
        x1, gate, cnt = _out_proj(att, rec, x, g_att[l].reshape(1, ATT_W), g_rec[l].reshape(1, REC_W),
                             w_out[l][:ATT_W].astype(BF16), w_out[l][ATT_W:].astype(BF16),
                             ln1_g[l].reshape(1, D_MODEL), ln1_b[l].reshape(1, D_MODEL),
                             rw, rb, TT)

        mp = _moe_plan(cnt[:, 0, :N_EXPERTS].astype(jnp.int32))
        xg = _moe_dispatch(mp, x1, gate)
        yg = _moe_grouped(mp, xg, w1[l], w3[l], w2[l])
        x = _moe_combine(mp, x1, gate, yg, ws1[l].astype(BF16), ws3[l].astype(BF16), ws2[l].astype(BF16),
                         ln2_g[l].reshape(1, D_MODEL), ln2_b[l].reshape(1, D_MODEL))

        kv = lambda rows, c0, shape: u[rows, c0:c0 + ATT_W].reshape(shape)
        p_rows, s_rows = slice(0, N_PROMPT), slice(N_PROMPT, N_TOK)
        p_shape = (BATCH, SEQ, H_ATT, HEAD_DIM)
        s_shape = (DEC_BATCH, DEC_SEQ, H_ATT, HEAD_DIM)
        outs["kp"].append(kv(p_rows, ATT_W, p_shape))
        outs["vp"].append(kv(p_rows, 2 * ATT_W, p_shape))
        outs["fp"].append(logf_p)
        outs["hp"].append(h_p.reshape(BATCH, REC_W))
        outs["cp"].append(conv_p)
        outs["ks"].append(kv(s_rows, ATT_W, s_shape))
        outs["vs"].append(kv(s_rows, 2 * ATT_W, s_shape))
        outs["fs"].append(logf_s)
        outs["hs"].append(h_s.reshape(DEC_BATCH, REC_W))
        outs["cs"].append(conv_s)

    st = lambda name: jnp.stack(outs[name])
    return (x[:N_PROMPT].reshape(BATCH, SEQ, D_MODEL), x[N_PROMPT:].reshape(DEC_BATCH, DEC_SEQ, D_MODEL),
            st("kp"), st("vp"), st("fp"), st("hp"), st("cp"),
            st("ks"), st("vs"), st("fs"), st("hs"), st("cs"))
```

```python
import functools

import jax
import jax.numpy as jnp
from jax import lax
from jax.experimental import pallas as pl
from jax.experimental.pallas import tpu as pltpu

F32 = jnp.float32
BF16 = jnp.bfloat16

D_MODEL = 1024
BATCH = 4
SEQ = 4096
DEPTH = 2
DEC_BATCH = 32
DEC_SEQ = 8
PAST_LEN = 8192
PAGE_SIZE = 128
ATT_W = 512
REC_W = 512
HEAD_DIM = 64
H_ATT = 8
N_REC_BLOCKS = 8
CONV_W = 4
RG_C = 8.0
N_EXPERTS = 64
TOP_K = 8
N_GROUPS = 8
TOPK_GROUPS = 4
D_EXPERT = 256
ROUTED_SCALE = 2.5
ALPHA = (2 * DEPTH) ** 0.25
EPS = 1e-5
ATT_SCALE = HEAD_DIM ** -0.5

N_PROMPT = BATCH * SEQ
N_SAMPLE = DEC_BATCH * DEC_SEQ
N_TOK = N_PROMPT + N_SAMPLE
N_PAGES = PAST_LEN // PAGE_SIZE
U_W = 3 * ATT_W + 2 * REC_W
LANES = 128
NEG = -1e30

TM_PROJ = 1280
TN_PROJ = 640
TM_OUT = 1280
TT = 256
ROW_Q = 16
SEG_MAX = TT
LC = TT * TOP_K + N_EXPERTS * ROW_Q
LCP = LC + SEG_MAX
TG = 512
CHUNK_BITS = (256, 128, 64, 32, 16)
ROWS_BITS = (2048, 1024, 512) + CHUNK_BITS
N_TILES = N_TOK // TT
P_ROWS = -(-(N_TOK * TOP_K + N_TILES * N_EXPERTS * (ROW_Q - 1) + N_EXPERTS * (TG - 1)) // TG) * TG
TQ = 512
TC_REC = 512
PAGES_PER_STEP = 16
VMEM_LIMIT = 56 * 1024 * 1024


def _cparams(sem, vmem=None):
    return pltpu.CompilerParams(dimension_semantics=sem, vmem_limit_bytes=vmem)


def _mm_kernel(x_ref, w_ref, o_ref):
    o_ref[...] = jnp.dot(x_ref[...].astype(BF16), w_ref[...], preferred_element_type=F32)


def _matmul(x, w, tm, tn):
    m, k = x.shape
    n = w.shape[1]
    return pl.pallas_call(
        _mm_kernel,
        grid=(m // tm, n // tn),
        in_specs=[pl.BlockSpec((tm, k), lambda i, j: (i, 0)),
                  pl.BlockSpec((k, tn), lambda i, j: (0, j))],
        out_specs=pl.BlockSpec((tm, tn), lambda i, j: (i, j)),
        out_shape=jax.ShapeDtypeStruct((m, n), F32),
        compiler_params=_cparams(("parallel", "arbitrary"), VMEM_LIMIT),
        name="in_proj",
    )(x, w)


def _forget_kernel(x_ref, w_ref, b_ref, o_ref):
    z = jnp.dot(x_ref[...].astype(BF16), w_ref[...], preferred_element_type=F32) + b_ref[...]
    o_ref[...] = jnp.minimum(z, 0.0) - jnp.log1p(jnp.exp(-jnp.abs(z)))


def _forget_logits(x, w, b, tm):
    m, k = x.shape
    n = w.shape[1]
    return pl.pallas_call(
        _forget_kernel,
        grid=(m // tm,),
        in_specs=[pl.BlockSpec((tm, k), lambda i: (i, 0)),
                  pl.BlockSpec((k, n), lambda i: (0, 0)),
                  pl.BlockSpec((1, n), lambda i: (0, 0))],
        out_specs=pl.BlockSpec((tm, n), lambda i: (i, 0)),
        out_shape=jax.ShapeDtypeStruct((m, n), F32),
        compiler_params=_cparams(("parallel",), VMEM_LIMIT),
        name="forget_logits",
    )(x, w, b)


def _split3(x):
    hi = x.astype(BF16)
    r1 = x - hi.astype(F32)
    mid = r1.astype(BF16)
    lo = (r1 - mid.astype(F32)).astype(BF16)
    return hi, mid, lo


def _cumsum_kernel(x_ref, o_ref, *, nchunk):
    rows = x_ref.shape[0]
    r = lax.broadcasted_iota(jnp.int32, (LANES, LANES), 0)
    c = lax.broadcasted_iota(jnp.int32, (LANES, LANES), 1)
    upper = jnp.where(r <= c, 1.0, 0.0).astype(BF16)
    carry = jnp.zeros((rows, 1), F32)
    for ch in range(nchunk):
        x = x_ref[:, ch * LANES:(ch + 1) * LANES]
        hi, mid, lo = _split3(x)
        w = (jnp.dot(hi, upper, preferred_element_type=F32)
             + jnp.dot(mid, upper, preferred_element_type=F32)
             + jnp.dot(lo, upper, preferred_element_type=F32)) + carry
        o_ref[:, ch * LANES:(ch + 1) * LANES] = w
        carry = w[:, LANES - 1:LANES]


def _cumsum_lanes(x, tr):
    rows, s = x.shape
    return pl.pallas_call(
        functools.partial(_cumsum_kernel, nchunk=s // LANES),
        grid=(rows // tr,),
        in_specs=[pl.BlockSpec((tr, s), lambda i: (i, 0))],
        out_specs=pl.BlockSpec((tr, s), lambda i: (i, 0)),
        out_shape=jax.ShapeDtypeStruct((rows, s), F32),
        compiler_params=_cparams(("parallel",)),
        name="cumsum_lanes",
    )(x)


def _attn_prompt_kernel(q_ref, k_ref, v_ref, fc_ref, fr_ref, o_ref, s_sc):
    qb = pl.program_id(2)
    lane = lax.broadcasted_iota(jnp.int32, (TQ, LANES), 1)
    row = lax.broadcasted_iota(jnp.int32, (TQ, TQ), 0)
    col = lax.broadcasted_iota(jnp.int32, (TQ, TQ), 1)
    q = q_ref[...] * ATT_SCALE
    fc = fc_ref[0, 0]
    outs = []
    for h in range(2):
        in_head = (lane < HEAD_DIM) if h == 0 else (lane >= HEAD_DIM)
        qm = jnp.where(in_head, q, 0.0).astype(BF16)
        fq = fc[:, h:h + 1]

        def scores(kb, m, masked):
            k0 = pl.multiple_of(kb * TQ, TQ)
            k = k_ref[pl.ds(k0, TQ), :].astype(BF16)
            fk = fr_ref[0, 0, h:h + 1, pl.ds(k0, TQ)]
            s = lax.dot_general(qm, k, (((1,), (1,)), ((), ())), preferred_element_type=F32)
            s = s + (fq - fk)
            if masked:
                s = jnp.where(col <= row, s, NEG)
            s_sc[kb] = s
            return jnp.maximum(m, jnp.max(s, axis=1, keepdims=True))

        m = lax.fori_loop(0, qb, lambda kb, c: scores(kb, c, False), jnp.full((TQ, 1), NEG, F32))
        m = scores(qb, m, True)

        def exps(kb, l):
            e = jnp.exp(s_sc[kb] - m)
            s_sc[kb] = e
            return l + jnp.sum(e, axis=1, keepdims=True)

        r = 1.0 / lax.fori_loop(0, qb + 1, exps, jnp.zeros((TQ, 1), F32))

        def values(kb, acc):
            k0 = pl.multiple_of(kb * TQ, TQ)
            v = v_ref[pl.ds(k0, TQ), :].astype(BF16)
            return acc + jnp.dot((s_sc[kb] * r).astype(BF16), v, preferred_element_type=F32)

        outs.append(lax.fori_loop(0, qb + 1, values, jnp.zeros((TQ, LANES), F32)))
    o_ref[...] = jnp.where(lane < HEAD_DIM, outs[0], outs[1])


def _attn_prompt(u, fc, fr):
    nq = SEQ // TQ
    kblk = ATT_W // LANES
    return pl.pallas_call(
        _attn_prompt_kernel,
        grid=(BATCH, H_ATT // 2, nq),
        in_specs=[
            pl.BlockSpec((TQ, LANES), lambda b, p, i: (b * nq + i, p)),
            pl.BlockSpec((SEQ, LANES), lambda b, p, i: (b, kblk + p)),
            pl.BlockSpec((SEQ, LANES), lambda b, p, i: (b, 2 * kblk + p)),
            pl.BlockSpec((1, 1, TQ, 2), lambda b, p, i: (b, p, i, 0)),
            pl.BlockSpec((1, 1, 2, SEQ), lambda b, p, i: (b, p, 0, 0)),
        ],
        out_specs=pl.BlockSpec((TQ, LANES), lambda b, p, i: (b * nq + i, p)),
        out_shape=jax.ShapeDtypeStruct((N_PROMPT, ATT_W), F32),
        scratch_shapes=[pltpu.VMEM((nq, TQ, TQ), F32)],
        compiler_params=_cparams(("parallel", "parallel", "arbitrary"), VMEM_LIMIT),
        name="attn_prompt",
    )(u, u, u, fc, fr)


def _expand_heads(x):
    n = x.shape[1]
    return jnp.concatenate(
        [jnp.broadcast_to(x[h:h + 1, :], (DEC_SEQ, n)) for h in range(H_ATT)], axis=0)


def _attn_sample_kernel(pt_ref, q_ref, kn_ref, vn_ref, fq_ref, fn_ref, *rest):
    del pt_ref
    g = PAGES_PER_STEP
    nstep = N_PAGES // g
    k_refs, v_refs, w_refs = rest[:g], rest[g:2 * g], rest[2 * g:3 * g]
    o_ref = rest[3 * g]
    qbd_sc, s_sc, m_sc, r_sc, acc_sc, suf_sc = rest[3 * g + 1:]
    step = pl.program_id(1)
    nrow = H_ATT * DEC_SEQ
    row = lax.broadcasted_iota(jnp.int32, (nrow, ATT_W), 0)
    colw = lax.broadcasted_iota(jnp.int32, (nrow, ATT_W), 1)
    own_head = (colw >> 6) == (row >> 3)
    fq = fq_ref[0]
    pad = jnp.zeros((PAGE_SIZE - DEC_SEQ, ATT_W), F32)
    nt = (((1,), (1,)), ((), ()))

    @pl.when(step == 0)
    def _():
        q = q_ref[...] * ATT_SCALE
        qt = jnp.concatenate([q] * H_ATT, axis=0)
        qbd_sc[...] = jnp.where(own_head, qt, 0.0).astype(BF16)
        suf_sc[...] = jnp.zeros((H_ATT, 1), F32)
        kn = jnp.concatenate([kn_ref[...], pad], axis=0).astype(BF16)
        s = lax.dot_general(qbd_sc[...], kn, nt, preferred_element_type=F32)
        s = s + (fq - _expand_heads(fn_ref[0]))
        r = lax.broadcasted_iota(jnp.int32, (nrow, PAGE_SIZE), 0)
        c = lax.broadcasted_iota(jnp.int32, (nrow, PAGE_SIZE), 1)
        s = jnp.where(c <= (r & (DEC_SEQ - 1)), s, NEG)
        s_sc[N_PAGES] = s
        m_sc[...] = jnp.max(s, axis=1, keepdims=True)

    @pl.when(step < nstep)
    def _():
        first = (nstep - 1 - step) * g
        qbd = qbd_sc[...]
        suf = suf_sc[...]
        mrun = jnp.full((nrow, PAGE_SIZE), NEG, F32)
        for i in range(g - 1, -1, -1):
            w = w_refs[i][0, 0]
            suf = suf + w[:, PAGE_SIZE - 1:PAGE_SIZE]
            kt = k_refs[i][0, 0].astype(BF16)
            s = jnp.dot(qbd, kt, preferred_element_type=F32)
            s = s + (fq + _expand_heads(suf - w))
            s_sc[first + i] = s
            mrun = jnp.maximum(mrun, s)
        suf_sc[...] = suf
        m_sc[...] = jnp.maximum(m_sc[...], jnp.max(mrun, axis=1, keepdims=True))

    @pl.when(step == nstep - 1)
    def _():
        m = m_sc[...]

        def slot(p, lacc):
            ex = jnp.exp(s_sc[p] - m)
            s_sc[p] = ex
            return lacc + ex

        lacc = lax.fori_loop(0, N_PAGES + 1, slot, jnp.zeros((nrow, PAGE_SIZE), F32))
        r_sc[...] = 1.0 / jnp.sum(lacc, axis=1, keepdims=True)

    @pl.when(step >= nstep)
    def _():
        first = (2 * nstep - 1 - step) * g
        r = r_sc[...]

        @pl.when(step == nstep)
        def _():
            vn = jnp.concatenate([vn_ref[...], pad], axis=0).astype(BF16)
            acc_sc[...] = jnp.dot((s_sc[N_PAGES] * r).astype(BF16), vn, preferred_element_type=F32)

        acc = acc_sc[...]
        for i in range(g - 1, -1, -1):
            vt = v_refs[i][0, 0].astype(BF16)
            p = (s_sc[first + i] * r).astype(BF16)
            acc = acc + lax.dot_general(p, vt, nt, preferred_element_type=F32)
        acc_sc[...] = acc

    @pl.when(step == 2 * nstep - 1)
    def _():
        o = jnp.where(own_head, acc_sc[...], 0.0)
        out = o[0:DEC_SEQ, :]
        for h in range(1, H_ATT):
            out = out + o[h * DEC_SEQ:(h + 1) * DEC_SEQ, :]
        o_ref[...] = out


def _attn_sample(layer, page_table, u, cache_kt, cache_vt, wt, fq, fn):
    g = PAGES_PER_STEP
    nstep = N_PAGES // g
    srow = N_PROMPT // DEC_SEQ
    nrow = H_ATT * DEC_SEQ

    def key_map(i):
        return lambda b, s, pt: (layer, pt[b, (nstep - 1 - jnp.minimum(s, nstep - 1)) * g + i], 0, 0)

    def value_map(i):
        return lambda b, s, pt: (layer, pt[b, (nstep - 1 - jnp.maximum(s - nstep, 0)) * g + i], 0, 0)

    in_specs = [
        pl.BlockSpec((DEC_SEQ, ATT_W), lambda b, s, pt: (srow + b, 0)),
        pl.BlockSpec((DEC_SEQ, ATT_W), lambda b, s, pt: (srow + b, 1)),
        pl.BlockSpec((DEC_SEQ, ATT_W), lambda b, s, pt: (srow + b, 2)),
        pl.BlockSpec((1, nrow, 1), lambda b, s, pt: (b, 0, 0)),
        pl.BlockSpec((1, H_ATT, LANES), lambda b, s, pt: (b, 0, 0)),
    ]
    in_specs += [pl.BlockSpec((1, 1, ATT_W, PAGE_SIZE), key_map(i)) for i in range(g)]
    in_specs += [pl.BlockSpec((1, 1, ATT_W, PAGE_SIZE), value_map(i)) for i in range(g)]
    in_specs += [pl.BlockSpec((1, 1, H_ATT, PAGE_SIZE), key_map(i)) for i in range(g)]
    grid_spec = pltpu.PrefetchScalarGridSpec(
        num_scalar_prefetch=1,
        grid=(DEC_BATCH, 2 * nstep),
        in_specs=in_specs,
        out_specs=pl.BlockSpec((DEC_SEQ, ATT_W), lambda b, s, pt: (b, 0)),
        scratch_shapes=[
            pltpu.VMEM((nrow, ATT_W), BF16),
            pltpu.VMEM((N_PAGES + 1, nrow, PAGE_SIZE), F32),
            pltpu.VMEM((nrow, 1), F32),
            pltpu.VMEM((nrow, 1), F32),
            pltpu.VMEM((nrow, ATT_W), F32),
            pltpu.VMEM((H_ATT, 1), F32),
        ],
    )
    return pl.pallas_call(
        _attn_sample_kernel,
        grid_spec=grid_spec,
        out_shape=jax.ShapeDtypeStruct((N_SAMPLE, ATT_W), F32),
        compiler_params=_cparams(("parallel", "arbitrary"), VMEM_LIMIT),
        name="attn_sample",
    )(page_table, u, u, u, fq, fn, *([cache_kt] * g), *([cache_vt] * g), *([wt] * g))


def _rglru_kernel(xr_ref, gr_ref, cs_ref, h0_ref, cw_ref, cb_ref, wg_ref, bg_ref, lam_ref,
                  y_ref, hN_ref, cN_ref, xbuf, abuf, bbuf, hbuf, hcar, *, tc):
    c = pl.program_id(1)
    x = xr_ref[...]

    @pl.when(c == 0)
    def _():
        xbuf[0:8, :] = jnp.zeros((8, REC_W), F32)
        xbuf[8 - (CONV_W - 1):8, :] = cs_ref[0]
        hcar[...] = h0_ref[0]

    xbuf[8:8 + tc, :] = x
    xc = cb_ref[...] + xbuf[5:5 + tc, :] * cw_ref[0:1, :]
    xc = xc + xbuf[6:6 + tc, :] * cw_ref[1:2, :]
    xc = xc + xbuf[7:7 + tc, :] * cw_ref[2:3, :]
    xc = xc + x * cw_ref[3:4, :]
    xbuf[0:8, :] = xbuf[tc:tc + 8, :]

    gates = jnp.dot(xc.astype(BF16), wg_ref[...], preferred_element_type=F32) + bg_ref[...]
    rg = jax.nn.sigmoid(gates[:, :REC_W])
    ig = jax.nn.sigmoid(gates[:, REC_W:])
    nl = -lam_ref[...]
    softplus = jnp.maximum(nl, 0.0) + jnp.log1p(jnp.exp(-jnp.abs(nl)))
    log_a = (-RG_C * softplus) * rg
    a = jnp.exp(log_a)
    abuf[...] = a
    bbuf[...] = jnp.sqrt(-jnp.tanh(log_a) * (a * a + 1.0)) * (ig * xc)

    def group(gi, h):
        base = pl.multiple_of(gi * 8, 8)
        a8 = abuf[pl.ds(base, 8), :]
        b8 = bbuf[pl.ds(base, 8), :]
        rows = []
        for r in range(8):
            h = a8[r:r + 1, :] * h + b8[r:r + 1, :]
            rows.append(h)
        hbuf[pl.ds(base, 8), :] = jnp.concatenate(rows, axis=0)
        return h

    h_last = lax.fori_loop(0, tc // 8, group, hcar[...])
    hcar[...] = h_last
    y_ref[...] = hbuf[...] * jax.nn.gelu(gr_ref[...], approximate=True)

    @pl.when(c == pl.num_programs(1) - 1)
    def _():
        hN_ref[0] = h_last
        cN_ref[0] = x[tc - (CONV_W - 1):tc, :]


def _rglru(u, conv_state, h0, cw, cb, wg, bg, lam, *, nseq, slen, tc, row0):
    nchunk = slen // tc
    blk0 = row0 // tc
    xcol = (3 * ATT_W) // REC_W
    rep = lambda shape: pl.BlockSpec(shape, lambda b, c: (0,) * len(shape))
    return pl.pallas_call(
        functools.partial(_rglru_kernel, tc=tc),
        grid=(nseq, nchunk),
        in_specs=[
            pl.BlockSpec((tc, REC_W), lambda b, c: (blk0 + b * nchunk + c, xcol)),
            pl.BlockSpec((tc, REC_W), lambda b, c: (blk0 + b * nchunk + c, xcol + 1)),
            pl.BlockSpec((1, CONV_W - 1, REC_W), lambda b, c: (b, 0, 0)),
            pl.BlockSpec((1, 1, REC_W), lambda b, c: (b, 0, 0)),
            rep((CONV_W, REC_W)), rep((1, REC_W)), rep((REC_W, 2 * REC_W)),
            rep((1, 2 * REC_W)), rep((1, REC_W)),
        ],
        out_specs=[
            pl.BlockSpec((tc, REC_W), lambda b, c: (b * nchunk + c, 0)),
            pl.BlockSpec((1, 1, REC_W), lambda b, c: (b, 0, 0)),
            pl.BlockSpec((1, CONV_W - 1, REC_W), lambda b, c: (b, 0, 0)),
        ],
        out_shape=[
            jax.ShapeDtypeStruct((nseq * slen, REC_W), F32),
            jax.ShapeDtypeStruct((nseq, 1, REC_W), F32),
            jax.ShapeDtypeStruct((nseq, CONV_W - 1, REC_W), F32),
        ],
        scratch_shapes=[
            pltpu.VMEM((tc + 8, REC_W), F32),
            pltpu.VMEM((tc, REC_W), F32),
            pltpu.VMEM((tc, REC_W), F32),
            pltpu.VMEM((tc, REC_W), F32),
            pltpu.VMEM((1, REC_W), F32),
        ],
        compiler_params=_cparams(("parallel", "arbitrary"), VMEM_LIMIT),
        name="rglru",
    )(u, u, conv_state, h0, cw, cb, wg, bg, lam)


def _layer_norm(x, g, b):
    mu = jnp.mean(x, axis=-1, keepdims=True)
    xc = x - mu
    var = jnp.mean(xc * xc, axis=-1, keepdims=True)
    return xc * lax.rsqrt(var + EPS) * g + b


def _rms_norm(x, g):
    return x * lax.rsqrt(jnp.mean(x * x, axis=-1, keepdims=True) + EPS) * g


def _route(logits, rbias):
    shape = logits.shape
    ninf = -jnp.inf
    lane = lax.broadcasted_iota(jnp.int32, shape, 1)
    lane_f = lane.astype(F32)
    gsz = N_EXPERTS // N_GROUPS
    grp_f = (lane >> 3).astype(F32)
    valid = lane < N_EXPERTS
    scores = jax.nn.sigmoid(logits)
    biased = jnp.where(valid, scores + rbias, ninf)

    def first_max(vals, idx_f):
        m = jnp.max(vals, axis=1, keepdims=True)
        i = jnp.min(jnp.where(vals == m, idx_f, 1e9), axis=1, keepdims=True)
        return m, i

    gs = jnp.full(shape, ninf, F32)
    for g in range(N_GROUPS):
        in_g = (lane >= g * gsz) & (lane < (g + 1) * gsz)
        v = jnp.where(in_g, biased, ninf)
        m1, i1 = first_max(v, lane_f)
        m2 = jnp.max(jnp.where(lane_f == i1, ninf, v), axis=1, keepdims=True)
        gs = jnp.where(in_g, m1 + m2, gs)
    keep = jnp.zeros(shape, F32)
    for _ in range(TOPK_GROUPS):
        _, gi = first_max(gs, grp_f)
        pick = grp_f == gi
        keep = jnp.where(pick, 1.0, keep)
        gs = jnp.where(pick, ninf, gs)
    cand = jnp.where(keep > 0.0, biased, ninf)
    sel = jnp.zeros(shape, F32)
    for _ in range(TOP_K):
        _, ei = first_max(cand, lane_f)
        pick = lane_f == ei
        sel = jnp.where(pick, 1.0, sel)
        cand = jnp.where(pick, ninf, cand)
    picked = jnp.where(sel > 0.0, scores, 0.0)
    gate = picked / jnp.sum(picked, axis=1, keepdims=True) * ROUTED_SCALE
    return jnp.where(lane == N_EXPERTS, 1.0, gate)


def _out_kernel(att_ref, rec_ref, x_ref, ga_ref, gr_ref, wa_ref, wr_ref, g1_ref, b1_ref,
                rw_ref, rb_ref, x1_ref, gate_ref, cnt_ref):
    na = _rms_norm(att_ref[...], ga_ref[...]).astype(BF16)
    nr = _rms_norm(rec_ref[...], gr_ref[...]).astype(BF16)
    mix = (jnp.dot(na, wa_ref[...], preferred_element_type=F32)
           + jnp.dot(nr, wr_ref[...], preferred_element_type=F32))
    x1 = _layer_norm(ALPHA * x_ref[...] + mix, g1_ref[...], b1_ref[...])
    x1_ref[...] = x1
    logits = jnp.dot(x1.astype(BF16), rw_ref[...], preferred_element_type=F32)
    gate = _route(logits, rb_ref[...])
    gate_ref[...] = gate
    lane = lax.broadcasted_iota(jnp.int32, gate.shape, 1)
    routed = (gate > 0.0) & (lane < N_EXPERTS)
    hit = jnp.where(routed, 1.0, 0.0)
    for j in range(cnt_ref.shape[0]):
        cnt_ref[j] = jnp.sum(hit[j * TT:(j + 1) * TT, :], axis=0, keepdims=True)


def _out_proj(att, rec, x, ga, gr, wa, wr, g1, b1, rw, rb, tm):
    n = x.shape[0]
    rep = lambda shape: pl.BlockSpec(shape, lambda i: (0,) * len(shape))
    return pl.pallas_call(
        _out_kernel,
        grid=(n // tm,),
        in_specs=[
            pl.BlockSpec((tm, ATT_W), lambda i: (i, 0)),
            pl.BlockSpec((tm, REC_W), lambda i: (i, 0)),
            pl.BlockSpec((tm, D_MODEL), lambda i: (i, 0)),
            rep((1, ATT_W)), rep((1, REC_W)),
            rep((ATT_W, D_MODEL)), rep((REC_W, D_MODEL)),
            rep((1, D_MODEL)), rep((1, D_MODEL)),
            rep((D_MODEL, LANES)), rep((1, LANES)),
        ],
        out_specs=[pl.BlockSpec((tm, D_MODEL), lambda i: (i, 0)),
                   pl.BlockSpec((tm, LANES), lambda i: (i, 0)),
                   pl.BlockSpec((tm // TT, 1, LANES), lambda i: (i, 0, 0))],
        out_shape=[jax.ShapeDtypeStruct((n, D_MODEL), F32),
                   jax.ShapeDtypeStruct((n, LANES), F32),
                   jax.ShapeDtypeStruct((n // TT, 1, LANES), F32)],
        compiler_params=_cparams(("parallel",), VMEM_LIMIT),
        name="out_proj_router",
    )(att, rec, x, ga, gr, wa, wr, g1, b1, rw, rb)


def _moe_plan(cnt):
    c = (cnt + ROW_Q - 1) // ROW_Q * ROW_Q
    lo = jnp.cumsum(c, axis=1) - c
    tot = jnp.sum(c, axis=0)
    cap = (tot + TG - 1) // TG * TG
    ends = jnp.cumsum(cap)
    base = ends - cap
    go = base[None, :] + jnp.cumsum(c, axis=0) - c
    nused = (ends[-1] // TG).astype(jnp.int32).reshape(1)
    nt = N_TILES
    nt3 = P_ROWS // TG
    starts = jnp.arange(nt3, dtype=jnp.int32) * TG
    te = jnp.minimum(jnp.sum((ends[None, :] <= starts[:, None]).astype(jnp.int32), axis=1), N_EXPERTS - 1)
    rows = jnp.sum(c, axis=1).astype(jnp.int32)
    tail = (cap - tot).astype(jnp.int32)
    tdst = (base + tot).astype(jnp.int32)
    pad = lambda a: jnp.pad(a.astype(F32), ((0, 0), (0, LANES - N_EXPERTS)))
    lo_col = pad(lo).reshape(nt, LANES, 1)
    lo_row = pad(lo).reshape(nt, 1, LANES)
    c_row = pad(c).reshape(nt, 1, LANES)
    return dict(c=c.astype(jnp.int32), lo=lo.astype(jnp.int32), go=go.astype(jnp.int32), rows=rows, tail=tail,
                tdst=tdst, te=te, nused=nused, lo_col=lo_col, lo_row=lo_row, c_row=c_row)


def _tile_plan(t, gate, lo_col, lo_s, dest_sc, p_sc):
    gt = gate.T
    sub = lax.broadcasted_iota(jnp.int32, (LANES, TT), 0)
    sel = (gt > 0.0) & (sub < N_EXPERTS)
    r = lax.broadcasted_iota(jnp.int32, (TT, TT), 0)
    c = lax.broadcasted_iota(jnp.int32, (TT, TT), 1)
    before = jnp.where(r < c, 1.0, 0.0).astype(BF16)
    rank = jnp.dot(jnp.where(sel, 1.0, 0.0).astype(BF16), before, preferred_element_type=F32)
    dest_sc[...] = jnp.where(sel, lo_col + rank, -1.0)
    p_sc[...] = jnp.zeros(p_sc.shape, p_sc.dtype)
    rows = lax.broadcasted_iota(jnp.int32, (SEG_MAX, TT), 0).astype(F32)

    def window(e, carry):
        lo_e = pl.multiple_of(lo_s[t, e], ROW_Q)
        d = dest_sc[pl.ds(e, 1), :]
        p_sc[pl.ds(lo_e, SEG_MAX), :] = jnp.where(rows + lo_e.astype(F32) == d, 1.0, 0.0).astype(p_sc.dtype)
        return carry

    lax.fori_loop(0, N_EXPERTS, window, 0)


def _for_chunks(count, src, dst, fn, bits=CHUNK_BITS):
    off = jnp.int32(0)
    for bit in bits:
        hit = (count & bit) != 0

        @pl.when(hit)
        def _(off=off, bit=bit):
            fn(src + off, dst + off, bit)

        off = off + (count & bit)


def _dispatch_kernel(c_s, lo_s, go_s, rows_s, tail_s, tdst_s, x_ref, gate_ref, locol_ref, xg_ref, p_ref,
                     dest_sc, xs_sc, z_sc, sem):
    t = pl.program_id(0)
    p_sc = p_ref.at[0]
    _tile_plan(t, gate_ref[...], locol_ref[0], lo_s, dest_sc, p_sc)
    xs_sc[...] = jnp.dot(p_sc[...], x_ref[...].astype(BF16), preferred_element_type=F32).astype(BF16)

    def seg_copy(src, dst, n):
        return pltpu.make_async_copy(xs_sc.at[pl.ds(pl.multiple_of(src, ROW_Q), n), :],
                                     xg_ref.at[pl.ds(pl.multiple_of(dst, ROW_Q), n), :], sem)

    def start(e, carry):
        _for_chunks(c_s[t, e], lo_s[t, e], go_s[t, e], lambda s, d, n: seg_copy(s, d, n).start())
        return carry

    lax.fori_loop(0, N_EXPERTS, start, 0)

    def zero_copy(src, dst, n):
        return pltpu.make_async_copy(z_sc.at[pl.ds(0, n), :],
                                     xg_ref.at[pl.ds(pl.multiple_of(dst, ROW_Q), n), :], sem)

    last = t == pl.num_programs(0) - 1

    @pl.when(last)
    def _():
        z_sc[...] = jnp.zeros(z_sc.shape, z_sc.dtype)

        def zstart(e, carry):
            _for_chunks(tail_s[e], 0, tdst_s[e], lambda s, d, n: zero_copy(s, d, n).start())
            return carry

        lax.fori_loop(0, N_EXPERTS, zstart, 0)

    _for_chunks(rows_s[t], 0, 0, lambda s, d, n: seg_copy(s, d, n).wait(), ROWS_BITS)

    @pl.when(last)
    def _():
        def zwait(e, carry):
            _for_chunks(tail_s[e], 0, tdst_s[e], lambda s, d, n: zero_copy(s, d, n).wait())
            return carry

        lax.fori_loop(0, N_EXPERTS, zwait, 0)


def _moe_dispatch(pl_, x1, gate):
    nt, pmax = N_TILES, P_ROWS
    grid_spec = pltpu.PrefetchScalarGridSpec(
        num_scalar_prefetch=6,
        grid=(nt,),
        in_specs=[
            pl.BlockSpec((TT, D_MODEL), lambda t, *_: (t, 0)),
            pl.BlockSpec((TT, LANES), lambda t, *_: (t, 0)),
            pl.BlockSpec((1, LANES, 1), lambda t, *_: (t, 0, 0)),
        ],
        out_specs=[pl.BlockSpec(memory_space=pl.ANY),
                   pl.BlockSpec((1, LCP, TT), lambda t, *_: (t, 0, 0))],
        scratch_shapes=[
            pltpu.VMEM((LANES, TT), F32),
            pltpu.VMEM((LCP, D_MODEL), BF16),
            pltpu.VMEM((CHUNK_BITS[0], D_MODEL), BF16),
            pltpu.SemaphoreType.DMA,
        ],
    )
    return pl.pallas_call(
        _dispatch_kernel,
        grid_spec=grid_spec,
        out_shape=[jax.ShapeDtypeStruct((pmax, D_MODEL), BF16),
                   jax.ShapeDtypeStruct((nt, LCP, TT), BF16)],
        compiler_params=_cparams(("arbitrary",), VMEM_LIMIT),
        name="moe_dispatch",
    )(pl_["c"], pl_["lo"], pl_["go"], pl_["rows"], pl_["tail"], pl_["tdst"], x1, gate, pl_["lo_col"])


def _grouped_kernel(te_s, nused_s, x_ref, w1_ref, w3_ref, w2_ref, y_ref):
    i = pl.program_id(0)

    @pl.when(i < nused_s[0])
    def _():
        x = x_ref[...]
        h1 = jnp.dot(x, w1_ref[0, 0].astype(BF16), preferred_element_type=F32)
        h3 = jnp.dot(x, w3_ref[0, 0].astype(BF16), preferred_element_type=F32)
        h = (h1 * jax.nn.sigmoid(h1)) * h3
        y_ref[...] = jnp.dot(h.astype(BF16), w2_ref[0, 0].astype(BF16), preferred_element_type=F32)


def _moe_grouped(layer, pl_, xg, w1, w3, w2):
    pmax = xg.shape[0]
    row = lambda i, te, nu: (jnp.minimum(i, nu[0] - 1), 0)
    wmap = lambda i, te, nu: (layer, te[i], 0, 0)
    grid_spec = pltpu.PrefetchScalarGridSpec(
        num_scalar_prefetch=2,
        grid=(pmax // TG,),
        in_specs=[
            pl.BlockSpec((TG, D_MODEL), row),
            pl.BlockSpec((1, 1, D_MODEL, D_EXPERT), wmap),
            pl.BlockSpec((1, 1, D_MODEL, D_EXPERT), wmap),
            pl.BlockSpec((1, 1, D_EXPERT, D_MODEL), wmap),
        ],
        out_specs=pl.BlockSpec((TG, D_MODEL), row),
    )
    return pl.pallas_call(
        _grouped_kernel,
        grid_spec=grid_spec,
        out_shape=jax.ShapeDtypeStruct((pmax, D_MODEL), F32),
        compiler_params=_cparams(("arbitrary",), VMEM_LIMIT),
        name="moe_grouped",
    )(pl_["te"], pl_["nused"], xg, w1, w3, w2)


def _combine_kernel(c_s, lo_s, go_s, rows_s, x_ref, gate_ref, p_ref, lorow_ref, crow_ref, yg_ref,
                    ws1_ref, ws3_ref, ws2_ref, g2_ref, b2_ref, o_ref, ys_sc, acc_sc, sem):
    t = pl.program_id(0)

    @pl.when(t == 0)
    def _():
        ys_sc[...] = jnp.zeros(ys_sc.shape, ys_sc.dtype)

    def seg_copy(src, dst, n):
        return pltpu.make_async_copy(yg_ref.at[pl.ds(pl.multiple_of(src, ROW_Q), n), :],
                                     ys_sc.at[pl.ds(pl.multiple_of(dst, ROW_Q), n), :], sem)

    def start(e, carry):
        _for_chunks(c_s[t, e], go_s[t, e], lo_s[t, e], lambda s, d, n: seg_copy(s, d, n).start())
        return carry

    lax.fori_loop(0, N_EXPERTS, start, 0)

    x = x_ref[...]
    xb = x.astype(BF16)
    s1 = jnp.dot(xb, ws1_ref[...], preferred_element_type=F32)
    s3 = jnp.dot(xb, ws3_ref[...], preferred_element_type=F32)
    shared = jnp.dot(((s1 * jax.nn.sigmoid(s1)) * s3).astype(BF16), ws2_ref[...], preferred_element_type=F32)
    g3 = jnp.concatenate(_split3(gate_ref[...]), axis=1)

    _for_chunks(rows_s[t], 0, 0, lambda s, d, n: seg_copy(s, d, n).wait(), ROWS_BITS)

    lo_row = lorow_ref[0]
    c_row = crow_ref[0]
    acc_sc[...] = jnp.zeros(acc_sc.shape, acc_sc.dtype)
    ck = 256
    for k0 in range(0, LCP, ck):
        @pl.when(k0 < rows_s[t])
        def _(k0=k0):
            p = p_ref[0, k0:k0 + ck, :]
            pg3 = jnp.dot(p, g3, preferred_element_type=F32)
            pg = (pg3[:, :LANES] + pg3[:, LANES:2 * LANES]) + pg3[:, 2 * LANES:]
            rowi = (lax.broadcasted_iota(jnp.int32, (ck, LANES), 0) + k0).astype(F32)
            mine = (rowi >= lo_row) & (rowi < lo_row + c_row)
            gs = jnp.sum(jnp.where(mine, pg, 0.0), axis=1, keepdims=True)
            inside = jnp.sum(jnp.where(mine, 1.0, 0.0), axis=1, keepdims=True) > 0.0
            y = jnp.where(inside, ys_sc[k0:k0 + ck, :] * gs, 0.0)
            yh = y.astype(BF16)
            yl = (y - yh.astype(F32)).astype(BF16)
            tn = (((0,), (0,)), ((), ()))
            both = lax.dot_general(p, jnp.concatenate([yh, yl], axis=1), tn, preferred_element_type=F32)
            acc_sc[...] += both[:, :D_MODEL] + both[:, D_MODEL:]
    o_ref[...] = _layer_norm(ALPHA * x + (acc_sc[...] + shared), g2_ref[...], b2_ref[...])


def _moe_combine(pl_, x1, gate, pmat, yg, ws1, ws3, ws2, g2, b2):
    nt = N_TILES
    rep = lambda shape: pl.BlockSpec(shape, lambda t, *_: (0,) * len(shape))
    grid_spec = pltpu.PrefetchScalarGridSpec(
        num_scalar_prefetch=4,
        grid=(nt,),
        in_specs=[
            pl.BlockSpec((TT, D_MODEL), lambda t, *_: (t, 0)),
            pl.BlockSpec((TT, LANES), lambda t, *_: (t, 0)),
            pl.BlockSpec((1, LCP, TT), lambda t, *_: (t, 0, 0)),
            pl.BlockSpec((1, 1, LANES), lambda t, *_: (t, 0, 0)),
            pl.BlockSpec((1, 1, LANES), lambda t, *_: (t, 0, 0)),
            pl.BlockSpec(memory_space=pl.ANY),
            rep((D_MODEL, D_EXPERT)), rep((D_MODEL, D_EXPERT)), rep((D_EXPERT, D_MODEL)),
            rep((1, D_MODEL)), rep((1, D_MODEL)),
        ],
        out_specs=pl.BlockSpec((TT, D_MODEL), lambda t, *_: (t, 0)),
        scratch_shapes=[
            pltpu.VMEM((LCP, D_MODEL), F32),
            pltpu.VMEM((TT, D_MODEL), F32),
            pltpu.SemaphoreType.DMA,
        ],
    )
    return pl.pallas_call(
        _combine_kernel,
        grid_spec=grid_spec,
        out_shape=jax.ShapeDtypeStruct((nt * TT, D_MODEL), F32),
        compiler_params=_cparams(("arbitrary",), VMEM_LIMIT),
        name="moe_combine",
    )(pl_["c"], pl_["lo"], pl_["go"], pl_["rows"], x1, gate, pmat, pl_["lo_row"], pl_["c_row"], yg,
      ws1, ws3, ws2, g2, b2)


def _block_diag(w):
    nb, k, j = w.shape
    eye = jnp.eye(nb, dtype=w.dtype)
    return jnp.einsum('nkj,nm->nkmj', w, eye).reshape(nb * k, nb * j)


def kernel(x_prompt, x_sample, cache_k, cache_v, cache_logf, state_h, state_conv, page_table, w_in, b_f, conv_w, conv_b, w_gr, b_gr, w_gi, b_gi, lam, g_att, g_rec, w_out, ln1_g, ln1_b, w_router, router_bias, w1, w3, w2, ws1, ws3, ws2, ln2_g, ln2_b):
    n_pool = cache_k.shape[1]
    x = jnp.concatenate([x_prompt.reshape(N_PROMPT, D_MODEL), x_sample.reshape(N_SAMPLE, D_MODEL)], axis=0)
    ckt = jnp.transpose(cache_k, (0, 1, 3, 4, 2)).reshape(DEPTH, n_pool, ATT_W, PAGE_SIZE)
    cvt = jnp.transpose(cache_v, (0, 1, 3, 4, 2)).reshape(DEPTH, n_pool, ATT_W, PAGE_SIZE)
    lf_t = jnp.swapaxes(cache_logf, -1, -2).reshape(DEPTH * n_pool * H_ATT, PAGE_SIZE)
    wt = _cumsum_lanes(lf_t, 4096).reshape(DEPTH, n_pool, H_ATT, PAGE_SIZE)
    zero_conv = jnp.zeros((BATCH, CONV_W - 1, REC_W), F32)
    zero_h = jnp.zeros((BATCH, 1, REC_W), F32)
    cf = 3 * ATT_W

    outs = {k: [] for k in ("kp", "vp", "fp", "hp", "cp", "ks", "vs", "fs", "hs", "cs")}
    for l in range(DEPTH):
        w_main = jnp.concatenate([w_in[l][:, :cf], w_in[l][:, cf + H_ATT:]], axis=1).astype(BF16)
        w_f = jnp.pad(w_in[l][:, cf:cf + H_ATT], ((0, 0), (0, LANES - H_ATT))).astype(BF16)
        bf_pad = jnp.pad(b_f[l], (0, LANES - H_ATT)).reshape(1, LANES)
        u = _matmul(x, w_main, TM_PROJ, TN_PROJ)
        flog = _forget_logits(x, w_f, bf_pad, TM_PROJ)[:, :H_ATT]
        logf_p = flog[:N_PROMPT].reshape(BATCH, SEQ, H_ATT)
        logf_s = flog[N_PROMPT:].reshape(DEC_BATCH, DEC_SEQ, H_ATT)

        ft = _cumsum_lanes(jnp.swapaxes(logf_p, 1, 2).reshape(BATCH * H_ATT, SEQ), BATCH * H_ATT)
        fr = ft.reshape(BATCH, H_ATT // 2, 2, SEQ)
        fc = jnp.swapaxes(fr, 2, 3)
        att_p = _attn_prompt(u, fc, fr)

        ls_t = jnp.pad(jnp.swapaxes(logf_s, 1, 2), ((0, 0), (0, 0), (0, LANES - DEC_SEQ)))
        fn = _cumsum_lanes(ls_t.reshape(DEC_BATCH * H_ATT, LANES), DEC_BATCH * H_ATT)
        fn = fn.reshape(DEC_BATCH, H_ATT, LANES)
        fq = fn[:, :, :DEC_SEQ].reshape(DEC_BATCH, H_ATT * DEC_SEQ, 1)
        att_s = _attn_sample(l, page_table, u, ckt, cvt, wt, fq, fn)

        wg = jnp.concatenate([_block_diag(w_gr[l]), _block_diag(w_gi[l])], axis=1).astype(BF16)
        bg = jnp.concatenate([b_gr[l], b_gi[l]]).reshape(1, 2 * REC_W)
        rec_args = (conv_w[l], conv_b[l].reshape(1, REC_W), wg, bg, lam[l].reshape(1, REC_W))
        rec_p, h_p, conv_p = _rglru(u, zero_conv, zero_h, *rec_args,
                                    nseq=BATCH, slen=SEQ, tc=TC_REC, row0=0)
        rec_s, h_s, conv_s = _rglru(u, state_conv[l], state_h[l].reshape(DEC_BATCH, 1, REC_W), *rec_args,
                                    nseq=DEC_BATCH, slen=DEC_SEQ, tc=DEC_SEQ, row0=N_PROMPT)

        att = jnp.concatenate([att_p, att_s], axis=0)
        rec = jnp.concatenate([rec_p, rec_s], axis=0)
        rw = jnp.pad(w_router[l], ((0, 0), (0, LANES - N_EXPERTS))).astype(BF16)
        rb = jnp.pad(router_bias[l], (0, LANES - N_EXPERTS)).reshape(1, LANES)
        x1, gate, cnt = _out_proj(att, rec, x, g_att[l].reshape(1, ATT_W), g_rec[l].reshape(1, REC_W),
                             w_out[l][:ATT_W].astype(BF16), w_out[l][ATT_W:].astype(BF16),
                             ln1_g[l].reshape(1, D_MODEL), ln1_b[l].reshape(1, D_MODEL),
                             rw, rb, TM_OUT)

        mp = _moe_plan(cnt[:, 0, :N_EXPERTS].astype(jnp.int32))
        xg, pmat = _moe_dispatch(mp, x1, gate)
        yg = _moe_grouped(l, mp, xg, w1, w3, w2)
        x = _moe_combine(mp, x1, gate, pmat, yg, ws1[l].astype(BF16), ws3[l].astype(BF16), ws2[l].astype(BF16),
                         ln2_g[l].reshape(1, D_MODEL), ln2_b[l].reshape(1, D_MODEL))

        kv = lambda rows, c0, shape: u[rows, c0:c0 + ATT_W].reshape(shape)
        p_rows, s_rows = slice(0, N_PROMPT), slice(N_PROMPT, N_TOK)
        p_shape = (BATCH, SEQ, H_ATT, HEAD_DIM)
        s_shape = (DEC_BATCH, DEC_SEQ, H_ATT, HEAD_DIM)
        outs["kp"].append(kv(p_rows, ATT_W, p_shape))
        outs["vp"].append(kv(p_rows, 2 * ATT_W, p_shape))
        outs["fp"].append(logf_p)
        outs["hp"].append(h_p.reshape(BATCH, REC_W))
        outs["cp"].append(conv_p)
        outs["ks"].append(kv(s_rows, ATT_W, s_shape))
        outs["vs"].append(kv(s_rows, 2 * ATT_W, s_shape))
        outs["fs"].append(logf_s)
        outs["hs"].append(h_s.reshape(DEC_BATCH, REC_W))
        outs["cs"].append(conv_s)

    st = lambda name: jnp.stack(outs[name])
    return (x[:N_PROMPT].reshape(BATCH, SEQ, D_MODEL), x[N_PROMPT:].reshape(DEC_BATCH, DEC_SEQ, D_MODEL),
            st("kp"), st("vp"), st("fp"), st("hp"), st("cp"),
            st("ks"), st("vs"), st("fs"), st("hs"), st("cs"))
```

```python
import functools

import jax
import jax.numpy as jnp
from jax import lax
from jax.experimental import pallas as pl
from jax.experimental.pallas import tpu as pltpu

F32 = jnp.float32
BF16 = jnp.bfloat16

D_MODEL = 1024
BATCH = 4
SEQ = 4096
DEPTH = 2
DEC_BATCH = 32
DEC_SEQ = 8
PAST_LEN = 8192
PAGE_SIZE = 128
ATT_W = 512
REC_W = 512
HEAD_DIM = 64
H_ATT = 8
N_REC_BLOCKS = 8
CONV_W = 4
RG_C = 8.0
N_EXPERTS = 64
TOP_K = 8
N_GROUPS = 8
TOPK_GROUPS = 4
D_EXPERT = 256
ROUTED_SCALE = 2.5
ALPHA = (2 * DEPTH) ** 0.25
EPS = 1e-5
ATT_SCALE = HEAD_DIM ** -0.5

N_PROMPT = BATCH * SEQ
N_SAMPLE = DEC_BATCH * DEC_SEQ
N_TOK = N_PROMPT + N_SAMPLE
N_PAGES = PAST_LEN // PAGE_SIZE
U_W = 3 * ATT_W + 2 * REC_W
LANES = 128
NEG = -1e30

TM_PROJ = 1280
TN_PROJ = 640
TM_OUT = 1280
TT = 256
ROW_Q = 16
SEG_MAX = TT
LC = TT * TOP_K + N_EXPERTS * ROW_Q
LCP = LC + SEG_MAX
TG = 512
CHUNK_BITS = (256, 128, 64, 32, 16)
ROWS_BITS = (2048, 1024, 512) + CHUNK_BITS
N_TILES = N_TOK // TT
P_ROWS = -(-(N_TOK * TOP_K + N_TILES * N_EXPERTS * (ROW_Q - 1) + N_EXPERTS * (TG - 1)) // TG) * TG
TQ = 512
TC_REC = 512
PAGES_PER_STEP = 16
VMEM_LIMIT = 56 * 1024 * 1024


def _cparams(sem, vmem=None):
    return pltpu.CompilerParams(dimension_semantics=sem, vmem_limit_bytes=vmem)


def _mm_kernel(x_ref, w_ref, o_ref):
    o_ref[...] = jnp.dot(x_ref[...].astype(BF16), w_ref[...], preferred_element_type=F32)


def _matmul(x, w, tm, tn):
    m, k = x.shape
    n = w.shape[1]
    return pl.pallas_call(
        _mm_kernel,
        grid=(m // tm, n // tn),
        in_specs=[pl.BlockSpec((tm, k), lambda i, j: (i, 0)),
                  pl.BlockSpec((k, tn), lambda i, j: (0, j))],
        out_specs=pl.BlockSpec((tm, tn), lambda i, j: (i, j)),
        out_shape=jax.ShapeDtypeStruct((m, n), F32),
        compiler_params=_cparams(("parallel", "arbitrary"), VMEM_LIMIT),
        name="in_proj",
    )(x, w)


def _forget_kernel(x_ref, w_ref, b_ref, o_ref):
    z = jnp.dot(x_ref[...].astype(BF16), w_ref[...], preferred_element_type=F32) + b_ref[...]
    o_ref[...] = jnp.minimum(z, 0.0) - jnp.log1p(jnp.exp(-jnp.abs(z)))


def _forget_logits(x, w, b, tm):
    m, k = x.shape
    n = w.shape[1]
    return pl.pallas_call(
        _forget_kernel,
        grid=(m // tm,),
        in_specs=[pl.BlockSpec((tm, k), lambda i: (i, 0)),
                  pl.BlockSpec((k, n), lambda i: (0, 0)),
                  pl.BlockSpec((1, n), lambda i: (0, 0))],
        out_specs=pl.BlockSpec((tm, n), lambda i: (i, 0)),
        out_shape=jax.ShapeDtypeStruct((m, n), F32),
        compiler_params=_cparams(("parallel",), VMEM_LIMIT),
        name="forget_logits",
    )(x, w, b)


def _split3(x):
    hi = x.astype(BF16)
    r1 = x - hi.astype(F32)
    mid = r1.astype(BF16)
    lo = (r1 - mid.astype(F32)).astype(BF16)
    return hi, mid, lo


def _cumsum_kernel(x_ref, o_ref, *, nchunk):
    rows = x_ref.shape[0]
    r = lax.broadcasted_iota(jnp.int32, (LANES, LANES), 0)
    c = lax.broadcasted_iota(jnp.int32, (LANES, LANES), 1)
    upper = jnp.where(r <= c, 1.0, 0.0).astype(BF16)
    carry = jnp.zeros((rows, 1), F32)
    for ch in range(nchunk):
        x = x_ref[:, ch * LANES:(ch + 1) * LANES]
        hi, mid, lo = _split3(x)
        w = (jnp.dot(hi, upper, preferred_element_type=F32)
             + jnp.dot(mid, upper, preferred_element_type=F32)
             + jnp.dot(lo, upper, preferred_element_type=F32)) + carry
        o_ref[:, ch * LANES:(ch + 1) * LANES] = w
        carry = w[:, LANES - 1:LANES]


def _cumsum_lanes(x, tr):
    rows, s = x.shape
    return pl.pallas_call(
        functools.partial(_cumsum_kernel, nchunk=s // LANES),
        grid=(rows // tr,),
        in_specs=[pl.BlockSpec((tr, s), lambda i: (i, 0))],
        out_specs=pl.BlockSpec((tr, s), lambda i: (i, 0)),
        out_shape=jax.ShapeDtypeStruct((rows, s), F32),
        compiler_params=_cparams(("parallel",)),
        name="cumsum_lanes",
    )(x)


def _attn_prompt_kernel(q_ref, k_ref, v_ref, fc_ref, fr_ref, o_ref, s_sc, kt_sc, vb_sc):
    qb = pl.program_id(2)
    nq = SEQ // TQ

    @pl.when(qb == 0)
    def _():
        for c in range(nq):
            kt_sc[c] = k_ref[c * TQ:(c + 1) * TQ, :].T.astype(BF16)
        vb_sc[...] = v_ref[...].astype(BF16)

    lane = lax.broadcasted_iota(jnp.int32, (TQ, LANES), 1)
    row = lax.broadcasted_iota(jnp.int32, (TQ, TQ), 0)
    col = lax.broadcasted_iota(jnp.int32, (TQ, TQ), 1)
    q = q_ref[...] * ATT_SCALE
    fc = fc_ref[0, 0]
    qm = [jnp.where(lane < HEAD_DIM, q, 0.0).astype(BF16), jnp.where(lane >= HEAD_DIM, q, 0.0).astype(BF16)]
    fq = [fc[:, 0:1], fc[:, 1:2]]

    def scores(kb, ms, masked):
        k0 = pl.multiple_of(kb * TQ, TQ)
        kt = kt_sc[kb]
        out = []
        for h in range(2):
            fk = fr_ref[0, 0, h:h + 1, pl.ds(k0, TQ)]
            s = jnp.dot(qm[h], kt, preferred_element_type=F32) + (fq[h] - fk)
            if masked:
                s = jnp.where(col <= row, s, NEG)
            s_sc[h, kb] = s
            out.append(jnp.maximum(ms[h], jnp.max(s, axis=1, keepdims=True)))
        return tuple(out)

    neg = jnp.full((TQ, 1), NEG, F32)
    ms = lax.fori_loop(0, qb, lambda kb, c: scores(kb, c, False), (neg, neg))
    ms = scores(qb, ms, True)

    def exps(kb, ls):
        out = []
        for h in range(2):
            e = jnp.exp(s_sc[h, kb] - ms[h])
            s_sc[h, kb] = e
            out.append(ls[h] + jnp.sum(e, axis=1, keepdims=True))
        return tuple(out)

    zero = jnp.zeros((TQ, 1), F32)
    ls = lax.fori_loop(0, qb + 1, exps, (zero, zero))
    rs = [1.0 / ls[0], 1.0 / ls[1]]

    def values(kb, accs):
        k0 = pl.multiple_of(kb * TQ, TQ)
        v = vb_sc[pl.ds(k0, TQ), :]
        return tuple(accs[h] + jnp.dot((s_sc[h, kb] * rs[h]).astype(BF16), v, preferred_element_type=F32)
                     for h in range(2))

    zacc = jnp.zeros((TQ, LANES), F32)
    outs = lax.fori_loop(0, qb + 1, values, (zacc, zacc))
    o_ref[...] = jnp.where(lane < HEAD_DIM, outs[0], outs[1])


def _attn_prompt(u, fc, fr):
    nq = SEQ // TQ
    kblk = ATT_W // LANES
    return pl.pallas_call(
        _attn_prompt_kernel,
        grid=(BATCH, H_ATT // 2, nq),
        in_specs=[
            pl.BlockSpec((TQ, LANES), lambda b, p, i: (b * nq + i, p)),
            pl.BlockSpec((SEQ, LANES), lambda b, p, i: (b, kblk + p)),
            pl.BlockSpec((SEQ, LANES), lambda b, p, i: (b, 2 * kblk + p)),
            pl.BlockSpec((1, 1, TQ, 2), lambda b, p, i: (b, p, i, 0)),
            pl.BlockSpec((1, 1, 2, SEQ), lambda b, p, i: (b, p, 0, 0)),
        ],
        out_specs=pl.BlockSpec((TQ, LANES), lambda b, p, i: (b * nq + i, p)),
        out_shape=jax.ShapeDtypeStruct((N_PROMPT, ATT_W), F32),
        scratch_shapes=[pltpu.VMEM((2, nq, TQ, TQ), F32), pltpu.VMEM((nq, LANES, TQ), BF16),
                        pltpu.VMEM((SEQ, LANES), BF16)],
        compiler_params=_cparams(("parallel", "parallel", "arbitrary"), VMEM_LIMIT),
        name="attn_prompt",
    )(u, u, u, fc, fr)


def _expand_heads(x):
    n = x.shape[1]
    return jnp.concatenate(
        [jnp.broadcast_to(x[h:h + 1, :], (DEC_SEQ, n)) for h in range(H_ATT)], axis=0)


def _attn_sample_kernel(pt_ref, q_ref, kn_ref, vn_ref, fq_ref, fn_ref, *rest):
    del pt_ref
    g = PAGES_PER_STEP
    nstep = N_PAGES // g
    k_refs, v_refs, w_refs = rest[:g], rest[g:2 * g], rest[2 * g:3 * g]
    o_ref = rest[3 * g]
    qbd_sc, s_sc, m_sc, r_sc, acc_sc, suf_sc = rest[3 * g + 1:]
    step = pl.program_id(1)
    nrow = H_ATT * DEC_SEQ
    row = lax.broadcasted_iota(jnp.int32, (nrow, ATT_W), 0)
    colw = lax.broadcasted_iota(jnp.int32, (nrow, ATT_W), 1)
    own_head = (colw >> 6) == (row >> 3)
    fq = fq_ref[0]
    pad = jnp.zeros((PAGE_SIZE - DEC_SEQ, ATT_W), F32)
    nt = (((1,), (1,)), ((), ()))

    @pl.when(step == 0)
    def _():
        q = q_ref[...] * ATT_SCALE
        qt = jnp.concatenate([q] * H_ATT, axis=0)
        qbd_sc[...] = jnp.where(own_head, qt, 0.0).astype(BF16)
        suf_sc[...] = jnp.zeros((H_ATT, 1), F32)
        kn = jnp.concatenate([kn_ref[...], pad], axis=0).astype(BF16)
        s = lax.dot_general(qbd_sc[...], kn, nt, preferred_element_type=F32)
        s = s + (fq - _expand_heads(fn_ref[0]))
        r = lax.broadcasted_iota(jnp.int32, (nrow, PAGE_SIZE), 0)
        c = lax.broadcasted_iota(jnp.int32, (nrow, PAGE_SIZE), 1)
        s = jnp.where(c <= (r & (DEC_SEQ - 1)), s, NEG)
        s_sc[N_PAGES] = s
        m_sc[...] = jnp.max(s, axis=1, keepdims=True)

    @pl.when(step < nstep)
    def _():
        first = (nstep - 1 - step) * g
        qbd = qbd_sc[...]
        suf = suf_sc[...]
        mrun = jnp.full((nrow, PAGE_SIZE), NEG, F32)
        for i in range(g - 1, -1, -1):
            w = w_refs[i][0, 0]
            suf = suf + w[:, PAGE_SIZE - 1:PAGE_SIZE]
            kt = k_refs[i][0, 0].astype(BF16)
            s = jnp.dot(qbd, kt, preferred_element_type=F32)
            s = s + (fq + _expand_heads(suf - w))
            s_sc[first + i] = s
            mrun = jnp.maximum(mrun, s)
        suf_sc[...] = suf
        m_sc[...] = jnp.maximum(m_sc[...], jnp.max(mrun, axis=1, keepdims=True))

    @pl.when(step == nstep - 1)
    def _():
        m = m_sc[...]

        def slot(p, lacc):
            ex = jnp.exp(s_sc[p] - m)
            s_sc[p] = ex
            return lacc + ex

        lacc = lax.fori_loop(0, N_PAGES + 1, slot, jnp.zeros((nrow, PAGE_SIZE), F32))
        r_sc[...] = 1.0 / jnp.sum(lacc, axis=1, keepdims=True)

    @pl.when(step >= nstep)
    def _():
        first = (2 * nstep - 1 - step) * g
        r = r_sc[...]

        @pl.when(step == nstep)
        def _():
            vn = jnp.concatenate([vn_ref[...], pad], axis=0).astype(BF16)
            acc_sc[...] = jnp.dot((s_sc[N_PAGES] * r).astype(BF16), vn, preferred_element_type=F32)

        acc = acc_sc[...]
        for i in range(g - 1, -1, -1):
            vt = v_refs[i][0, 0].astype(BF16)
            p = (s_sc[first + i] * r).astype(BF16)
            acc = acc + lax.dot_general(p, vt, nt, preferred_element_type=F32)
        acc_sc[...] = acc

    @pl.when(step == 2 * nstep - 1)
    def _():
        o = jnp.where(own_head, acc_sc[...], 0.0)
        out = o[0:DEC_SEQ, :]
        for h in range(1, H_ATT):
            out = out + o[h * DEC_SEQ:(h + 1) * DEC_SEQ, :]
        o_ref[...] = out


def _attn_sample(layer, page_table, u, cache_kt, cache_vt, wt, fq, fn):
    g = PAGES_PER_STEP
    nstep = N_PAGES // g
    srow = N_PROMPT // DEC_SEQ
    nrow = H_ATT * DEC_SEQ

    def key_map(i):
        return lambda b, s, pt: (layer, pt[b, (nstep - 1 - jnp.minimum(s, nstep - 1)) * g + i], 0, 0)

    def value_map(i):
        return lambda b, s, pt: (layer, pt[b, (nstep - 1 - jnp.maximum(s - nstep, 0)) * g + i], 0, 0)

    in_specs = [
        pl.BlockSpec((DEC_SEQ, ATT_W), lambda b, s, pt: (srow + b, 0)),
        pl.BlockSpec((DEC_SEQ, ATT_W), lambda b, s, pt: (srow + b, 1)),
        pl.BlockSpec((DEC_SEQ, ATT_W), lambda b, s, pt: (srow + b, 2)),
        pl.BlockSpec((1, nrow, 1), lambda b, s, pt: (b, 0, 0)),
        pl.BlockSpec((1, H_ATT, LANES), lambda b, s, pt: (b, 0, 0)),
    ]
    in_specs += [pl.BlockSpec((1, 1, ATT_W, PAGE_SIZE), key_map(i)) for i in range(g)]
    in_specs += [pl.BlockSpec((1, 1, ATT_W, PAGE_SIZE), value_map(i)) for i in range(g)]
    in_specs += [pl.BlockSpec((1, 1, H_ATT, PAGE_SIZE), key_map(i)) for i in range(g)]
    grid_spec = pltpu.PrefetchScalarGridSpec(
        num_scalar_prefetch=1,
        grid=(DEC_BATCH, 2 * nstep),
        in_specs=in_specs,
        out_specs=pl.BlockSpec((DEC_SEQ, ATT_W), lambda b, s, pt: (b, 0)),
        scratch_shapes=[
            pltpu.VMEM((nrow, ATT_W), BF16),
            pltpu.VMEM((N_PAGES + 1, nrow, PAGE_SIZE), F32),
            pltpu.VMEM((nrow, 1), F32),
            pltpu.VMEM((nrow, 1), F32),
            pltpu.VMEM((nrow, ATT_W), F32),
            pltpu.VMEM((H_ATT, 1), F32),
        ],
    )
    return pl.pallas_call(
        _attn_sample_kernel,
        grid_spec=grid_spec,
        out_shape=jax.ShapeDtypeStruct((N_SAMPLE, ATT_W), F32),
        compiler_params=_cparams(("parallel", "arbitrary"), VMEM_LIMIT),
        name="attn_sample",
    )(page_table, u, u, u, fq, fn, *([cache_kt] * g), *([cache_vt] * g), *([wt] * g))


def _rglru_kernel(xr_ref, gr_ref, cs_ref, h0_ref, cw_ref, cb_ref, wg_ref, bg_ref, lam_ref,
                  y_ref, hN_ref, cN_ref, xbuf, abuf, bbuf, hbuf, hcar, *, tc):
    c = pl.program_id(1)
    x = xr_ref[...]

    @pl.when(c == 0)
    def _():
        xbuf[0:8, :] = jnp.zeros((8, REC_W), F32)
        xbuf[8 - (CONV_W - 1):8, :] = cs_ref[0]
        hcar[...] = h0_ref[0]

    xbuf[8:8 + tc, :] = x
    xc = cb_ref[...] + xbuf[5:5 + tc, :] * cw_ref[0:1, :]
    xc = xc + xbuf[6:6 + tc, :] * cw_ref[1:2, :]
    xc = xc + xbuf[7:7 + tc, :] * cw_ref[2:3, :]
    xc = xc + x * cw_ref[3:4, :]
    xbuf[0:8, :] = xbuf[tc:tc + 8, :]

    gates = jnp.dot(xc.astype(BF16), wg_ref[...], preferred_element_type=F32) + bg_ref[...]
    rg = jax.nn.sigmoid(gates[:, :REC_W])
    ig = jax.nn.sigmoid(gates[:, REC_W:])
    nl = -lam_ref[...]
    softplus = jnp.maximum(nl, 0.0) + jnp.log1p(jnp.exp(-jnp.abs(nl)))
    log_a = (-RG_C * softplus) * rg
    a = jnp.exp(log_a)
    abuf[...] = a
    bbuf[...] = jnp.sqrt(-jnp.tanh(log_a) * (a * a + 1.0)) * (ig * xc)

    def group(gi, h):
        base = pl.multiple_of(gi * 8, 8)
        a8 = abuf[pl.ds(base, 8), :]
        b8 = bbuf[pl.ds(base, 8), :]
        rows = []
        for r in range(8):
            h = a8[r:r + 1, :] * h + b8[r:r + 1, :]
            rows.append(h)
        hbuf[pl.ds(base, 8), :] = jnp.concatenate(rows, axis=0)
        return h

    h_last = lax.fori_loop(0, tc // 8, group, hcar[...])
    hcar[...] = h_last
    y_ref[...] = hbuf[...] * jax.nn.gelu(gr_ref[...], approximate=True)

    @pl.when(c == pl.num_programs(1) - 1)
    def _():
        hN_ref[0] = h_last
        cN_ref[0] = x[tc - (CONV_W - 1):tc, :]


def _rglru(u, conv_state, h0, cw, cb, wg, bg, lam, *, nseq, slen, tc, row0):
    nchunk = slen // tc
    blk0 = row0 // tc
    xcol = (3 * ATT_W) // REC_W
    rep = lambda shape: pl.BlockSpec(shape, lambda b, c: (0,) * len(shape))
    return pl.pallas_call(
        functools.partial(_rglru_kernel, tc=tc),
        grid=(nseq, nchunk),
        in_specs=[
            pl.BlockSpec((tc, REC_W), lambda b, c: (blk0 + b * nchunk + c, xcol)),
            pl.BlockSpec((tc, REC_W), lambda b, c: (blk0 + b * nchunk + c, xcol + 1)),
            pl.BlockSpec((1, CONV_W - 1, REC_W), lambda b, c: (b, 0, 0)),
            pl.BlockSpec((1, 1, REC_W), lambda b, c: (b, 0, 0)),
            rep((CONV_W, REC_W)), rep((1, REC_W)), rep((REC_W, 2 * REC_W)),
            rep((1, 2 * REC_W)), rep((1, REC_W)),
        ],
        out_specs=[
            pl.BlockSpec((tc, REC_W), lambda b, c: (b * nchunk + c, 0)),
            pl.BlockSpec((1, 1, REC_W), lambda b, c: (b, 0, 0)),
            pl.BlockSpec((1, CONV_W - 1, REC_W), lambda b, c: (b, 0, 0)),
        ],
        out_shape=[
            jax.ShapeDtypeStruct((nseq * slen, REC_W), F32),
            jax.ShapeDtypeStruct((nseq, 1, REC_W), F32),
            jax.ShapeDtypeStruct((nseq, CONV_W - 1, REC_W), F32),
        ],
        scratch_shapes=[
            pltpu.VMEM((tc + 8, REC_W), F32),
            pltpu.VMEM((tc, REC_W), F32),
            pltpu.VMEM((tc, REC_W), F32),
            pltpu.VMEM((tc, REC_W), F32),
            pltpu.VMEM((1, REC_W), F32),
        ],
        compiler_params=_cparams(("parallel", "arbitrary"), VMEM_LIMIT),
        name="rglru",
    )(u, u, conv_state, h0, cw, cb, wg, bg, lam)


def _layer_norm(x, g, b):
    mu = jnp.mean(x, axis=-1, keepdims=True)
    xc = x - mu
    var = jnp.mean(xc * xc, axis=-1, keepdims=True)
    return xc * lax.rsqrt(var + EPS) * g + b


def _rms_norm(x, g):
    return x * lax.rsqrt(jnp.mean(x * x, axis=-1, keepdims=True) + EPS) * g


def _route(logits, rbias):
    shape = logits.shape
    ninf = -jnp.inf
    lane = lax.broadcasted_iota(jnp.int32, shape, 1)
    lane_f = lane.astype(F32)
    gsz = N_EXPERTS // N_GROUPS
    grp_f = (lane >> 3).astype(F32)
    valid = lane < N_EXPERTS
    scores = jax.nn.sigmoid(logits)
    biased = jnp.where(valid, scores + rbias, ninf)

    def first_max(vals, idx_f):
        m = jnp.max(vals, axis=1, keepdims=True)
        i = jnp.min(jnp.where(vals == m, idx_f, 1e9), axis=1, keepdims=True)
        return m, i

    gs = jnp.full(shape, ninf, F32)
    for g in range(N_GROUPS):
        in_g = (lane >= g * gsz) & (lane < (g + 1) * gsz)
        v = jnp.where(in_g, biased, ninf)
        m1, i1 = first_max(v, lane_f)
        m2 = jnp.max(jnp.where(lane_f == i1, ninf, v), axis=1, keepdims=True)
        gs = jnp.where(in_g, m1 + m2, gs)
    keep = jnp.zeros(shape, F32)
    for _ in range(TOPK_GROUPS):
        _, gi = first_max(gs, grp_f)
        pick = grp_f == gi
        keep = jnp.where(pick, 1.0, keep)
        gs = jnp.where(pick, ninf, gs)
    cand = jnp.where(keep > 0.0, biased, ninf)
    sel = jnp.zeros(shape, F32)
    for _ in range(TOP_K):
        _, ei = first_max(cand, lane_f)
        pick = lane_f == ei
        sel = jnp.where(pick, 1.0, sel)
        cand = jnp.where(pick, ninf, cand)
    picked = jnp.where(sel > 0.0, scores, 0.0)
    gate = picked / jnp.sum(picked, axis=1, keepdims=True) * ROUTED_SCALE
    return jnp.where(lane == N_EXPERTS, 1.0, gate)


def _out_kernel(att_ref, rec_ref, x_ref, ga_ref, gr_ref, wa_ref, wr_ref, g1_ref, b1_ref,
                rw_ref, rb_ref, x1_ref, gate_ref, cnt_ref):
    na = _rms_norm(att_ref[...], ga_ref[...]).astype(BF16)
    nr = _rms_norm(rec_ref[...], gr_ref[...]).astype(BF16)
    mix = (jnp.dot(na, wa_ref[...], preferred_element_type=F32)
           + jnp.dot(nr, wr_ref[...], preferred_element_type=F32))
    x1 = _layer_norm(ALPHA * x_ref[...] + mix, g1_ref[...], b1_ref[...])
    x1_ref[...] = x1
    logits = jnp.dot(x1.astype(BF16), rw_ref[...], preferred_element_type=F32)
    gate = _route(logits, rb_ref[...])
    gate_ref[...] = gate
    lane = lax.broadcasted_iota(jnp.int32, gate.shape, 1)
    routed = (gate > 0.0) & (lane < N_EXPERTS)
    hit = jnp.where(routed, 1.0, 0.0)
    for j in range(cnt_ref.shape[0]):
        cnt_ref[j] = jnp.sum(hit[j * TT:(j + 1) * TT, :], axis=0, keepdims=True)


def _out_proj(att, rec, x, ga, gr, wa, wr, g1, b1, rw, rb, tm):
    n = x.shape[0]
    rep = lambda shape: pl.BlockSpec(shape, lambda i: (0,) * len(shape))
    return pl.pallas_call(
        _out_kernel,
        grid=(n // tm,),
        in_specs=[
            pl.BlockSpec((tm, ATT_W), lambda i: (i, 0)),
            pl.BlockSpec((tm, REC_W), lambda i: (i, 0)),
            pl.BlockSpec((tm, D_MODEL), lambda i: (i, 0)),
            rep((1, ATT_W)), rep((1, REC_W)),
            rep((ATT_W, D_MODEL)), rep((REC_W, D_MODEL)),
            rep((1, D_MODEL)), rep((1, D_MODEL)),
            rep((D_MODEL, LANES)), rep((1, LANES)),
        ],
        out_specs=[pl.BlockSpec((tm, D_MODEL), lambda i: (i, 0)),
                   pl.BlockSpec((tm, LANES), lambda i: (i, 0)),
                   pl.BlockSpec((tm // TT, 1, LANES), lambda i: (i, 0, 0))],
        out_shape=[jax.ShapeDtypeStruct((n, D_MODEL), F32),
                   jax.ShapeDtypeStruct((n, LANES), F32),
                   jax.ShapeDtypeStruct((n // TT, 1, LANES), F32)],
        compiler_params=_cparams(("parallel",), VMEM_LIMIT),
        name="out_proj_router",
    )(att, rec, x, ga, gr, wa, wr, g1, b1, rw, rb)


def _moe_plan(cnt):
    c = (cnt + ROW_Q - 1) // ROW_Q * ROW_Q
    lo = jnp.cumsum(c, axis=1) - c
    tot = jnp.sum(c, axis=0)
    cap = (tot + TG - 1) // TG * TG
    ends = jnp.cumsum(cap)
    base = ends - cap
    go = base[None, :] + jnp.cumsum(c, axis=0) - c
    nused = (ends[-1] // TG).astype(jnp.int32).reshape(1)
    nt = N_TILES
    nt3 = P_ROWS // TG
    starts = jnp.arange(nt3, dtype=jnp.int32) * TG
    te = jnp.minimum(jnp.sum((ends[None, :] <= starts[:, None]).astype(jnp.int32), axis=1), N_EXPERTS - 1)
    rows = jnp.sum(c, axis=1).astype(jnp.int32)
    tail = (cap - tot).astype(jnp.int32)
    tdst = (base + tot).astype(jnp.int32)
    pad = lambda a: jnp.pad(a.astype(F32), ((0, 0), (0, LANES - N_EXPERTS)))
    lo_col = pad(lo).reshape(nt, LANES, 1)
    lo_row = pad(lo).reshape(nt, 1, LANES)
    c_row = pad(c).reshape(nt, 1, LANES)
    return dict(c=c.astype(jnp.int32), lo=lo.astype(jnp.int32), go=go.astype(jnp.int32), rows=rows, tail=tail,
                tdst=tdst, te=te, nused=nused, lo_col=lo_col, lo_row=lo_row, c_row=c_row)


def _tile_plan(t, gate, lo_col, lo_s, dest_sc, p_sc):
    gt = gate.T
    sub = lax.broadcasted_iota(jnp.int32, (LANES, TT), 0)
    sel = (gt > 0.0) & (sub < N_EXPERTS)
    r = lax.broadcasted_iota(jnp.int32, (TT, TT), 0)
    c = lax.broadcasted_iota(jnp.int32, (TT, TT), 1)
    before = jnp.where(r < c, 1.0, 0.0).astype(BF16)
    rank = jnp.dot(jnp.where(sel, 1.0, 0.0).astype(BF16), before, preferred_element_type=F32)
    dest_sc[...] = jnp.where(sel, lo_col + rank, -1.0)
    p_sc[...] = jnp.zeros(p_sc.shape, p_sc.dtype)
    rows = lax.broadcasted_iota(jnp.int32, (SEG_MAX, TT), 0).astype(F32)

    def window(e, carry):
        lo_e = pl.multiple_of(lo_s[t, e], ROW_Q)
        d = dest_sc[pl.ds(e, 1), :]
        p_sc[pl.ds(lo_e, SEG_MAX), :] = jnp.where(rows + lo_e.astype(F32) == d, 1.0, 0.0).astype(p_sc.dtype)
        return carry

    lax.fori_loop(0, N_EXPERTS, window, 0)


def _for_chunks(count, src, dst, fn, bits=CHUNK_BITS):
    small = 4 * ROW_Q

    def pieces(some_bits):
        for bit in some_bits:
            off = count & ~(2 * bit - 1)

            @pl.when((count & bit) != 0)
            def _(off=off, bit=bit):
                fn(src + off, dst + off, bit)

    @pl.when(count >= small)
    def _():
        pieces([b for b in bits if b >= small])

    pieces([b for b in bits if b < small])


def _dispatch_kernel(c_s, lo_s, go_s, rows_s, tail_s, tdst_s, x_ref, gate_ref, locol_ref, xg_ref, p_ref,
                     dest_sc, xs_sc, z_sc, sem):
    t = pl.program_id(0)
    p_sc = p_ref.at[0]
    _tile_plan(t, gate_ref[...], locol_ref[0], lo_s, dest_sc, p_sc)
    xs_sc[...] = jnp.dot(p_sc[...], x_ref[...].astype(BF16), preferred_element_type=F32).astype(BF16)

    def seg_copy(src, dst, n):
        return pltpu.make_async_copy(xs_sc.at[pl.ds(pl.multiple_of(src, ROW_Q), n), :],
                                     xg_ref.at[pl.ds(pl.multiple_of(dst, ROW_Q), n), :], sem)

    def start(e, carry):
        _for_chunks(c_s[t, e], lo_s[t, e], go_s[t, e], lambda s, d, n: seg_copy(s, d, n).start())
        return carry

    lax.fori_loop(0, N_EXPERTS, start, 0)

    def zero_copy(src, dst, n):
        return pltpu.make_async_copy(z_sc.at[pl.ds(0, n), :],
                                     xg_ref.at[pl.ds(pl.multiple_of(dst, ROW_Q), n), :], sem)

    last = t == pl.num_programs(0) - 1

    @pl.when(last)
    def _():
        z_sc[...] = jnp.zeros(z_sc.shape, z_sc.dtype)

        def zstart(e, carry):
            _for_chunks(tail_s[e], 0, tdst_s[e], lambda s, d, n: zero_copy(s, d, n).start())
            return carry

        lax.fori_loop(0, N_EXPERTS, zstart, 0)

    _for_chunks(rows_s[t], 0, 0, lambda s, d, n: seg_copy(s, d, n).wait(), ROWS_BITS)

    @pl.when(last)
    def _():
        def zwait(e, carry):
            _for_chunks(tail_s[e], 0, tdst_s[e], lambda s, d, n: zero_copy(s, d, n).wait())
            return carry

        lax.fori_loop(0, N_EXPERTS, zwait, 0)


def _moe_dispatch(pl_, x1, gate):
    nt, pmax = N_TILES, P_ROWS
    grid_spec = pltpu.PrefetchScalarGridSpec(
        num_scalar_prefetch=6,
        grid=(nt,),
        in_specs=[
            pl.BlockSpec((TT, D_MODEL), lambda t, *_: (t, 0)),
            pl.BlockSpec((TT, LANES), lambda t, *_: (t, 0)),
            pl.BlockSpec((1, LANES, 1), lambda t, *_: (t, 0, 0)),
        ],
        out_specs=[pl.BlockSpec(memory_space=pl.ANY),
                   pl.BlockSpec((1, LCP, TT), lambda t, *_: (t, 0, 0))],
        scratch_shapes=[
            pltpu.VMEM((LANES, TT), F32),
            pltpu.VMEM((LCP, D_MODEL), BF16),
            pltpu.VMEM((CHUNK_BITS[0], D_MODEL), BF16),
            pltpu.SemaphoreType.DMA,
        ],
    )
    return pl.pallas_call(
        _dispatch_kernel,
        grid_spec=grid_spec,
        out_shape=[jax.ShapeDtypeStruct((pmax, D_MODEL), BF16),
                   jax.ShapeDtypeStruct((nt, LCP, TT), BF16)],
        compiler_params=_cparams(("arbitrary",), VMEM_LIMIT),
        name="moe_dispatch",
    )(pl_["c"], pl_["lo"], pl_["go"], pl_["rows"], pl_["tail"], pl_["tdst"], x1, gate, pl_["lo_col"])


def _grouped_kernel(te_s, nused_s, x_ref, w1_ref, w3_ref, w2_ref, y_ref):
    i = pl.program_id(0)

    @pl.when(i < nused_s[0])
    def _():
        x = x_ref[...]
        h1 = jnp.dot(x, w1_ref[0, 0].astype(BF16), preferred_element_type=F32)
        h3 = jnp.dot(x, w3_ref[0, 0].astype(BF16), preferred_element_type=F32)
        h = (h1 * jax.nn.sigmoid(h1)) * h3
        y_ref[...] = jnp.dot(h.astype(BF16), w2_ref[0, 0].astype(BF16), preferred_element_type=F32)


def _moe_grouped(layer, pl_, xg, w1, w3, w2):
    pmax = xg.shape[0]
    row = lambda i, te, nu: (jnp.minimum(i, nu[0] - 1), 0)
    wmap = lambda i, te, nu: (layer, te[i], 0, 0)
    grid_spec = pltpu.PrefetchScalarGridSpec(
        num_scalar_prefetch=2,
        grid=(pmax // TG,),
        in_specs=[
            pl.BlockSpec((TG, D_MODEL), row),
            pl.BlockSpec((1, 1, D_MODEL, D_EXPERT), wmap),
            pl.BlockSpec((1, 1, D_MODEL, D_EXPERT), wmap),
            pl.BlockSpec((1, 1, D_EXPERT, D_MODEL), wmap),
        ],
        out_specs=pl.BlockSpec((TG, D_MODEL), row),
    )
    return pl.pallas_call(
        _grouped_kernel,
        grid_spec=grid_spec,
        out_shape=jax.ShapeDtypeStruct((pmax, D_MODEL), F32),
        compiler_params=_cparams(("arbitrary",), VMEM_LIMIT),
        name="moe_grouped",
    )(pl_["te"], pl_["nused"], xg, w1, w3, w2)


def _combine_kernel(c_s, lo_s, go_s, rows_s, x_ref, gate_ref, p_ref, lorow_ref, crow_ref, yg_ref,
                    ws1_ref, ws3_ref, ws2_ref, g2_ref, b2_ref, o_ref, ys_sc, acc_sc, sem):
    t = pl.program_id(0)

    @pl.when(t == 0)
    def _():
        ys_sc[...] = jnp.zeros(ys_sc.shape, ys_sc.dtype)

    def seg_copy(src, dst, n):
        return pltpu.make_async_copy(yg_ref.at[pl.ds(pl.multiple_of(src, ROW_Q), n), :],
                                     ys_sc.at[pl.ds(pl.multiple_of(dst, ROW_Q), n), :], sem)

    def start(e, carry):
        _for_chunks(c_s[t, e], go_s[t, e], lo_s[t, e], lambda s, d, n: seg_copy(s, d, n).start())
        return carry

    lax.fori_loop(0, N_EXPERTS, start, 0)

    x = x_ref[...]
    xb = x.astype(BF16)
    s1 = jnp.dot(xb, ws1_ref[...], preferred_element_type=F32)
    s3 = jnp.dot(xb, ws3_ref[...], preferred_element_type=F32)
    shared = jnp.dot(((s1 * jax.nn.sigmoid(s1)) * s3).astype(BF16), ws2_ref[...], preferred_element_type=F32)
    g3 = jnp.concatenate(_split3(gate_ref[...]), axis=1)

    _for_chunks(rows_s[t], 0, 0, lambda s, d, n: seg_copy(s, d, n).wait(), ROWS_BITS)

    lo_row = lorow_ref[0]
    c_row = crow_ref[0]
    ck = 256

    def chunk(k0):
        p = p_ref[0, k0:k0 + ck, :]
        pg3 = jnp.dot(p, g3, preferred_element_type=F32)
        pg = (pg3[:, :LANES] + pg3[:, LANES:2 * LANES]) + pg3[:, 2 * LANES:]
        rowi = (lax.broadcasted_iota(jnp.int32, (ck, LANES), 0) + k0).astype(F32)
        mine = (rowi >= lo_row) & (rowi < lo_row + c_row)
        gs = jnp.sum(jnp.where(mine, pg, 0.0), axis=1, keepdims=True)
        inside = jnp.sum(jnp.where(mine, 1.0, 0.0), axis=1, keepdims=True) > 0.0
        y = jnp.where(inside, ys_sc[k0:k0 + ck, :] * gs, 0.0)
        yh = y.astype(BF16)
        yl = (y - yh.astype(F32)).astype(BF16)
        tn = (((0,), (0,)), ((), ()))
        both = lax.dot_general(p, jnp.concatenate([yh, yl], axis=1), tn, preferred_element_type=F32)
        return both[:, :D_MODEL] + both[:, D_MODEL:]

    always = TT * TOP_K
    acc = chunk(0)
    for k0 in range(ck, always, ck):
        acc = acc + chunk(k0)
    acc_sc[...] = acc
    for k0 in range(always, LCP, ck):
        @pl.when(k0 < rows_s[t])
        def _(k0=k0):
            acc_sc[...] += chunk(k0)
    o_ref[...] = _layer_norm(ALPHA * x + (acc_sc[...] + shared), g2_ref[...], b2_ref[...])


def _moe_combine(pl_, x1, gate, pmat, yg, ws1, ws3, ws2, g2, b2):
    nt = N_TILES
    rep = lambda shape: pl.BlockSpec(shape, lambda t, *_: (0,) * len(shape))
    grid_spec = pltpu.PrefetchScalarGridSpec(
        num_scalar_prefetch=4,
        grid=(nt,),
        in_specs=[
            pl.BlockSpec((TT, D_MODEL), lambda t, *_: (t, 0)),
            pl.BlockSpec((TT, LANES), lambda t, *_: (t, 0)),
            pl.BlockSpec((1, LCP, TT), lambda t, *_: (t, 0, 0)),
            pl.BlockSpec((1, 1, LANES), lambda t, *_: (t, 0, 0)),
            pl.BlockSpec((1, 1, LANES), lambda t, *_: (t, 0, 0)),
            pl.BlockSpec(memory_space=pl.ANY),
            rep((D_MODEL, D_EXPERT)), rep((D_MODEL, D_EXPERT)), rep((D_EXPERT, D_MODEL)),
            rep((1, D_MODEL)), rep((1, D_MODEL)),
        ],
        out_specs=pl.BlockSpec((TT, D_MODEL), lambda t, *_: (t, 0)),
        scratch_shapes=[
            pltpu.VMEM((LCP, D_MODEL), F32),
            pltpu.VMEM((TT, D_MODEL), F32),
            pltpu.SemaphoreType.DMA,
        ],
    )
    return pl.pallas_call(
        _combine_kernel,
        grid_spec=grid_spec,
        out_shape=jax.ShapeDtypeStruct((nt * TT, D_MODEL), F32),
        compiler_params=_cparams(("arbitrary",), VMEM_LIMIT),
        name="moe_combine",
    )(pl_["c"], pl_["lo"], pl_["go"], pl_["rows"], x1, gate, pmat, pl_["lo_row"], pl_["c_row"], yg,
      ws1, ws3, ws2, g2, b2)


def _block_diag(w):
    nb, k, j = w.shape
    eye = jnp.eye(nb, dtype=w.dtype)
    return jnp.einsum('nkj,nm->nkmj', w, eye).reshape(nb * k, nb * j)


def kernel(x_prompt, x_sample, cache_k, cache_v, cache_logf, state_h, state_conv, page_table, w_in, b_f, conv_w, conv_b, w_gr, b_gr, w_gi, b_gi, lam, g_att, g_rec, w_out, ln1_g, ln1_b, w_router, router_bias, w1, w3, w2, ws1, ws3, ws2, ln2_g, ln2_b):
    n_pool = cache_k.shape[1]
    x = jnp.concatenate([x_prompt.reshape(N_PROMPT, D_MODEL), x_sample.reshape(N_SAMPLE, D_MODEL)], axis=0)
    ckt = jnp.transpose(cache_k, (0, 1, 3, 4, 2)).reshape(DEPTH, n_pool, ATT_W, PAGE_SIZE)
    cvt = jnp.transpose(cache_v, (0, 1, 3, 4, 2)).reshape(DEPTH, n_pool, ATT_W, PAGE_SIZE)
    lf_t = jnp.swapaxes(cache_logf, -1, -2).reshape(DEPTH * n_pool * H_ATT, PAGE_SIZE)
    wt = _cumsum_lanes(lf_t, 4096).reshape(DEPTH, n_pool, H_ATT, PAGE_SIZE)
    zero_conv = jnp.zeros((BATCH, CONV_W - 1, REC_W), F32)
    zero_h = jnp.zeros((BATCH, 1, REC_W), F32)
    cf = 3 * ATT_W

    outs = {k: [] for k in ("kp", "vp", "fp", "hp", "cp", "ks", "vs", "fs", "hs", "cs")}
    for l in range(DEPTH):
        w_main = jnp.concatenate([w_in[l][:, :cf], w_in[l][:, cf + H_ATT:]], axis=1).astype(BF16)
        w_f = jnp.pad(w_in[l][:, cf:cf + H_ATT], ((0, 0), (0, LANES - H_ATT))).astype(BF16)
        bf_pad = jnp.pad(b_f[l], (0, LANES - H_ATT)).reshape(1, LANES)
        u = _matmul(x, w_main, TM_PROJ, TN_PROJ)
        flog = _forget_logits(x, w_f, bf_pad, TM_PROJ)[:, :H_ATT]
        logf_p = flog[:N_PROMPT].reshape(BATCH, SEQ, H_ATT)
        logf_s = flog[N_PROMPT:].reshape(DEC_BATCH, DEC_SEQ, H_ATT)

        ft = _cumsum_lanes(jnp.swapaxes(logf_p, 1, 2).reshape(BATCH * H_ATT, SEQ), BATCH * H_ATT)
        fr = ft.reshape(BATCH, H_ATT // 2, 2, SEQ)
        fc = jnp.swapaxes(fr, 2, 3)
        att_p = _attn_prompt(u, fc, fr)

        ls_t = jnp.pad(jnp.swapaxes(logf_s, 1, 2), ((0, 0), (0, 0), (0, LANES - DEC_SEQ)))
        fn = _cumsum_lanes(ls_t.reshape(DEC_BATCH * H_ATT, LANES), DEC_BATCH * H_ATT)
        fn = fn.reshape(DEC_BATCH, H_ATT, LANES)
        fq = fn[:, :, :DEC_SEQ].reshape(DEC_BATCH, H_ATT * DEC_SEQ, 1)
        att_s = _attn_sample(l, page_table, u, ckt, cvt, wt, fq, fn)

        wg = jnp.concatenate([_block_diag(w_gr[l]), _block_diag(w_gi[l])], axis=1).astype(BF16)
        bg = jnp.concatenate([b_gr[l], b_gi[l]]).reshape(1, 2 * REC_W)
        rec_args = (conv_w[l], conv_b[l].reshape(1, REC_W), wg, bg, lam[l].reshape(1, REC_W))
        rec_p, h_p, conv_p = _rglru(u, zero_conv, zero_h, *rec_args,
                                    nseq=BATCH, slen=SEQ, tc=TC_REC, row0=0)
        rec_s, h_s, conv_s = _rglru(u, state_conv[l], state_h[l].reshape(DEC_BATCH, 1, REC_W), *rec_args,
                                    nseq=DEC_BATCH, slen=DEC_SEQ, tc=DEC_SEQ, row0=N_PROMPT)

        att = jnp.concatenate([att_p, att_s], axis=0)
        rec = jnp.concatenate([rec_p, rec_s], axis=0)
        rw = jnp.pad(w_router[l], ((0, 0), (0, LANES - N_EXPERTS))).astype(BF16)
        rb = jnp.pad(router_bias[l], (0, LANES - N_EXPERTS)).reshape(1, LANES)
        x1, gate, cnt = _out_proj(att, rec, x, g_att[l].reshape(1, ATT_W), g_rec[l].reshape(1, REC_W),
                             w_out[l][:ATT_W].astype(BF16), w_out[l][ATT_W:].astype(BF16),
                             ln1_g[l].reshape(1, D_MODEL), ln1_b[l].reshape(1, D_MODEL),
                             rw, rb, TM_OUT)

        mp = _moe_plan(cnt[:, 0, :N_EXPERTS].astype(jnp.int32))
        xg, pmat = _moe_dispatch(mp, x1, gate)
        yg = _moe_grouped(l, mp, xg, w1, w3, w2)
        x = _moe_combine(mp, x1, gate, pmat, yg, ws1[l].astype(BF16), ws3[l].astype(BF16), ws2[l].astype(BF16),
                         ln2_g[l].reshape(1, D_MODEL), ln2_b[l].reshape(1, D_MODEL))

        kv = lambda rows, c0, shape: u[rows, c0:c0 + ATT_W].reshape(shape)
        p_rows, s_rows = slice(0, N_PROMPT), slice(N_PROMPT, N_TOK)
        p_shape = (BATCH, SEQ, H_ATT, HEAD_DIM)
        s_shape = (DEC_BATCH, DEC_SEQ, H_ATT, HEAD_DIM)
        outs["kp"].append(kv(p_rows, ATT_W, p_shape))
        outs["vp"].append(kv(p_rows, 2 * ATT_W, p_shape))
        outs["fp"].append(logf_p)
        outs["hp"].append(h_p.reshape(BATCH, REC_W))
        outs["cp"].append(conv_p)
        outs["ks"].append(kv(s_rows, ATT_W, s_shape))
        outs["vs"].append(kv(s_rows, 2 * ATT_W, s_shape))
        outs["fs"].append(logf_s)
        outs["hs"].append(h_s.reshape(DEC_BATCH, REC_W))
        outs["cs"].append(conv_s)

    st = lambda name: jnp.stack(outs[name])
    return (x[:N_PROMPT].reshape(BATCH, SEQ, D_MODEL), x[N_PROMPT:].reshape(DEC_BATCH, DEC_SEQ, D_MODEL),
            st("kp"), st("vp"), st("fp"), st("hp"), st("cp"),
            st("ks"), st("vs"), st("fs"), st("hs"), st("cs"))
```

```python
import functools

import jax
import jax.numpy as jnp
from jax import lax
from jax.experimental import pallas as pl
from jax.experimental.pallas import tpu as pltpu

F32 = jnp.float32
BF16 = jnp.bfloat16

D_MODEL = 1024
BATCH = 4
SEQ = 4096
DEPTH = 2
DEC_BATCH = 32
DEC_SEQ = 8
PAST_LEN = 8192
PAGE_SIZE = 128
ATT_W = 512
REC_W = 512
HEAD_DIM = 64
H_ATT = 8
N_REC_BLOCKS = 8
CONV_W = 4
RG_C = 8.0
N_EXPERTS = 64
TOP_K = 8
N_GROUPS = 8
TOPK_GROUPS = 4
D_EXPERT = 256
ROUTED_SCALE = 2.5
ALPHA = (2 * DEPTH) ** 0.25
EPS = 1e-5
ATT_SCALE = HEAD_DIM ** -0.5

N_PROMPT = BATCH * SEQ
N_SAMPLE = DEC_BATCH * DEC_SEQ
N_TOK = N_PROMPT + N_SAMPLE
N_PAGES = PAST_LEN // PAGE_SIZE
U_W = 3 * ATT_W + 2 * REC_W
LANES = 128
NEG = -1e30

TM_PROJ = 1280
TN_PROJ = 640
TM_OUT = 1280
TT = 256
ROW_Q = 16
SEG_MAX = TT
LC = TT * TOP_K + N_EXPERTS * ROW_Q
LCP = LC + SEG_MAX
TG = 512
CHUNK_BITS = (256, 128, 64, 32, 16)
ROWS_BITS = (2048, 1024, 512) + CHUNK_BITS
N_TILES = N_TOK // TT
P_ROWS = -(-(N_TOK * TOP_K + N_TILES * N_EXPERTS * (ROW_Q - 1) + N_EXPERTS * (TG - 1)) // TG) * TG
TQ = 512
TC_REC = 512
PAGES_PER_STEP = 16
VMEM_LIMIT = 56 * 1024 * 1024


def _cparams(sem, vmem=None):
    return pltpu.CompilerParams(dimension_semantics=sem, vmem_limit_bytes=vmem)


def _mm_kernel(x_ref, w_ref, o_ref):
    o_ref[...] = jnp.dot(x_ref[...].astype(BF16), w_ref[...], preferred_element_type=F32)


def _matmul(x, w, tm, tn):
    m, k = x.shape
    n = w.shape[1]
    return pl.pallas_call(
        _mm_kernel,
        grid=(m // tm, n // tn),
        in_specs=[pl.BlockSpec((tm, k), lambda i, j: (i, 0)),
                  pl.BlockSpec((k, tn), lambda i, j: (0, j))],
        out_specs=pl.BlockSpec((tm, tn), lambda i, j: (i, j)),
        out_shape=jax.ShapeDtypeStruct((m, n), F32),
        compiler_params=_cparams(("parallel", "arbitrary"), VMEM_LIMIT),
        name="in_proj",
    )(x, w)


def _forget_kernel(x_ref, w_ref, b_ref, o_ref):
    z = jnp.dot(x_ref[...].astype(BF16), w_ref[...], preferred_element_type=F32) + b_ref[...]
    o_ref[...] = jnp.minimum(z, 0.0) - jnp.log1p(jnp.exp(-jnp.abs(z)))


def _forget_logits(x, w, b, tm):
    m, k = x.shape
    n = w.shape[1]
    return pl.pallas_call(
        _forget_kernel,
        grid=(m // tm,),
        in_specs=[pl.BlockSpec((tm, k), lambda i: (i, 0)),
                  pl.BlockSpec((k, n), lambda i: (0, 0)),
                  pl.BlockSpec((1, n), lambda i: (0, 0))],
        out_specs=pl.BlockSpec((tm, n), lambda i: (i, 0)),
        out_shape=jax.ShapeDtypeStruct((m, n), F32),
        compiler_params=_cparams(("parallel",), VMEM_LIMIT),
        name="forget_logits",
    )(x, w, b)


def _split3(x):
    hi = x.astype(BF16)
    r1 = x - hi.astype(F32)
    mid = r1.astype(BF16)
    lo = (r1 - mid.astype(F32)).astype(BF16)
    return hi, mid, lo


def _cumsum_kernel(x_ref, o_ref, *, nchunk):
    rows = x_ref.shape[0]
    r = lax.broadcasted_iota(jnp.int32, (LANES, LANES), 0)
    c = lax.broadcasted_iota(jnp.int32, (LANES, LANES), 1)
    upper = jnp.where(r <= c, 1.0, 0.0).astype(BF16)
    carry = jnp.zeros((rows, 1), F32)
    for ch in range(nchunk):
        x = x_ref[:, ch * LANES:(ch + 1) * LANES]
        hi, mid, lo = _split3(x)
        w = (jnp.dot(hi, upper, preferred_element_type=F32)
             + jnp.dot(mid, upper, preferred_element_type=F32)
             + jnp.dot(lo, upper, preferred_element_type=F32)) + carry
        o_ref[:, ch * LANES:(ch + 1) * LANES] = w
        carry = w[:, LANES - 1:LANES]


def _cumsum_lanes(x, tr):
    rows, s = x.shape
    return pl.pallas_call(
        functools.partial(_cumsum_kernel, nchunk=s // LANES),
        grid=(rows // tr,),
        in_specs=[pl.BlockSpec((tr, s), lambda i: (i, 0))],
        out_specs=pl.BlockSpec((tr, s), lambda i: (i, 0)),
        out_shape=jax.ShapeDtypeStruct((rows, s), F32),
        compiler_params=_cparams(("parallel",)),
        name="cumsum_lanes",
    )(x)


def _attn_prompt_kernel(q_ref, k_ref, v_ref, fc_ref, fr_ref, o_ref, s_sc, kt_sc, vb_sc):
    qb = pl.program_id(2)
    nq = SEQ // TQ

    @pl.when(qb == 0)
    def _():
        for c in range(nq):
            kt_sc[c] = k_ref[c * TQ:(c + 1) * TQ, :].T.astype(BF16)
        vb_sc[...] = v_ref[...].astype(BF16)

    lane = lax.broadcasted_iota(jnp.int32, (TQ, LANES), 1)
    row = lax.broadcasted_iota(jnp.int32, (TQ, TQ), 0)
    col = lax.broadcasted_iota(jnp.int32, (TQ, TQ), 1)
    q = q_ref[...] * ATT_SCALE
    fc = fc_ref[0, 0]
    qm = [jnp.where(lane < HEAD_DIM, q, 0.0).astype(BF16), jnp.where(lane >= HEAD_DIM, q, 0.0).astype(BF16)]
    fq = [fc[:, 0:1], fc[:, 1:2]]

    def scores(kb, ms, masked):
        k0 = pl.multiple_of(kb * TQ, TQ)
        kt = kt_sc[kb]
        out = []
        for h in range(2):
            fk = fr_ref[0, 0, h:h + 1, pl.ds(k0, TQ)]
            s = jnp.dot(qm[h], kt, preferred_element_type=F32) + (fq[h] - fk)
            if masked:
                s = jnp.where(col <= row, s, NEG)
            s_sc[h, kb] = s
            out.append(jnp.maximum(ms[h], jnp.max(s, axis=1, keepdims=True)))
        return tuple(out)

    neg = jnp.full((TQ, 1), NEG, F32)
    ms = lax.fori_loop(0, qb, lambda kb, c: scores(kb, c, False), (neg, neg))
    ms = scores(qb, ms, True)

    def exps(kb, ls):
        out = []
        for h in range(2):
            e = jnp.exp(s_sc[h, kb] - ms[h])
            s_sc[h, kb] = e
            out.append(ls[h] + jnp.sum(e, axis=1, keepdims=True))
        return tuple(out)

    zero = jnp.zeros((TQ, 1), F32)
    ls = lax.fori_loop(0, qb + 1, exps, (zero, zero))
    rs = [1.0 / ls[0], 1.0 / ls[1]]

    def values(kb, accs):
        k0 = pl.multiple_of(kb * TQ, TQ)
        v = vb_sc[pl.ds(k0, TQ), :]
        return tuple(accs[h] + jnp.dot((s_sc[h, kb] * rs[h]).astype(BF16), v, preferred_element_type=F32)
                     for h in range(2))

    zacc = jnp.zeros((TQ, LANES), F32)
    outs = lax.fori_loop(0, qb + 1, values, (zacc, zacc))
    o_ref[...] = jnp.where(lane < HEAD_DIM, outs[0], outs[1])


def _attn_prompt(u, fc, fr):
    nq = SEQ // TQ
    kblk = ATT_W // LANES
    return pl.pallas_call(
        _attn_prompt_kernel,
        grid=(BATCH, H_ATT // 2, nq),
        in_specs=[
            pl.BlockSpec((TQ, LANES), lambda b, p, i: (b * nq + i, p)),
            pl.BlockSpec((SEQ, LANES), lambda b, p, i: (b, kblk + p)),
            pl.BlockSpec((SEQ, LANES), lambda b, p, i: (b, 2 * kblk + p)),
            pl.BlockSpec((1, 1, TQ, 2), lambda b, p, i: (b, p, i, 0)),
            pl.BlockSpec((1, 1, 2, SEQ), lambda b, p, i: (b, p, 0, 0)),
        ],
        out_specs=pl.BlockSpec((TQ, LANES), lambda b, p, i: (b * nq + i, p)),
        out_shape=jax.ShapeDtypeStruct((N_PROMPT, ATT_W), F32),
        scratch_shapes=[pltpu.VMEM((2, nq, TQ, TQ), F32), pltpu.VMEM((nq, LANES, TQ), BF16),
                        pltpu.VMEM((SEQ, LANES), BF16)],
        compiler_params=_cparams(("parallel", "parallel", "arbitrary"), VMEM_LIMIT),
        name="attn_prompt",
    )(u, u, u, fc, fr)


def _expand_heads(x):
    n = x.shape[1]
    return jnp.concatenate(
        [jnp.broadcast_to(x[h:h + 1, :], (DEC_SEQ, n)) for h in range(H_ATT)], axis=0)


def _attn_sample_kernel(pt_ref, q_ref, kn_ref, vn_ref, fq_ref, fn_ref, *rest):
    del pt_ref
    g = PAGES_PER_STEP
    nstep = N_PAGES // g
    k_refs, v_refs, w_refs = rest[:g], rest[g:2 * g], rest[2 * g:3 * g]
    o_ref = rest[3 * g]
    qbd_sc, s_sc, m_sc, r_sc, acc_sc, suf_sc = rest[3 * g + 1:]
    step = pl.program_id(1)
    nrow = H_ATT * DEC_SEQ
    row = lax.broadcasted_iota(jnp.int32, (nrow, ATT_W), 0)
    colw = lax.broadcasted_iota(jnp.int32, (nrow, ATT_W), 1)
    own_head = (colw >> 6) == (row >> 3)
    fq = fq_ref[0]
    pad = jnp.zeros((PAGE_SIZE - DEC_SEQ, ATT_W), F32)
    nt = (((1,), (1,)), ((), ()))

    @pl.when(step == 0)
    def _():
        q = q_ref[...] * ATT_SCALE
        qt = jnp.concatenate([q] * H_ATT, axis=0)
        qbd_sc[...] = jnp.where(own_head, qt, 0.0).astype(BF16)
        suf_sc[...] = jnp.zeros((H_ATT, 1), F32)
        kn = jnp.concatenate([kn_ref[...], pad], axis=0).astype(BF16)
        s = lax.dot_general(qbd_sc[...], kn, nt, preferred_element_type=F32)
        s = s + (fq - _expand_heads(fn_ref[0]))
        r = lax.broadcasted_iota(jnp.int32, (nrow, PAGE_SIZE), 0)
        c = lax.broadcasted_iota(jnp.int32, (nrow, PAGE_SIZE), 1)
        s = jnp.where(c <= (r & (DEC_SEQ - 1)), s, NEG)
        s_sc[N_PAGES] = s
        m_sc[...] = jnp.max(s, axis=1, keepdims=True)

    @pl.when(step < nstep)
    def _():
        first = (nstep - 1 - step) * g
        qbd = qbd_sc[...]
        suf = suf_sc[...]
        mrun = jnp.full((nrow, PAGE_SIZE), NEG, F32)
        for i in range(g - 1, -1, -1):
            w = w_refs[i][0, 0]
            suf = suf + w[:, PAGE_SIZE - 1:PAGE_SIZE]
            kt = k_refs[i][0, 0].astype(BF16)
            s = jnp.dot(qbd, kt, preferred_element_type=F32)
            s = s + (fq + _expand_heads(suf - w))
            s_sc[first + i] = s
            mrun = jnp.maximum(mrun, s)
        suf_sc[...] = suf
        m_sc[...] = jnp.maximum(m_sc[...], jnp.max(mrun, axis=1, keepdims=True))

    @pl.when(step == nstep - 1)
    def _():
        m = m_sc[...]

        def slot(p, lacc):
            ex = jnp.exp(s_sc[p] - m)
            s_sc[p] = ex
            return lacc + ex

        lacc = lax.fori_loop(0, N_PAGES + 1, slot, jnp.zeros((nrow, PAGE_SIZE), F32))
        r_sc[...] = 1.0 / jnp.sum(lacc, axis=1, keepdims=True)

    @pl.when(step >= nstep)
    def _():
        first = (2 * nstep - 1 - step) * g
        r = r_sc[...]

        @pl.when(step == nstep)
        def _():
            vn = jnp.concatenate([vn_ref[...], pad], axis=0).astype(BF16)
            acc_sc[...] = jnp.dot((s_sc[N_PAGES] * r).astype(BF16), vn, preferred_element_type=F32)

        acc = acc_sc[...]
        for i in range(g - 1, -1, -1):
            vt = v_refs[i][0, 0].astype(BF16)
            p = (s_sc[first + i] * r).astype(BF16)
            acc = acc + lax.dot_general(p, vt, nt, preferred_element_type=F32)
        acc_sc[...] = acc

    @pl.when(step == 2 * nstep - 1)
    def _():
        o = jnp.where(own_head, acc_sc[...], 0.0)
        out = o[0:DEC_SEQ, :]
        for h in range(1, H_ATT):
            out = out + o[h * DEC_SEQ:(h + 1) * DEC_SEQ, :]
        o_ref[...] = out


def _attn_sample(layer, page_table, u, cache_kt, cache_vt, wt, fq, fn):
    g = PAGES_PER_STEP
    nstep = N_PAGES // g
    srow = N_PROMPT // DEC_SEQ
    nrow = H_ATT * DEC_SEQ

    def key_map(i):
        return lambda b, s, pt: (layer, pt[b, (nstep - 1 - jnp.minimum(s, nstep - 1)) * g + i], 0, 0)

    def value_map(i):
        return lambda b, s, pt: (layer, pt[b, (nstep - 1 - jnp.maximum(s - nstep, 0)) * g + i], 0, 0)

    in_specs = [
        pl.BlockSpec((DEC_SEQ, ATT_W), lambda b, s, pt: (srow + b, 0)),
        pl.BlockSpec((DEC_SEQ, ATT_W), lambda b, s, pt: (srow + b, 1)),
        pl.BlockSpec((DEC_SEQ, ATT_W), lambda b, s, pt: (srow + b, 2)),
        pl.BlockSpec((1, nrow, 1), lambda b, s, pt: (b, 0, 0)),
        pl.BlockSpec((1, H_ATT, LANES), lambda b, s, pt: (b, 0, 0)),
    ]
    in_specs += [pl.BlockSpec((1, 1, ATT_W, PAGE_SIZE), key_map(i)) for i in range(g)]
    in_specs += [pl.BlockSpec((1, 1, ATT_W, PAGE_SIZE), value_map(i)) for i in range(g)]
    in_specs += [pl.BlockSpec((1, 1, H_ATT, PAGE_SIZE), key_map(i)) for i in range(g)]
    grid_spec = pltpu.PrefetchScalarGridSpec(
        num_scalar_prefetch=1,
        grid=(DEC_BATCH, 2 * nstep),
        in_specs=in_specs,
        out_specs=pl.BlockSpec((DEC_SEQ, ATT_W), lambda b, s, pt: (b, 0)),
        scratch_shapes=[
            pltpu.VMEM((nrow, ATT_W), BF16),
            pltpu.VMEM((N_PAGES + 1, nrow, PAGE_SIZE), F32),
            pltpu.VMEM((nrow, 1), F32),
            pltpu.VMEM((nrow, 1), F32),
            pltpu.VMEM((nrow, ATT_W), F32),
            pltpu.VMEM((H_ATT, 1), F32),
        ],
    )
    return pl.pallas_call(
        _attn_sample_kernel,
        grid_spec=grid_spec,
        out_shape=jax.ShapeDtypeStruct((N_SAMPLE, ATT_W), F32),
        compiler_params=_cparams(("parallel", "arbitrary"), VMEM_LIMIT),
        name="attn_sample",
    )(page_table, u, u, u, fq, fn, *([cache_kt] * g), *([cache_vt] * g), *([wt] * g))


def _rglru_kernel(xr_ref, gr_ref, cs_ref, h0_ref, cw_ref, cb_ref, wg_ref, bg_ref, lam_ref,
                  y_ref, hN_ref, cN_ref, xbuf, abuf, bbuf, hbuf, hcar, *, tc):
    c = pl.program_id(1)
    x = xr_ref[...]

    @pl.when(c == 0)
    def _():
        xbuf[0:8, :] = jnp.zeros((8, REC_W), F32)
        xbuf[8 - (CONV_W - 1):8, :] = cs_ref[0]
        hcar[...] = h0_ref[0]

    xbuf[8:8 + tc, :] = x
    xc = cb_ref[...] + xbuf[5:5 + tc, :] * cw_ref[0:1, :]
    xc = xc + xbuf[6:6 + tc, :] * cw_ref[1:2, :]
    xc = xc + xbuf[7:7 + tc, :] * cw_ref[2:3, :]
    xc = xc + x * cw_ref[3:4, :]
    xbuf[0:8, :] = xbuf[tc:tc + 8, :]

    gates = jnp.dot(xc.astype(BF16), wg_ref[...], preferred_element_type=F32) + bg_ref[...]
    rg = jax.nn.sigmoid(gates[:, :REC_W])
    ig = jax.nn.sigmoid(gates[:, REC_W:])
    nl = -lam_ref[...]
    softplus = jnp.maximum(nl, 0.0) + jnp.log1p(jnp.exp(-jnp.abs(nl)))
    log_a = (-RG_C * softplus) * rg
    a = jnp.exp(log_a)
    abuf[...] = a
    bbuf[...] = jnp.sqrt(-jnp.tanh(log_a) * (a * a + 1.0)) * (ig * xc)

    def group(gi, h):
        base = pl.multiple_of(gi * 8, 8)
        a8 = abuf[pl.ds(base, 8), :]
        b8 = bbuf[pl.ds(base, 8), :]
        rows = []
        for r in range(8):
            h = a8[r:r + 1, :] * h + b8[r:r + 1, :]
            rows.append(h)
        hbuf[pl.ds(base, 8), :] = jnp.concatenate(rows, axis=0)
        return h

    h_last = lax.fori_loop(0, tc // 8, group, hcar[...])
    hcar[...] = h_last
    y_ref[...] = hbuf[...] * jax.nn.gelu(gr_ref[...], approximate=True)

    @pl.when(c == pl.num_programs(1) - 1)
    def _():
        hN_ref[0] = h_last
        cN_ref[0] = x[tc - (CONV_W - 1):tc, :]


def _rglru(u, conv_state, h0, cw, cb, wg, bg, lam, *, nseq, slen, tc, row0):
    nchunk = slen // tc
    blk0 = row0 // tc
    xcol = (3 * ATT_W) // REC_W
    rep = lambda shape: pl.BlockSpec(shape, lambda b, c: (0,) * len(shape))
    return pl.pallas_call(
        functools.partial(_rglru_kernel, tc=tc),
        grid=(nseq, nchunk),
        in_specs=[
            pl.BlockSpec((tc, REC_W), lambda b, c: (blk0 + b * nchunk + c, xcol)),
            pl.BlockSpec((tc, REC_W), lambda b, c: (blk0 + b * nchunk + c, xcol + 1)),
            pl.BlockSpec((1, CONV_W - 1, REC_W), lambda b, c: (b, 0, 0)),
            pl.BlockSpec((1, 1, REC_W), lambda b, c: (b, 0, 0)),
            rep((CONV_W, REC_W)), rep((1, REC_W)), rep((REC_W, 2 * REC_W)),
            rep((1, 2 * REC_W)), rep((1, REC_W)),
        ],
        out_specs=[
            pl.BlockSpec((tc, REC_W), lambda b, c: (b * nchunk + c, 0)),
            pl.BlockSpec((1, 1, REC_W), lambda b, c: (b, 0, 0)),
            pl.BlockSpec((1, CONV_W - 1, REC_W), lambda b, c: (b, 0, 0)),
        ],
        out_shape=[
            jax.ShapeDtypeStruct((nseq * slen, REC_W), F32),
            jax.ShapeDtypeStruct((nseq, 1, REC_W), F32),
            jax.ShapeDtypeStruct((nseq, CONV_W - 1, REC_W), F32),
        ],
        scratch_shapes=[
            pltpu.VMEM((tc + 8, REC_W), F32),
            pltpu.VMEM((tc, REC_W), F32),
            pltpu.VMEM((tc, REC_W), F32),
            pltpu.VMEM((tc, REC_W), F32),
            pltpu.VMEM((1, REC_W), F32),
        ],
        compiler_params=_cparams(("parallel", "arbitrary"), VMEM_LIMIT),
        name="rglru",
    )(u, u, conv_state, h0, cw, cb, wg, bg, lam)


def _layer_norm(x, g, b):
    mu = jnp.mean(x, axis=-1, keepdims=True)
    xc = x - mu
    var = jnp.mean(xc * xc, axis=-1, keepdims=True)
    return xc * lax.rsqrt(var + EPS) * g + b


def _rms_norm(x, g):
    return x * lax.rsqrt(jnp.mean(x * x, axis=-1, keepdims=True) + EPS) * g


def _route(logits, rbias):
    shape = logits.shape
    ninf = -jnp.inf
    lane = lax.broadcasted_iota(jnp.int32, shape, 1)
    lane_f = lane.astype(F32)
    gsz = N_EXPERTS // N_GROUPS
    grp_f = (lane >> 3).astype(F32)
    valid = lane < N_EXPERTS
    scores = jax.nn.sigmoid(logits)
    biased = jnp.where(valid, scores + rbias, ninf)

    def first_max(vals, idx_f):
        m = jnp.max(vals, axis=1, keepdims=True)
        i = jnp.min(jnp.where(vals == m, idx_f, 1e9), axis=1, keepdims=True)
        return m, i

    gs = jnp.full(shape, ninf, F32)
    for g in range(N_GROUPS):
        in_g = (lane >= g * gsz) & (lane < (g + 1) * gsz)
        v = jnp.where(in_g, biased, ninf)
        m1, i1 = first_max(v, lane_f)
        m2 = jnp.max(jnp.where(lane_f == i1, ninf, v), axis=1, keepdims=True)
        gs = jnp.where(in_g, m1 + m2, gs)
    keep = jnp.zeros(shape, F32)
    for _ in range(TOPK_GROUPS):
        _, gi = first_max(gs, grp_f)
        pick = grp_f == gi
        keep = jnp.where(pick, 1.0, keep)
        gs = jnp.where(pick, ninf, gs)
    cand = jnp.where(keep > 0.0, biased, ninf)
    sel = jnp.zeros(shape, F32)
    for _ in range(TOP_K):
        _, ei = first_max(cand, lane_f)
        pick = lane_f == ei
        sel = jnp.where(pick, 1.0, sel)
        cand = jnp.where(pick, ninf, cand)
    picked = jnp.where(sel > 0.0, scores, 0.0)
    gate = picked / jnp.sum(picked, axis=1, keepdims=True) * ROUTED_SCALE
    return jnp.where(lane == N_EXPERTS, 1.0, gate)


def _out_kernel(att_ref, rec_ref, x_ref, ga_ref, gr_ref, wa_ref, wr_ref, g1_ref, b1_ref,
                rw_ref, rb_ref, x1_ref, gate_ref, cnt_ref):
    na = _rms_norm(att_ref[...], ga_ref[...]).astype(BF16)
    nr = _rms_norm(rec_ref[...], gr_ref[...]).astype(BF16)
    mix = (jnp.dot(na, wa_ref[...], preferred_element_type=F32)
           + jnp.dot(nr, wr_ref[...], preferred_element_type=F32))
    x1 = _layer_norm(ALPHA * x_ref[...] + mix, g1_ref[...], b1_ref[...])
    x1_ref[...] = x1
    logits = jnp.dot(x1.astype(BF16), rw_ref[...], preferred_element_type=F32)
    gate = _route(logits, rb_ref[...])
    gate_ref[...] = gate
    lane = lax.broadcasted_iota(jnp.int32, gate.shape, 1)
    routed = (gate > 0.0) & (lane < N_EXPERTS)
    hit = jnp.where(routed, 1.0, 0.0)
    for j in range(cnt_ref.shape[0]):
        cnt_ref[j] = jnp.sum(hit[j * TT:(j + 1) * TT, :], axis=0, keepdims=True)


def _out_proj(att, rec, x, ga, gr, wa, wr, g1, b1, rw, rb, tm):
    n = x.shape[0]
    rep = lambda shape: pl.BlockSpec(shape, lambda i: (0,) * len(shape))
    return pl.pallas_call(
        _out_kernel,
        grid=(n // tm,),
        in_specs=[
            pl.BlockSpec((tm, ATT_W), lambda i: (i, 0)),
            pl.BlockSpec((tm, REC_W), lambda i: (i, 0)),
            pl.BlockSpec((tm, D_MODEL), lambda i: (i, 0)),
            rep((1, ATT_W)), rep((1, REC_W)),
            rep((ATT_W, D_MODEL)), rep((REC_W, D_MODEL)),
            rep((1, D_MODEL)), rep((1, D_MODEL)),
            rep((D_MODEL, LANES)), rep((1, LANES)),
        ],
        out_specs=[pl.BlockSpec((tm, D_MODEL), lambda i: (i, 0)),
                   pl.BlockSpec((tm, LANES), lambda i: (i, 0)),
                   pl.BlockSpec((tm // TT, 1, LANES), lambda i: (i, 0, 0))],
        out_shape=[jax.ShapeDtypeStruct((n, D_MODEL), F32),
                   jax.ShapeDtypeStruct((n, LANES), F32),
                   jax.ShapeDtypeStruct((n // TT, 1, LANES), F32)],
        compiler_params=_cparams(("parallel",), VMEM_LIMIT),
        name="out_proj_router",
    )(att, rec, x, ga, gr, wa, wr, g1, b1, rw, rb)


def _moe_plan(cnt):
    c = (cnt + ROW_Q - 1) // ROW_Q * ROW_Q
    lo = jnp.cumsum(c, axis=1) - c
    tot = jnp.sum(c, axis=0)
    cap = (tot + TG - 1) // TG * TG
    ends = jnp.cumsum(cap)
    base = ends - cap
    go = base[None, :] + jnp.cumsum(c, axis=0) - c
    nused = (ends[-1] // TG).astype(jnp.int32).reshape(1)
    nt = N_TILES
    nt3 = P_ROWS // TG
    starts = jnp.arange(nt3, dtype=jnp.int32) * TG
    te = jnp.minimum(jnp.sum((ends[None, :] <= starts[:, None]).astype(jnp.int32), axis=1), N_EXPERTS - 1)
    rows = jnp.sum(c, axis=1).astype(jnp.int32)
    tail = (cap - tot).astype(jnp.int32)
    tdst = (base + tot).astype(jnp.int32)
    pad = lambda a: jnp.pad(a.astype(F32), ((0, 0), (0, LANES - N_EXPERTS)))
    lo_col = pad(lo).reshape(nt, LANES, 1)
    lo_row = pad(lo).reshape(nt, 1, LANES)
    c_row = pad(c).reshape(nt, 1, LANES)
    return dict(c=c.astype(jnp.int32), lo=lo.astype(jnp.int32), go=go.astype(jnp.int32), rows=rows, tail=tail,
                tdst=tdst, te=te, nused=nused, lo_col=lo_col, lo_row=lo_row, c_row=c_row)


def _tile_plan(t, gate, lo_col, lo_s, dest_sc, p_sc):
    gt = gate.T
    sub = lax.broadcasted_iota(jnp.int32, (LANES, TT), 0)
    sel = (gt > 0.0) & (sub < N_EXPERTS)
    r = lax.broadcasted_iota(jnp.int32, (TT, TT), 0)
    c = lax.broadcasted_iota(jnp.int32, (TT, TT), 1)
    before = jnp.where(r < c, 1.0, 0.0).astype(BF16)
    rank = jnp.dot(jnp.where(sel, 1.0, 0.0).astype(BF16), before, preferred_element_type=F32)
    dest_sc[...] = jnp.where(sel, lo_col + rank, -1.0)
    p_sc[...] = jnp.zeros(p_sc.shape, p_sc.dtype)
    rows = lax.broadcasted_iota(jnp.int32, (SEG_MAX, TT), 0).astype(F32)

    def window(e, carry):
        lo_e = pl.multiple_of(lo_s[t, e], ROW_Q)
        d = dest_sc[pl.ds(e, 1), :]
        p_sc[pl.ds(lo_e, SEG_MAX), :] = jnp.where(rows + lo_e.astype(F32) == d, 1.0, 0.0).astype(p_sc.dtype)
        return carry

    lax.fori_loop(0, N_EXPERTS, window, 0)


def _for_chunks(count, src, dst, fn, bits=CHUNK_BITS):
    small = 4 * ROW_Q

    def pieces(some_bits):
        for bit in some_bits:
            off = count & ~(2 * bit - 1)

            @pl.when((count & bit) != 0)
            def _(off=off, bit=bit):
                fn(src + off, dst + off, bit)

    @pl.when(count >= small)
    def _():
        pieces([b for b in bits if b >= small])

    pieces([b for b in bits if b < small])


def _dispatch_kernel(c_s, lo_s, go_s, rows_s, tail_s, tdst_s, x_ref, gate_ref, locol_ref, xg_ref, p_ref,
                     dest_sc, xs_sc, z_sc, sem, zsem):
    t = pl.program_id(0)
    last = t == pl.num_programs(0) - 1
    slot = t & 1

    def seg_copy(sl, src, dst, n):
        return pltpu.make_async_copy(xs_sc.at[sl, pl.ds(pl.multiple_of(src, ROW_Q), n), :],
                                     xg_ref.at[pl.ds(pl.multiple_of(dst, ROW_Q), n), :], sem.at[sl])

    def wait_tile(tile, sl):
        _for_chunks(rows_s[tile], 0, 0, lambda s, d, n: seg_copy(sl, s, d, n).wait(), ROWS_BITS)

    @pl.when(t >= 2)
    def _():
        wait_tile(t - 2, slot)

    p_sc = p_ref.at[0]
    _tile_plan(t, gate_ref[...], locol_ref[0], lo_s, dest_sc, p_sc)
    xs_sc[slot] = jnp.dot(p_sc[...], x_ref[...].astype(BF16), preferred_element_type=F32).astype(BF16)

    def start(e, carry):
        _for_chunks(c_s[t, e], lo_s[t, e], go_s[t, e], lambda s, d, n: seg_copy(slot, s, d, n).start())
        return carry

    lax.fori_loop(0, N_EXPERTS, start, 0)

    def zero_copy(src, dst, n):
        return pltpu.make_async_copy(z_sc.at[pl.ds(0, n), :],
                                     xg_ref.at[pl.ds(pl.multiple_of(dst, ROW_Q), n), :], zsem)

    @pl.when(last)
    def _():
        z_sc[...] = jnp.zeros(z_sc.shape, z_sc.dtype)

        def zstart(e, carry):
            _for_chunks(tail_s[e], 0, tdst_s[e], lambda s, d, n: zero_copy(s, d, n).start())
            return carry

        def zwait(e, carry):
            _for_chunks(tail_s[e], 0, tdst_s[e], lambda s, d, n: zero_copy(s, d, n).wait())
            return carry

        lax.fori_loop(0, N_EXPERTS, zstart, 0)

        @pl.when(t >= 1)
        def _():
            wait_tile(t - 1, 1 - slot)

        wait_tile(t, slot)
        lax.fori_loop(0, N_EXPERTS, zwait, 0)


def _moe_dispatch(pl_, x1, gate):
    nt, pmax = N_TILES, P_ROWS
    grid_spec = pltpu.PrefetchScalarGridSpec(
        num_scalar_prefetch=6,
        grid=(nt,),
        in_specs=[
            pl.BlockSpec((TT, D_MODEL), lambda t, *_: (t, 0)),
            pl.BlockSpec((TT, LANES), lambda t, *_: (t, 0)),
            pl.BlockSpec((1, LANES, 1), lambda t, *_: (t, 0, 0)),
        ],
        out_specs=[pl.BlockSpec(memory_space=pl.ANY),
                   pl.BlockSpec((1, LCP, TT), lambda t, *_: (t, 0, 0))],
        scratch_shapes=[
            pltpu.VMEM((LANES, TT), F32),
            pltpu.VMEM((2, LCP, D_MODEL), BF16),
            pltpu.VMEM((CHUNK_BITS[0], D_MODEL), BF16),
            pltpu.SemaphoreType.DMA((2,)),
            pltpu.SemaphoreType.DMA,
        ],
    )
    return pl.pallas_call(
        _dispatch_kernel,
        grid_spec=grid_spec,
        out_shape=[jax.ShapeDtypeStruct((pmax, D_MODEL), BF16),
                   jax.ShapeDtypeStruct((nt, LCP, TT), BF16)],
        compiler_params=_cparams(("arbitrary",), VMEM_LIMIT),
        name="moe_dispatch",
    )(pl_["c"], pl_["lo"], pl_["go"], pl_["rows"], pl_["tail"], pl_["tdst"], x1, gate, pl_["lo_col"])


def _grouped_kernel(te_s, nused_s, x_ref, w1_ref, w3_ref, w2_ref, y_ref):
    i = pl.program_id(0)

    @pl.when(i < nused_s[0])
    def _():
        x = x_ref[...]
        h1 = jnp.dot(x, w1_ref[0, 0].astype(BF16), preferred_element_type=F32)
        h3 = jnp.dot(x, w3_ref[0, 0].astype(BF16), preferred_element_type=F32)
        h = (h1 * jax.nn.sigmoid(h1)) * h3
        y_ref[...] = jnp.dot(h.astype(BF16), w2_ref[0, 0].astype(BF16), preferred_element_type=F32)


def _moe_grouped(layer, pl_, xg, w1, w3, w2):
    pmax = xg.shape[0]
    row = lambda i, te, nu: (jnp.minimum(i, nu[0] - 1), 0)
    wmap = lambda i, te, nu: (layer, te[i], 0, 0)
    grid_spec = pltpu.PrefetchScalarGridSpec(
        num_scalar_prefetch=2,
        grid=(pmax // TG,),
        in_specs=[
            pl.BlockSpec((TG, D_MODEL), row),
            pl.BlockSpec((1, 1, D_MODEL, D_EXPERT), wmap),
            pl.BlockSpec((1, 1, D_MODEL, D_EXPERT), wmap),
            pl.BlockSpec((1, 1, D_EXPERT, D_MODEL), wmap),
        ],
        out_specs=pl.BlockSpec((TG, D_MODEL), row),
    )
    return pl.pallas_call(
        _grouped_kernel,
        grid_spec=grid_spec,
        out_shape=jax.ShapeDtypeStruct((pmax, D_MODEL), F32),
        compiler_params=_cparams(("arbitrary",), VMEM_LIMIT),
        name="moe_grouped",
    )(pl_["te"], pl_["nused"], xg, w1, w3, w2)


def _combine_kernel(c_s, lo_s, go_s, rows_s, x_ref, gate_ref, p_ref, lorow_ref, crow_ref, yg_ref,
                    ws1_ref, ws3_ref, ws2_ref, g2_ref, b2_ref, o_ref, ys_sc, acc_sc, sem):
    t = pl.program_id(0)
    slot = t & 1

    def seg_copy(sl, src, dst, n):
        return pltpu.make_async_copy(yg_ref.at[pl.ds(pl.multiple_of(src, ROW_Q), n), :],
                                     ys_sc.at[sl, pl.ds(pl.multiple_of(dst, ROW_Q), n), :], sem.at[sl])

    def start_tile(tile, sl):
        def start(e, carry):
            _for_chunks(c_s[tile, e], go_s[tile, e], lo_s[tile, e], lambda s, d, n: seg_copy(sl, s, d, n).start())
            return carry

        lax.fori_loop(0, N_EXPERTS, start, 0)

    @pl.when(t == 0)
    def _():
        ys_sc[...] = jnp.zeros(ys_sc.shape, ys_sc.dtype)
        start_tile(0, 0)

    @pl.when(t + 1 < pl.num_programs(0))
    def _():
        start_tile(t + 1, 1 - slot)

    x = x_ref[...]
    xb = x.astype(BF16)
    s1 = jnp.dot(xb, ws1_ref[...], preferred_element_type=F32)
    s3 = jnp.dot(xb, ws3_ref[...], preferred_element_type=F32)
    shared = jnp.dot(((s1 * jax.nn.sigmoid(s1)) * s3).astype(BF16), ws2_ref[...], preferred_element_type=F32)
    g3 = jnp.concatenate(_split3(gate_ref[...]), axis=1)

    _for_chunks(rows_s[t], 0, 0, lambda s, d, n: seg_copy(slot, s, d, n).wait(), ROWS_BITS)

    lo_row = lorow_ref[0]
    c_row = crow_ref[0]
    ck = 256

    def chunk(k0):
        p = p_ref[0, k0:k0 + ck, :]
        pg3 = jnp.dot(p, g3, preferred_element_type=F32)
        pg = (pg3[:, :LANES] + pg3[:, LANES:2 * LANES]) + pg3[:, 2 * LANES:]
        rowi = (lax.broadcasted_iota(jnp.int32, (ck, LANES), 0) + k0).astype(F32)
        mine = (rowi >= lo_row) & (rowi < lo_row + c_row)
        gs = jnp.sum(jnp.where(mine, pg, 0.0), axis=1, keepdims=True)
        inside = jnp.sum(jnp.where(mine, 1.0, 0.0), axis=1, keepdims=True) > 0.0
        y = jnp.where(inside, ys_sc[slot, k0:k0 + ck, :] * gs, 0.0)
        yh = y.astype(BF16)
        yl = (y - yh.astype(F32)).astype(BF16)
        tn = (((0,), (0,)), ((), ()))
        both = lax.dot_general(p, jnp.concatenate([yh, yl], axis=1), tn, preferred_element_type=F32)
        return both[:, :D_MODEL] + both[:, D_MODEL:]

    always = TT * TOP_K
    acc = chunk(0)
    for k0 in range(ck, always, ck):
        acc = acc + chunk(k0)
    acc_sc[...] = acc
    for k0 in range(always, LCP, ck):
        @pl.when(k0 < rows_s[t])
        def _(k0=k0):
            acc_sc[...] += chunk(k0)
    o_ref[...] = _layer_norm(ALPHA * x + (acc_sc[...] + shared), g2_ref[...], b2_ref[...])


def _moe_combine(pl_, x1, gate, pmat, yg, ws1, ws3, ws2, g2, b2):
    nt = N_TILES
    rep = lambda shape: pl.BlockSpec(shape, lambda t, *_: (0,) * len(shape))
    grid_spec = pltpu.PrefetchScalarGridSpec(
        num_scalar_prefetch=4,
        grid=(nt,),
        in_specs=[
            pl.BlockSpec((TT, D_MODEL), lambda t, *_: (t, 0)),
            pl.BlockSpec((TT, LANES), lambda t, *_: (t, 0)),
            pl.BlockSpec((1, LCP, TT), lambda t, *_: (t, 0, 0)),
            pl.BlockSpec((1, 1, LANES), lambda t, *_: (t, 0, 0)),
            pl.BlockSpec((1, 1, LANES), lambda t, *_: (t, 0, 0)),
            pl.BlockSpec(memory_space=pl.ANY),
            rep((D_MODEL, D_EXPERT)), rep((D_MODEL, D_EXPERT)), rep((D_EXPERT, D_MODEL)),
            rep((1, D_MODEL)), rep((1, D_MODEL)),
        ],
        out_specs=pl.BlockSpec((TT, D_MODEL), lambda t, *_: (t, 0)),
        scratch_shapes=[
            pltpu.VMEM((2, LCP, D_MODEL), F32),
            pltpu.VMEM((TT, D_MODEL), F32),
            pltpu.SemaphoreType.DMA((2,)),
        ],
    )
    return pl.pallas_call(
        _combine_kernel,
        grid_spec=grid_spec,
        out_shape=jax.ShapeDtypeStruct((nt * TT, D_MODEL), F32),
        compiler_params=_cparams(("arbitrary",), VMEM_LIMIT),
        name="moe_combine",
    )(pl_["c"], pl_["lo"], pl_["go"], pl_["rows"], x1, gate, pmat, pl_["lo_row"], pl_["c_row"], yg,
      ws1, ws3, ws2, g2, b2)


def _block_diag(w):
    nb, k, j = w.shape
    eye = jnp.eye(nb, dtype=w.dtype)
    return jnp.einsum('nkj,nm->nkmj', w, eye).reshape(nb * k, nb * j)


def kernel(x_prompt, x_sample, cache_k, cache_v, cache_logf, state_h, state_conv, page_table, w_in, b_f, conv_w, conv_b, w_gr, b_gr, w_gi, b_gi, lam, g_att, g_rec, w_out, ln1_g, ln1_b, w_router, router_bias, w1, w3, w2, ws1, ws3, ws2, ln2_g, ln2_b):
    n_pool = cache_k.shape[1]
    x = jnp.concatenate([x_prompt.reshape(N_PROMPT, D_MODEL), x_sample.reshape(N_SAMPLE, D_MODEL)], axis=0)
    ckt = jnp.transpose(cache_k, (0, 1, 3, 4, 2)).reshape(DEPTH, n_pool, ATT_W, PAGE_SIZE)
    cvt = jnp.transpose(cache_v, (0, 1, 3, 4, 2)).reshape(DEPTH, n_pool, ATT_W, PAGE_SIZE)
    lf_t = jnp.swapaxes(cache_logf, -1, -2).reshape(DEPTH * n_pool * H_ATT, PAGE_SIZE)
    wt = _cumsum_lanes(lf_t, 4096).reshape(DEPTH, n_pool, H_ATT, PAGE_SIZE)
    zero_conv = jnp.zeros((BATCH, CONV_W - 1, REC_W), F32)
    zero_h = jnp.zeros((BATCH, 1, REC_W), F32)
    cf = 3 * ATT_W

    outs = {k: [] for k in ("kp", "vp", "fp", "hp", "cp", "ks", "vs", "fs", "hs", "cs")}
    for l in range(DEPTH):
        w_main = jnp.concatenate([w_in[l][:, :cf], w_in[l][:, cf + H_ATT:]], axis=1).astype(BF16)
        w_f = jnp.pad(w_in[l][:, cf:cf + H_ATT], ((0, 0), (0, LANES - H_ATT))).astype(BF16)
        bf_pad = jnp.pad(b_f[l], (0, LANES - H_ATT)).reshape(1, LANES)
        u = _matmul(x, w_main, TM_PROJ, TN_PROJ)
        flog = _forget_logits(x, w_f, bf_pad, TM_PROJ)[:, :H_ATT]
        logf_p = flog[:N_PROMPT].reshape(BATCH, SEQ, H_ATT)
        logf_s = flog[N_PROMPT:].reshape(DEC_BATCH, DEC_SEQ, H_ATT)

        ft = _cumsum_lanes(jnp.swapaxes(logf_p, 1, 2).reshape(BATCH * H_ATT, SEQ), BATCH * H_ATT)
        fr = ft.reshape(BATCH, H_ATT // 2, 2, SEQ)
        fc = jnp.swapaxes(fr, 2, 3)
        att_p = _attn_prompt(u, fc, fr)

        ls_t = jnp.pad(jnp.swapaxes(logf_s, 1, 2), ((0, 0), (0, 0), (0, LANES - DEC_SEQ)))
        fn = _cumsum_lanes(ls_t.reshape(DEC_BATCH * H_ATT, LANES), DEC_BATCH * H_ATT)
        fn = fn.reshape(DEC_BATCH, H_ATT, LANES)
        fq = fn[:, :, :DEC_SEQ].reshape(DEC_BATCH, H_ATT * DEC_SEQ, 1)
        att_s = _attn_sample(l, page_table, u, ckt, cvt, wt, fq, fn)

        wg = jnp.concatenate([_block_diag(w_gr[l]), _block_diag(w_gi[l])], axis=1).astype(BF16)
        bg = jnp.concatenate([b_gr[l], b_gi[l]]).reshape(1, 2 * REC_W)
        rec_args = (conv_w[l], conv_b[l].reshape(1, REC_W), wg, bg, lam[l].reshape(1, REC_W))
        rec_p, h_p, conv_p = _rglru(u, zero_conv, zero_h, *rec_args,
                                    nseq=BATCH, slen=SEQ, tc=TC_REC, row0=0)
        rec_s, h_s, conv_s = _rglru(u, state_conv[l], state_h[l].reshape(DEC_BATCH, 1, REC_W), *rec_args,
                                    nseq=DEC_BATCH, slen=DEC_SEQ, tc=DEC_SEQ, row0=N_PROMPT)

        att = jnp.concatenate([att_p, att_s], axis=0)
        rec = jnp.concatenate([rec_p, rec_s], axis=0)
        rw = jnp.pad(w_router[l], ((0, 0), (0, LANES - N_EXPERTS))).astype(BF16)
        rb = jnp.pad(router_bias[l], (0, LANES - N_EXPERTS)).reshape(1, LANES)
        x1, gate, cnt = _out_proj(att, rec, x, g_att[l].reshape(1, ATT_W), g_rec[l].reshape(1, REC_W),
                             w_out[l][:ATT_W].astype(BF16), w_out[l][ATT_W:].astype(BF16),
                             ln1_g[l].reshape(1, D_MODEL), ln1_b[l].reshape(1, D_MODEL),
                             rw, rb, TM_OUT)

        mp = _moe_plan(cnt[:, 0, :N_EXPERTS].astype(jnp.int32))
        xg, pmat = _moe_dispatch(mp, x1, gate)
        yg = _moe_grouped(l, mp, xg, w1, w3, w2)
        x = _moe_combine(mp, x1, gate, pmat, yg, ws1[l].astype(BF16), ws3[l].astype(BF16), ws2[l].astype(BF16),
                         ln2_g[l].reshape(1, D_MODEL), ln2_b[l].reshape(1, D_MODEL))

        kv = lambda rows, c0, shape: u[rows, c0:c0 + ATT_W].reshape(shape)
        p_rows, s_rows = slice(0, N_PROMPT), slice(N_PROMPT, N_TOK)
        p_shape = (BATCH, SEQ, H_ATT, HEAD_DIM)
        s_shape = (DEC_BATCH, DEC_SEQ, H_ATT, HEAD_DIM)
        outs["kp"].append(kv(p_rows, ATT_W, p_shape))
        outs["vp"].append(kv(p_rows, 2 * ATT_W, p_shape))
        outs["fp"].append(logf_p)
        outs["hp"].append(h_p.reshape(BATCH, REC_W))
        outs["cp"].append(conv_p)
        outs["ks"].append(kv(s_rows, ATT_W, s_shape))
        outs["vs"].append(kv(s_rows, 2 * ATT_W, s_shape))
        outs["fs"].append(logf_s)
        outs["hs"].append(h_s.reshape(DEC_BATCH, REC_W))
        outs["cs"].append(conv_s)

    st = lambda name: jnp.stack(outs[name])
    return (x[:N_PROMPT].reshape(BATCH, SEQ, D_MODEL), x[N_PROMPT:].reshape(DEC_BATCH, DEC_SEQ, D_MODEL),
            st("kp"), st("vp"), st("fp"), st("hp"), st("cp"),
            st("ks"), st("vs"), st("fs"), st("hs"), st("cs"))
```

```python
import functools

import jax
import jax.numpy as jnp
from jax import lax
from jax.experimental import pallas as pl
from jax.experimental.pallas import tpu as pltpu

F32 = jnp.float32
BF16 = jnp.bfloat16

D_MODEL = 1024
BATCH = 4
SEQ = 4096
DEPTH = 2
DEC_BATCH = 32
DEC_SEQ = 8
PAST_LEN = 8192
PAGE_SIZE = 128
ATT_W = 512
REC_W = 512
HEAD_DIM = 64
H_ATT = 8
N_REC_BLOCKS = 8
CONV_W = 4
RG_C = 8.0
N_EXPERTS = 64
TOP_K = 8
N_GROUPS = 8
TOPK_GROUPS = 4
D_EXPERT = 256
ROUTED_SCALE = 2.5
ALPHA = (2 * DEPTH) ** 0.25
EPS = 1e-5
ATT_SCALE = HEAD_DIM ** -0.5

N_PROMPT = BATCH * SEQ
N_SAMPLE = DEC_BATCH * DEC_SEQ
N_TOK = N_PROMPT + N_SAMPLE
N_PAGES = PAST_LEN // PAGE_SIZE
U_W = 3 * ATT_W + 2 * REC_W
LANES = 128
NEG = -1e30

TM_PROJ = 1280
TN_PROJ = 640
TM_OUT = 1280
TT = 256
ROW_Q = 16
SEG_MAX = TT
LC = TT * TOP_K + N_EXPERTS * ROW_Q
LCP = LC + SEG_MAX
TG = 512
CHUNK_BITS = (256, 128, 64, 32, 16)
ROWS_BITS = (2048, 1024, 512) + CHUNK_BITS
N_TILES = N_TOK // TT
P_ROWS = -(-(N_TOK * TOP_K + N_TILES * N_EXPERTS * (ROW_Q - 1) + N_EXPERTS * (TG - 1)) // TG) * TG
TQ = 512
TC_REC = 512
PAGES_PER_STEP = 16
VMEM_LIMIT = 56 * 1024 * 1024


def _cparams(sem, vmem=None):
    return pltpu.CompilerParams(dimension_semantics=sem, vmem_limit_bytes=vmem)


def _mm_kernel(x_ref, w_ref, o_ref):
    o_ref[...] = jnp.dot(x_ref[...].astype(BF16), w_ref[...], preferred_element_type=F32)


def _matmul(x, w, tm, tn):
    m, k = x.shape
    n = w.shape[1]
    return pl.pallas_call(
        _mm_kernel,
        grid=(m // tm, n // tn),
        in_specs=[pl.BlockSpec((tm, k), lambda i, j: (i, 0)),
                  pl.BlockSpec((k, tn), lambda i, j: (0, j))],
        out_specs=pl.BlockSpec((tm, tn), lambda i, j: (i, j)),
        out_shape=jax.ShapeDtypeStruct((m, n), F32),
        compiler_params=_cparams(("parallel", "arbitrary"), VMEM_LIMIT),
        name="in_proj",
    )(x, w)


def _forget_kernel(x_ref, w_ref, b_ref, o_ref):
    z = jnp.dot(x_ref[...].astype(BF16), w_ref[...], preferred_element_type=F32) + b_ref[...]
    o_ref[...] = jnp.minimum(z, 0.0) - jnp.log1p(jnp.exp(-jnp.abs(z)))


def _forget_logits(x, w, b, tm):
    m, k = x.shape
    n = w.shape[1]
    return pl.pallas_call(
        _forget_kernel,
        grid=(m // tm,),
        in_specs=[pl.BlockSpec((tm, k), lambda i: (i, 0)),
                  pl.BlockSpec((k, n), lambda i: (0, 0)),
                  pl.BlockSpec((1, n), lambda i: (0, 0))],
        out_specs=pl.BlockSpec((tm, n), lambda i: (i, 0)),
        out_shape=jax.ShapeDtypeStruct((m, n), F32),
        compiler_params=_cparams(("parallel",), VMEM_LIMIT),
        name="forget_logits",
    )(x, w, b)


def _split3(x):
    hi = x.astype(BF16)
    r1 = x - hi.astype(F32)
    mid = r1.astype(BF16)
    lo = (r1 - mid.astype(F32)).astype(BF16)
    return hi, mid, lo


def _cumsum_kernel(x_ref, o_ref, *, nchunk):
    rows = x_ref.shape[0]
    r = lax.broadcasted_iota(jnp.int32, (LANES, LANES), 0)
    c = lax.broadcasted_iota(jnp.int32, (LANES, LANES), 1)
    upper = jnp.where(r <= c, 1.0, 0.0).astype(BF16)
    carry = jnp.zeros((rows, 1), F32)
    for ch in range(nchunk):
        x = x_ref[:, ch * LANES:(ch + 1) * LANES]
        hi, mid, lo = _split3(x)
        w = (jnp.dot(hi, upper, preferred_element_type=F32)
             + jnp.dot(mid, upper, preferred_element_type=F32)
             + jnp.dot(lo, upper, preferred_element_type=F32)) + carry
        o_ref[:, ch * LANES:(ch + 1) * LANES] = w
        carry = w[:, LANES - 1:LANES]


def _cumsum_lanes(x, tr):
    rows, s = x.shape
    return pl.pallas_call(
        functools.partial(_cumsum_kernel, nchunk=s // LANES),
        grid=(rows // tr,),
        in_specs=[pl.BlockSpec((tr, s), lambda i: (i, 0))],
        out_specs=pl.BlockSpec((tr, s), lambda i: (i, 0)),
        out_shape=jax.ShapeDtypeStruct((rows, s), F32),
        compiler_params=_cparams(("parallel",)),
        name="cumsum_lanes",
    )(x)


def _attn_prompt_kernel(q_ref, k_ref, v_ref, fc_ref, fr_ref, o_ref, s_sc, kt_sc, vb_sc):
    qb = pl.program_id(2)
    nq = SEQ // TQ

    @pl.when(qb == 0)
    def _():
        for c in range(nq):
            kt_sc[c] = k_ref[c * TQ:(c + 1) * TQ, :].T.astype(BF16)
        vb_sc[...] = v_ref[...].astype(BF16)

    lane = lax.broadcasted_iota(jnp.int32, (TQ, LANES), 1)
    rowh = lax.broadcasted_iota(jnp.int32, (TQ, TQ // 2), 0)
    colh = lax.broadcasted_iota(jnp.int32, (TQ, TQ // 2), 1)
    q = q_ref[...] * ATT_SCALE
    fc = fc_ref[0, 0]
    qm = [jnp.where(lane < HEAD_DIM, q, 0.0).astype(BF16), jnp.where(lane >= HEAD_DIM, q, 0.0).astype(BF16)]
    fq = [fc[:, 0:1], fc[:, 1:2]]

    def scores(kb, ms, masked):
        k0 = pl.multiple_of(kb * TQ, TQ)
        kt = kt_sc[kb]
        out = []
        half = TQ // 2
        for h in range(2):
            m = ms[h]
            for c0 in (0, half):
                fk = fr_ref[0, 0, h:h + 1, pl.ds(k0 + c0, half)]
                s = jnp.dot(qm[h], kt[:, c0:c0 + half], preferred_element_type=F32) + (fq[h] - fk)
                if masked:
                    s = jnp.where(colh + c0 <= rowh, s, NEG)
                s_sc[h, kb, :, c0:c0 + half] = s
                m = jnp.maximum(m, jnp.max(s, axis=1, keepdims=True))
            out.append(m)
        return tuple(out)

    neg = jnp.full((TQ, 1), NEG, F32)
    ms = lax.fori_loop(0, qb, lambda kb, c: scores(kb, c, False), (neg, neg))
    ms = scores(qb, ms, True)

    def exps(kb, ls):
        out = []
        for h in range(2):
            e = jnp.exp(s_sc[h, kb] - ms[h])
            s_sc[h, kb] = e
            out.append(ls[h] + jnp.sum(e, axis=1, keepdims=True))
        return tuple(out)

    zero = jnp.zeros((TQ, 1), F32)
    ls = lax.fori_loop(0, qb + 1, exps, (zero, zero))
    rs = [1.0 / ls[0], 1.0 / ls[1]]

    def values(kb, accs):
        k0 = pl.multiple_of(kb * TQ, TQ)
        v = vb_sc[pl.ds(k0, TQ), :]
        return tuple(accs[h] + jnp.dot((s_sc[h, kb] * rs[h]).astype(BF16), v, preferred_element_type=F32)
                     for h in range(2))

    zacc = jnp.zeros((TQ, LANES), F32)
    outs = lax.fori_loop(0, qb + 1, values, (zacc, zacc))
    o_ref[...] = jnp.where(lane < HEAD_DIM, outs[0], outs[1])


def _attn_prompt(u, fc, fr):
    nq = SEQ // TQ
    kblk = ATT_W // LANES
    return pl.pallas_call(
        _attn_prompt_kernel,
        grid=(BATCH, H_ATT // 2, nq),
        in_specs=[
            pl.BlockSpec((TQ, LANES), lambda b, p, i: (b * nq + i, p)),
            pl.BlockSpec((SEQ, LANES), lambda b, p, i: (b, kblk + p)),
            pl.BlockSpec((SEQ, LANES), lambda b, p, i: (b, 2 * kblk + p)),
            pl.BlockSpec((1, 1, TQ, 2), lambda b, p, i: (b, p, i, 0)),
            pl.BlockSpec((1, 1, 2, SEQ), lambda b, p, i: (b, p, 0, 0)),
        ],
        out_specs=pl.BlockSpec((TQ, LANES), lambda b, p, i: (b * nq + i, p)),
        out_shape=jax.ShapeDtypeStruct((N_PROMPT, ATT_W), F32),
        scratch_shapes=[pltpu.VMEM((2, nq, TQ, TQ), F32), pltpu.VMEM((nq, LANES, TQ), BF16),
                        pltpu.VMEM((SEQ, LANES), BF16)],
        compiler_params=_cparams(("parallel", "parallel", "arbitrary"), VMEM_LIMIT),
        name="attn_prompt",
    )(u, u, u, fc, fr)


def _expand_heads(x):
    n = x.shape[1]
    return jnp.concatenate(
        [jnp.broadcast_to(x[h:h + 1, :], (DEC_SEQ, n)) for h in range(H_ATT)], axis=0)


def _attn_sample_kernel(pt_ref, q_ref, kn_ref, vn_ref, fq_ref, fn_ref, *rest):
    del pt_ref
    g = PAGES_PER_STEP
    nstep = N_PAGES // g
    k_refs, v_refs, w_refs = rest[:g], rest[g:2 * g], rest[2 * g:3 * g]
    o_ref = rest[3 * g]
    qbd_sc, s_sc, m_sc, r_sc, acc_sc, suf_sc = rest[3 * g + 1:]
    step = pl.program_id(1)
    nrow = H_ATT * DEC_SEQ
    row = lax.broadcasted_iota(jnp.int32, (nrow, ATT_W), 0)
    colw = lax.broadcasted_iota(jnp.int32, (nrow, ATT_W), 1)
    own_head = (colw >> 6) == (row >> 3)
    fq = fq_ref[0]
    pad = jnp.zeros((PAGE_SIZE - DEC_SEQ, ATT_W), F32)
    nt = (((1,), (1,)), ((), ()))

    @pl.when(step == 0)
    def _():
        q = q_ref[...] * ATT_SCALE
        qt = jnp.concatenate([q] * H_ATT, axis=0)
        qbd_sc[...] = jnp.where(own_head, qt, 0.0).astype(BF16)
        suf_sc[...] = jnp.zeros((H_ATT, 1), F32)
        kn = jnp.concatenate([kn_ref[...], pad], axis=0).astype(BF16)
        s = lax.dot_general(qbd_sc[...], kn, nt, preferred_element_type=F32)
        s = s + (fq - _expand_heads(fn_ref[0]))
        r = lax.broadcasted_iota(jnp.int32, (nrow, PAGE_SIZE), 0)
        c = lax.broadcasted_iota(jnp.int32, (nrow, PAGE_SIZE), 1)
        s = jnp.where(c <= (r & (DEC_SEQ - 1)), s, NEG)
        s_sc[N_PAGES] = s
        m_sc[...] = jnp.max(s, axis=1, keepdims=True)

    @pl.when(step < nstep)
    def _():
        first = (nstep - 1 - step) * g
        qbd = qbd_sc[...]
        suf = suf_sc[...]
        mrun = jnp.full((nrow, PAGE_SIZE), NEG, F32)
        for i in range(g - 1, -1, -1):
            w = w_refs[i][0, 0]
            suf = suf + w[:, PAGE_SIZE - 1:PAGE_SIZE]
            kt = k_refs[i][0, 0].astype(BF16)
            s = jnp.dot(qbd, kt, preferred_element_type=F32)
            s = s + (fq + _expand_heads(suf - w))
            s_sc[first + i] = s
            mrun = jnp.maximum(mrun, s)
        suf_sc[...] = suf
        m_sc[...] = jnp.maximum(m_sc[...], jnp.max(mrun, axis=1, keepdims=True))

    @pl.when(step == nstep - 1)
    def _():
        m = m_sc[...]

        def slot(p, lacc):
            ex = jnp.exp(s_sc[p] - m)
            s_sc[p] = ex
            return lacc + ex

        lacc = lax.fori_loop(0, N_PAGES + 1, slot, jnp.zeros((nrow, PAGE_SIZE), F32))
        r_sc[...] = 1.0 / jnp.sum(lacc, axis=1, keepdims=True)

    @pl.when(step >= nstep)
    def _():
        first = (2 * nstep - 1 - step) * g
        r = r_sc[...]

        @pl.when(step == nstep)
        def _():
            vn = jnp.concatenate([vn_ref[...], pad], axis=0).astype(BF16)
            acc_sc[...] = jnp.dot((s_sc[N_PAGES] * r).astype(BF16), vn, preferred_element_type=F32)

        acc = acc_sc[...]
        for i in range(g - 1, -1, -1):
            vt = v_refs[i][0, 0].astype(BF16)
            p = (s_sc[first + i] * r).astype(BF16)
            acc = acc + lax.dot_general(p, vt, nt, preferred_element_type=F32)
        acc_sc[...] = acc

    @pl.when(step == 2 * nstep - 1)
    def _():
        o = jnp.where(own_head, acc_sc[...], 0.0)
        out = o[0:DEC_SEQ, :]
        for h in range(1, H_ATT):
            out = out + o[h * DEC_SEQ:(h + 1) * DEC_SEQ, :]
        o_ref[...] = out


def _attn_sample(layer, page_table, u, cache_kt, cache_vt, wt, fq, fn):
    g = PAGES_PER_STEP
    nstep = N_PAGES // g
    srow = N_PROMPT // DEC_SEQ
    nrow = H_ATT * DEC_SEQ

    def key_map(i):
        return lambda b, s, pt: (layer, pt[b, (nstep - 1 - jnp.minimum(s, nstep - 1)) * g + i], 0, 0)

    def value_map(i):
        return lambda b, s, pt: (layer, pt[b, (nstep - 1 - jnp.maximum(s - nstep, 0)) * g + i], 0, 0)

    in_specs = [
        pl.BlockSpec((DEC_SEQ, ATT_W), lambda b, s, pt: (srow + b, 0)),
        pl.BlockSpec((DEC_SEQ, ATT_W), lambda b, s, pt: (srow + b, 1)),
        pl.BlockSpec((DEC_SEQ, ATT_W), lambda b, s, pt: (srow + b, 2)),
        pl.BlockSpec((1, nrow, 1), lambda b, s, pt: (b, 0, 0)),
        pl.BlockSpec((1, H_ATT, LANES), lambda b, s, pt: (b, 0, 0)),
    ]
    in_specs += [pl.BlockSpec((1, 1, ATT_W, PAGE_SIZE), key_map(i)) for i in range(g)]
    in_specs += [pl.BlockSpec((1, 1, ATT_W, PAGE_SIZE), value_map(i)) for i in range(g)]
    in_specs += [pl.BlockSpec((1, 1, H_ATT, PAGE_SIZE), key_map(i)) for i in range(g)]
    grid_spec = pltpu.PrefetchScalarGridSpec(
        num_scalar_prefetch=1,
        grid=(DEC_BATCH, 2 * nstep),
        in_specs=in_specs,
        out_specs=pl.BlockSpec((DEC_SEQ, ATT_W), lambda b, s, pt: (b, 0)),
        scratch_shapes=[
            pltpu.VMEM((nrow, ATT_W), BF16),
            pltpu.VMEM((N_PAGES + 1, nrow, PAGE_SIZE), F32),
            pltpu.VMEM((nrow, 1), F32),
            pltpu.VMEM((nrow, 1), F32),
            pltpu.VMEM((nrow, ATT_W), F32),
            pltpu.VMEM((H_ATT, 1), F32),
        ],
    )
    return pl.pallas_call(
        _attn_sample_kernel,
        grid_spec=grid_spec,
        out_shape=jax.ShapeDtypeStruct((N_SAMPLE, ATT_W), F32),
        compiler_params=_cparams(("parallel", "arbitrary"), VMEM_LIMIT),
        name="attn_sample",
    )(page_table, u, u, u, fq, fn, *([cache_kt] * g), *([cache_vt] * g), *([wt] * g))


def _rglru_kernel(xr_ref, gr_ref, cs_ref, h0_ref, cw_ref, cb_ref, wg_ref, bg_ref, lam_ref,
                  y_ref, hN_ref, cN_ref, xbuf, abuf, bbuf, hbuf, hcar, *, tc):
    c = pl.program_id(1)
    x = xr_ref[...]

    @pl.when(c == 0)
    def _():
        xbuf[0:8, :] = jnp.zeros((8, REC_W), F32)
        xbuf[8 - (CONV_W - 1):8, :] = cs_ref[0]
        hcar[...] = h0_ref[0]

    xbuf[8:8 + tc, :] = x
    xc = cb_ref[...] + xbuf[5:5 + tc, :] * cw_ref[0:1, :]
    xc = xc + xbuf[6:6 + tc, :] * cw_ref[1:2, :]
    xc = xc + xbuf[7:7 + tc, :] * cw_ref[2:3, :]
    xc = xc + x * cw_ref[3:4, :]
    xbuf[0:8, :] = xbuf[tc:tc + 8, :]

    gates = jnp.dot(xc.astype(BF16), wg_ref[...], preferred_element_type=F32) + bg_ref[...]
    rg = jax.nn.sigmoid(gates[:, :REC_W])
    ig = jax.nn.sigmoid(gates[:, REC_W:])
    nl = -lam_ref[...]
    softplus = jnp.maximum(nl, 0.0) + jnp.log1p(jnp.exp(-jnp.abs(nl)))
    log_a = (-RG_C * softplus) * rg
    a = jnp.exp(log_a)
    abuf[...] = a
    bbuf[...] = jnp.sqrt(-jnp.tanh(log_a) * (a * a + 1.0)) * (ig * xc)

    def group(gi, h):
        base = pl.multiple_of(gi * 8, 8)
        a8 = abuf[pl.ds(base, 8), :]
        b8 = bbuf[pl.ds(base, 8), :]
        rows = []
        for r in range(8):
            h = a8[r:r + 1, :] * h + b8[r:r + 1, :]
            rows.append(h)
        hbuf[pl.ds(base, 8), :] = jnp.concatenate(rows, axis=0)
        return h

    h_last = lax.fori_loop(0, tc // 8, group, hcar[...])
    hcar[...] = h_last
    y_ref[...] = hbuf[...] * jax.nn.gelu(gr_ref[...], approximate=True)

    @pl.when(c == pl.num_programs(1) - 1)
    def _():
        hN_ref[0] = h_last
        cN_ref[0] = x[tc - (CONV_W - 1):tc, :]


def _rglru(u, conv_state, h0, cw, cb, wg, bg, lam, *, nseq, slen, tc, row0):
    nchunk = slen // tc
    blk0 = row0 // tc
    xcol = (3 * ATT_W) // REC_W
    rep = lambda shape: pl.BlockSpec(shape, lambda b, c: (0,) * len(shape))
    return pl.pallas_call(
        functools.partial(_rglru_kernel, tc=tc),
        grid=(nseq, nchunk),
        in_specs=[
            pl.BlockSpec((tc, REC_W), lambda b, c: (blk0 + b * nchunk + c, xcol)),
            pl.BlockSpec((tc, REC_W), lambda b, c: (blk0 + b * nchunk + c, xcol + 1)),
            pl.BlockSpec((1, CONV_W - 1, REC_W), lambda b, c: (b, 0, 0)),
            pl.BlockSpec((1, 1, REC_W), lambda b, c: (b, 0, 0)),
            rep((CONV_W, REC_W)), rep((1, REC_W)), rep((REC_W, 2 * REC_W)),
            rep((1, 2 * REC_W)), rep((1, REC_W)),
        ],
        out_specs=[
            pl.BlockSpec((tc, REC_W), lambda b, c: (b * nchunk + c, 0)),
            pl.BlockSpec((1, 1, REC_W), lambda b, c: (b, 0, 0)),
            pl.BlockSpec((1, CONV_W - 1, REC_W), lambda b, c: (b, 0, 0)),
        ],
        out_shape=[
            jax.ShapeDtypeStruct((nseq * slen, REC_W), F32),
            jax.ShapeDtypeStruct((nseq, 1, REC_W), F32),
            jax.ShapeDtypeStruct((nseq, CONV_W - 1, REC_W), F32),
        ],
        scratch_shapes=[
            pltpu.VMEM((tc + 8, REC_W), F32),
            pltpu.VMEM((tc, REC_W), F32),
            pltpu.VMEM((tc, REC_W), F32),
            pltpu.VMEM((tc, REC_W), F32),
            pltpu.VMEM((1, REC_W), F32),
        ],
        compiler_params=_cparams(("parallel", "arbitrary"), VMEM_LIMIT),
        name="rglru",
    )(u, u, conv_state, h0, cw, cb, wg, bg, lam)


def _layer_norm(x, g, b):
    mu = jnp.mean(x, axis=-1, keepdims=True)
    xc = x - mu
    var = jnp.mean(xc * xc, axis=-1, keepdims=True)
    return xc * lax.rsqrt(var + EPS) * g + b


def _rms_norm(x, g):
    return x * lax.rsqrt(jnp.mean(x * x, axis=-1, keepdims=True) + EPS) * g


def _route(logits, rbias):
    shape = logits.shape
    ninf = -jnp.inf
    lane = lax.broadcasted_iota(jnp.int32, shape, 1)
    lane_f = lane.astype(F32)
    gsz = N_EXPERTS // N_GROUPS
    grp_f = (lane >> 3).astype(F32)
    valid = lane < N_EXPERTS
    scores = jax.nn.sigmoid(logits)
    biased = jnp.where(valid, scores + rbias, ninf)

    def first_max(vals, idx_f):
        m = jnp.max(vals, axis=1, keepdims=True)
        i = jnp.min(jnp.where(vals == m, idx_f, 1e9), axis=1, keepdims=True)
        return m, i

    gs = jnp.full(shape, ninf, F32)
    for g in range(N_GROUPS):
        in_g = (lane >= g * gsz) & (lane < (g + 1) * gsz)
        v = jnp.where(in_g, biased, ninf)
        m1, i1 = first_max(v, lane_f)
        m2 = jnp.max(jnp.where(lane_f == i1, ninf, v), axis=1, keepdims=True)
        gs = jnp.where(in_g, m1 + m2, gs)
    keep = jnp.zeros(shape, F32)
    for _ in range(TOPK_GROUPS):
        _, gi = first_max(gs, grp_f)
        pick = grp_f == gi
        keep = jnp.where(pick, 1.0, keep)
        gs = jnp.where(pick, ninf, gs)
    cand = jnp.where(keep > 0.0, biased, ninf)
    sel = jnp.zeros(shape, F32)
    for _ in range(TOP_K):
        _, ei = first_max(cand, lane_f)
        pick = lane_f == ei
        sel = jnp.where(pick, 1.0, sel)
        cand = jnp.where(pick, ninf, cand)
    picked = jnp.where(sel > 0.0, scores, 0.0)
    gate = picked / jnp.sum(picked, axis=1, keepdims=True) * ROUTED_SCALE
    return jnp.where(lane == N_EXPERTS, 1.0, gate)


def _out_kernel(att_ref, rec_ref, x_ref, ga_ref, gr_ref, wa_ref, wr_ref, g1_ref, b1_ref,
                rw_ref, rb_ref, x1_ref, gate_ref, cnt_ref):
    na = _rms_norm(att_ref[...], ga_ref[...]).astype(BF16)
    nr = _rms_norm(rec_ref[...], gr_ref[...]).astype(BF16)
    mix = (jnp.dot(na, wa_ref[...], preferred_element_type=F32)
           + jnp.dot(nr, wr_ref[...], preferred_element_type=F32))
    x1 = _layer_norm(ALPHA * x_ref[...] + mix, g1_ref[...], b1_ref[...])
    x1_ref[...] = x1
    logits = jnp.dot(x1.astype(BF16), rw_ref[...], preferred_element_type=F32)
    gate = _route(logits, rb_ref[...])
    gate_ref[...] = gate
    lane = lax.broadcasted_iota(jnp.int32, gate.shape, 1)
    routed = (gate > 0.0) & (lane < N_EXPERTS)
    hit = jnp.where(routed, 1.0, 0.0)
    for j in range(cnt_ref.shape[0]):
        cnt_ref[j] = jnp.sum(hit[j * TT:(j + 1) * TT, :], axis=0, keepdims=True)


def _out_proj(att, rec, x, ga, gr, wa, wr, g1, b1, rw, rb, tm):
    n = x.shape[0]
    rep = lambda shape: pl.BlockSpec(shape, lambda i: (0,) * len(shape))
    return pl.pallas_call(
        _out_kernel,
        grid=(n // tm,),
        in_specs=[
            pl.BlockSpec((tm, ATT_W), lambda i: (i, 0)),
            pl.BlockSpec((tm, REC_W), lambda i: (i, 0)),
            pl.BlockSpec((tm, D_MODEL), lambda i: (i, 0)),
            rep((1, ATT_W)), rep((1, REC_W)),
            rep((ATT_W, D_MODEL)), rep((REC_W, D_MODEL)),
            rep((1, D_MODEL)), rep((1, D_MODEL)),
            rep((D_MODEL, LANES)), rep((1, LANES)),
        ],
        out_specs=[pl.BlockSpec((tm, D_MODEL), lambda i: (i, 0)),
                   pl.BlockSpec((tm, LANES), lambda i: (i, 0)),
                   pl.BlockSpec((tm // TT, 1, LANES), lambda i: (i, 0, 0))],
        out_shape=[jax.ShapeDtypeStruct((n, D_MODEL), F32),
                   jax.ShapeDtypeStruct((n, LANES), F32),
                   jax.ShapeDtypeStruct((n // TT, 1, LANES), F32)],
        compiler_params=_cparams(("parallel",), VMEM_LIMIT),
        name="out_proj_router",
    )(att, rec, x, ga, gr, wa, wr, g1, b1, rw, rb)


def _moe_plan(cnt):
    c = (cnt + ROW_Q - 1) // ROW_Q * ROW_Q
    lo = jnp.cumsum(c, axis=1) - c
    tot = jnp.sum(c, axis=0)
    cap = (tot + TG - 1) // TG * TG
    ends = jnp.cumsum(cap)
    base = ends - cap
    go = base[None, :] + jnp.cumsum(c, axis=0) - c
    nused = (ends[-1] // TG).astype(jnp.int32).reshape(1)
    nt = N_TILES
    nt3 = P_ROWS // TG
    starts = jnp.arange(nt3, dtype=jnp.int32) * TG
    te = jnp.minimum(jnp.sum((ends[None, :] <= starts[:, None]).astype(jnp.int32), axis=1), N_EXPERTS - 1)
    rows = jnp.sum(c, axis=1).astype(jnp.int32)
    tail = (cap - tot).astype(jnp.int32)
    tdst = (base + tot).astype(jnp.int32)
    pad = lambda a: jnp.pad(a.astype(F32), ((0, 0), (0, LANES - N_EXPERTS)))
    lo_col = pad(lo).reshape(nt, LANES, 1)
    lo_row = pad(lo).reshape(nt, 1, LANES)
    c_row = pad(c).reshape(nt, 1, LANES)
    return dict(c=c.astype(jnp.int32), lo=lo.astype(jnp.int32), go=go.astype(jnp.int32), rows=rows, tail=tail,
                tdst=tdst, te=te, nused=nused, lo_col=lo_col, lo_row=lo_row, c_row=c_row)


def _tile_plan(t, gate, lo_col, lo_s, c_s, dest_sc, p_sc):
    gt = gate.T
    sub = lax.broadcasted_iota(jnp.int32, (LANES, TT), 0)
    sel = (gt > 0.0) & (sub < N_EXPERTS)
    r = lax.broadcasted_iota(jnp.int32, (TT, TT), 0)
    c = lax.broadcasted_iota(jnp.int32, (TT, TT), 1)
    before = jnp.where(r < c, 1.0, 0.0).astype(BF16)
    rank = jnp.dot(jnp.where(sel, 1.0, 0.0).astype(BF16), before, preferred_element_type=F32)
    dest_sc[...] = jnp.where(sel, lo_col + rank, -1.0)
    p_sc[...] = jnp.zeros(p_sc.shape, p_sc.dtype)
    short = SEG_MAX // 4

    def window(e, carry):
        lo_e = pl.multiple_of(lo_s[t, e], ROW_Q)
        d = dest_sc[pl.ds(e, 1), :]
        fits = c_s[t, e] <= short

        def put(n):
            rows = lax.broadcasted_iota(jnp.int32, (n, TT), 0).astype(F32)
            p_sc[pl.ds(lo_e, n), :] = jnp.where(rows + lo_e.astype(F32) == d, 1.0, 0.0).astype(p_sc.dtype)

        pl.when(fits)(lambda: put(short))
        pl.when(jnp.logical_not(fits))(lambda: put(SEG_MAX))
        return carry

    lax.fori_loop(0, N_EXPERTS, window, 0)


def _for_chunks(count, src, dst, fn, bits=CHUNK_BITS):
    small = 4 * ROW_Q

    def pieces(some_bits):
        for bit in some_bits:
            off = count & ~(2 * bit - 1)

            @pl.when((count & bit) != 0)
            def _(off=off, bit=bit):
                fn(src + off, dst + off, bit)

    @pl.when(count >= small)
    def _():
        pieces([b for b in bits if b >= small])

    pieces([b for b in bits if b < small])


def _dispatch_kernel(c_s, lo_s, go_s, rows_s, tail_s, tdst_s, x_ref, gate_ref, locol_ref, xg_ref, p_ref,
                     dest_sc, xs_sc, z_sc, sem, zsem):
    t = pl.program_id(0)
    last = t == pl.num_programs(0) - 1
    slot = t & 1

    def seg_copy(sl, src, dst, n):
        return pltpu.make_async_copy(xs_sc.at[sl, pl.ds(pl.multiple_of(src, ROW_Q), n), :],
                                     xg_ref.at[pl.ds(pl.multiple_of(dst, ROW_Q), n), :], sem.at[sl])

    def wait_tile(tile, sl):
        _for_chunks(rows_s[tile], 0, 0, lambda s, d, n: seg_copy(sl, s, d, n).wait(), ROWS_BITS)

    @pl.when(t >= 2)
    def _():
        wait_tile(t - 2, slot)

    p_sc = p_ref.at[0]
    _tile_plan(t, gate_ref[...], locol_ref[0], lo_s, c_s, dest_sc, p_sc)
    xs_sc[slot] = jnp.dot(p_sc[...], x_ref[...].astype(BF16), preferred_element_type=F32).astype(BF16)

    def start(e, carry):
        _for_chunks(c_s[t, e], lo_s[t, e], go_s[t, e], lambda s, d, n: seg_copy(slot, s, d, n).start())
        return carry

    lax.fori_loop(0, N_EXPERTS, start, 0)

    def zero_copy(src, dst, n):
        return pltpu.make_async_copy(z_sc.at[pl.ds(0, n), :],
                                     xg_ref.at[pl.ds(pl.multiple_of(dst, ROW_Q), n), :], zsem)

    @pl.when(last)
    def _():
        z_sc[...] = jnp.zeros(z_sc.shape, z_sc.dtype)

        def zstart(e, carry):
            _for_chunks(tail_s[e], 0, tdst_s[e], lambda s, d, n: zero_copy(s, d, n).start())
            return carry

        def zwait(e, carry):
            _for_chunks(tail_s[e], 0, tdst_s[e], lambda s, d, n: zero_copy(s, d, n).wait())
            return carry

        lax.fori_loop(0, N_EXPERTS, zstart, 0)

        @pl.when(t >= 1)
        def _():
            wait_tile(t - 1, 1 - slot)

        wait_tile(t, slot)
        lax.fori_loop(0, N_EXPERTS, zwait, 0)


def _moe_dispatch(pl_, x1, gate):
    nt, pmax = N_TILES, P_ROWS
    grid_spec = pltpu.PrefetchScalarGridSpec(
        num_scalar_prefetch=6,
        grid=(nt,),
        in_specs=[
            pl.BlockSpec((TT, D_MODEL), lambda t, *_: (t, 0)),
            pl.BlockSpec((TT, LANES), lambda t, *_: (t, 0)),
            pl.BlockSpec((1, LANES, 1), lambda t, *_: (t, 0, 0)),
        ],
        out_specs=[pl.BlockSpec(memory_space=pl.ANY),
                   pl.BlockSpec((1, LCP, TT), lambda t, *_: (t, 0, 0))],
        scratch_shapes=[
            pltpu.VMEM((LANES, TT), F32),
            pltpu.VMEM((2, LCP, D_MODEL), BF16),
            pltpu.VMEM((CHUNK_BITS[0], D_MODEL), BF16),
            pltpu.SemaphoreType.DMA((2,)),
            pltpu.SemaphoreType.DMA,
        ],
    )
    return pl.pallas_call(
        _dispatch_kernel,
        grid_spec=grid_spec,
        out_shape=[jax.ShapeDtypeStruct((pmax, D_MODEL), BF16),
                   jax.ShapeDtypeStruct((nt, LCP, TT), BF16)],
        compiler_params=_cparams(("arbitrary",), VMEM_LIMIT),
        name="moe_dispatch",
    )(pl_["c"], pl_["lo"], pl_["go"], pl_["rows"], pl_["tail"], pl_["tdst"], x1, gate, pl_["lo_col"])


def _grouped_kernel(te_s, nused_s, x_ref, w1_ref, w3_ref, w2_ref, y_ref):
    i = pl.program_id(0)

    @pl.when(i < nused_s[0])
    def _():
        x = x_ref[...]
        h1 = jnp.dot(x, w1_ref[0, 0].astype(BF16), preferred_element_type=F32)
        h3 = jnp.dot(x, w3_ref[0, 0].astype(BF16), preferred_element_type=F32)
        h = (h1 * jax.nn.sigmoid(h1)) * h3
        y_ref[...] = jnp.dot(h.astype(BF16), w2_ref[0, 0].astype(BF16), preferred_element_type=F32)


def _moe_grouped(layer, pl_, xg, w1, w3, w2):
    pmax = xg.shape[0]
    row = lambda i, te, nu: (jnp.minimum(i, nu[0] - 1), 0)
    wmap = lambda i, te, nu: (layer, te[i], 0, 0)
    grid_spec = pltpu.PrefetchScalarGridSpec(
        num_scalar_prefetch=2,
        grid=(pmax // TG,),
        in_specs=[
            pl.BlockSpec((TG, D_MODEL), row),
            pl.BlockSpec((1, 1, D_MODEL, D_EXPERT), wmap),
            pl.BlockSpec((1, 1, D_MODEL, D_EXPERT), wmap),
            pl.BlockSpec((1, 1, D_EXPERT, D_MODEL), wmap),
        ],
        out_specs=pl.BlockSpec((TG, D_MODEL), row),
    )
    return pl.pallas_call(
        _grouped_kernel,
        grid_spec=grid_spec,
        out_shape=jax.ShapeDtypeStruct((pmax, D_MODEL), F32),
        compiler_params=_cparams(("arbitrary",), VMEM_LIMIT),
        name="moe_grouped",
    )(pl_["te"], pl_["nused"], xg, w1, w3, w2)


def _combine_kernel(c_s, lo_s, go_s, rows_s, x_ref, gate_ref, p_ref, lorow_ref, crow_ref, yg_ref,
                    ws1_ref, ws3_ref, ws2_ref, g2_ref, b2_ref, o_ref, ys_sc, acc_sc, sem):
    t = pl.program_id(0)
    slot = t & 1

    def seg_copy(sl, src, dst, n):
        return pltpu.make_async_copy(yg_ref.at[pl.ds(pl.multiple_of(src, ROW_Q), n), :],
                                     ys_sc.at[sl, pl.ds(pl.multiple_of(dst, ROW_Q), n), :], sem.at[sl])

    def start_tile(tile, sl):
        def start(e, carry):
            _for_chunks(c_s[tile, e], go_s[tile, e], lo_s[tile, e], lambda s, d, n: seg_copy(sl, s, d, n).start())
            return carry

        lax.fori_loop(0, N_EXPERTS, start, 0)

    @pl.when(t == 0)
    def _():
        ys_sc[...] = jnp.zeros(ys_sc.shape, ys_sc.dtype)
        start_tile(0, 0)

    @pl.when(t + 1 < pl.num_programs(0))
    def _():
        start_tile(t + 1, 1 - slot)

    x = x_ref[...]
    xb = x.astype(BF16)
    s1 = jnp.dot(xb, ws1_ref[...], preferred_element_type=F32)
    s3 = jnp.dot(xb, ws3_ref[...], preferred_element_type=F32)
    shared = jnp.dot(((s1 * jax.nn.sigmoid(s1)) * s3).astype(BF16), ws2_ref[...], preferred_element_type=F32)
    g3 = jnp.concatenate(_split3(gate_ref[...]), axis=1)

    _for_chunks(rows_s[t], 0, 0, lambda s, d, n: seg_copy(slot, s, d, n).wait(), ROWS_BITS)

    lo_row = lorow_ref[0]
    c_row = crow_ref[0]
    ck = 256

    def chunk(k0):
        p = p_ref[0, k0:k0 + ck, :]
        pg3 = jnp.dot(p, g3, preferred_element_type=F32)
        pg = (pg3[:, :LANES] + pg3[:, LANES:2 * LANES]) + pg3[:, 2 * LANES:]
        rowi = (lax.broadcasted_iota(jnp.int32, (ck, LANES), 0) + k0).astype(F32)
        mine = (rowi >= lo_row) & (rowi < lo_row + c_row)
        gs = jnp.sum(jnp.where(mine, pg, 0.0), axis=1, keepdims=True)
        inside = jnp.sum(jnp.where(mine, 1.0, 0.0), axis=1, keepdims=True) > 0.0
        y = jnp.where(inside, ys_sc[slot, k0:k0 + ck, :] * gs, 0.0)
        yh = y.astype(BF16)
        yl = (y - yh.astype(F32)).astype(BF16)
        tn = (((0,), (0,)), ((), ()))
        both = lax.dot_general(p, jnp.concatenate([yh, yl], axis=1), tn, preferred_element_type=F32)
        return both[:, :D_MODEL] + both[:, D_MODEL:]

    always = TT * TOP_K
    acc = chunk(0)
    for k0 in range(ck, always, ck):
        acc = acc + chunk(k0)
    acc_sc[...] = acc
    for k0 in range(always, LCP, ck):
        @pl.when(k0 < rows_s[t])
        def _(k0=k0):
            acc_sc[...] += chunk(k0)
    o_ref[...] = _layer_norm(ALPHA * x + (acc_sc[...] + shared), g2_ref[...], b2_ref[...])


def _moe_combine(pl_, x1, gate, pmat, yg, ws1, ws3, ws2, g2, b2):
    nt = N_TILES
    rep = lambda shape: pl.BlockSpec(shape, lambda t, *_: (0,) * len(shape))
    grid_spec = pltpu.PrefetchScalarGridSpec(
        num_scalar_prefetch=4,
        grid=(nt,),
        in_specs=[
            pl.BlockSpec((TT, D_MODEL), lambda t, *_: (t, 0)),
            pl.BlockSpec((TT, LANES), lambda t, *_: (t, 0)),
            pl.BlockSpec((1, LCP, TT), lambda t, *_: (t, 0, 0)),
            pl.BlockSpec((1, 1, LANES), lambda t, *_: (t, 0, 0)),
            pl.BlockSpec((1, 1, LANES), lambda t, *_: (t, 0, 0)),
            pl.BlockSpec(memory_space=pl.ANY),
            rep((D_MODEL, D_EXPERT)), rep((D_MODEL, D_EXPERT)), rep((D_EXPERT, D_MODEL)),
            rep((1, D_MODEL)), rep((1, D_MODEL)),
        ],
        out_specs=pl.BlockSpec((TT, D_MODEL), lambda t, *_: (t, 0)),
        scratch_shapes=[
            pltpu.VMEM((2, LCP, D_MODEL), F32),
            pltpu.VMEM((TT, D_MODEL), F32),
            pltpu.SemaphoreType.DMA((2,)),
        ],
    )
    return pl.pallas_call(
        _combine_kernel,
        grid_spec=grid_spec,
        out_shape=jax.ShapeDtypeStruct((nt * TT, D_MODEL), F32),
        compiler_params=_cparams(("arbitrary",), VMEM_LIMIT),
        name="moe_combine",
    )(pl_["c"], pl_["lo"], pl_["go"], pl_["rows"], x1, gate, pmat, pl_["lo_row"], pl_["c_row"], yg,
      ws1, ws3, ws2, g2, b2)


def _block_diag(w):
    nb, k, j = w.shape
    eye = jnp.eye(nb, dtype=w.dtype)
    return jnp.einsum('nkj,nm->nkmj', w, eye).reshape(nb * k, nb * j)


def kernel(x_prompt, x_sample, cache_k, cache_v, cache_logf, state_h, state_conv, page_table, w_in, b_f, conv_w, conv_b, w_gr, b_gr, w_gi, b_gi, lam, g_att, g_rec, w_out, ln1_g, ln1_b, w_router, router_bias, w1, w3, w2, ws1, ws3, ws2, ln2_g, ln2_b):
    n_pool = cache_k.shape[1]
    x = jnp.concatenate([x_prompt.reshape(N_PROMPT, D_MODEL), x_sample.reshape(N_SAMPLE, D_MODEL)], axis=0)
    ckt = jnp.transpose(cache_k, (0, 1, 3, 4, 2)).reshape(DEPTH, n_pool, ATT_W, PAGE_SIZE)
    cvt = jnp.transpose(cache_v, (0, 1, 3, 4, 2)).reshape(DEPTH, n_pool, ATT_W, PAGE_SIZE)
    lf_t = jnp.swapaxes(cache_logf, -1, -2).reshape(DEPTH * n_pool * H_ATT, PAGE_SIZE)
    wt = _cumsum_lanes(lf_t, 4096).reshape(DEPTH, n_pool, H_ATT, PAGE_SIZE)
    zero_conv = jnp.zeros((BATCH, CONV_W - 1, REC_W), F32)
    zero_h = jnp.zeros((BATCH, 1, REC_W), F32)
    cf = 3 * ATT_W

    outs = {k: [] for k in ("kp", "vp", "fp", "hp", "cp", "ks", "vs", "fs", "hs", "cs")}
    for l in range(DEPTH):
        w_main = jnp.concatenate([w_in[l][:, :cf], w_in[l][:, cf + H_ATT:]], axis=1).astype(BF16)
        w_f = jnp.pad(w_in[l][:, cf:cf + H_ATT], ((0, 0), (0, LANES - H_ATT))).astype(BF16)
        bf_pad = jnp.pad(b_f[l], (0, LANES - H_ATT)).reshape(1, LANES)
        u = _matmul(x, w_main, TM_PROJ, TN_PROJ)
        flog = _forget_logits(x, w_f, bf_pad, TM_PROJ)[:, :H_ATT]
        logf_p = flog[:N_PROMPT].reshape(BATCH, SEQ, H_ATT)
        logf_s = flog[N_PROMPT:].reshape(DEC_BATCH, DEC_SEQ, H_ATT)

        ft = _cumsum_lanes(jnp.swapaxes(logf_p, 1, 2).reshape(BATCH * H_ATT, SEQ), BATCH * H_ATT)
        fr = ft.reshape(BATCH, H_ATT // 2, 2, SEQ)
        fc = jnp.swapaxes(fr, 2, 3)
        att_p = _attn_prompt(u, fc, fr)

        ls_t = jnp.pad(jnp.swapaxes(logf_s, 1, 2), ((0, 0), (0, 0), (0, LANES - DEC_SEQ)))
        fn = _cumsum_lanes(ls_t.reshape(DEC_BATCH * H_ATT, LANES), DEC_BATCH * H_ATT)
        fn = fn.reshape(DEC_BATCH, H_ATT, LANES)
        fq = fn[:, :, :DEC_SEQ].reshape(DEC_BATCH, H_ATT * DEC_SEQ, 1)
        att_s = _attn_sample(l, page_table, u, ckt, cvt, wt, fq, fn)

        wg = jnp.concatenate([_block_diag(w_gr[l]), _block_diag(w_gi[l])], axis=1).astype(BF16)
        bg = jnp.concatenate([b_gr[l], b_gi[l]]).reshape(1, 2 * REC_W)
        rec_args = (conv_w[l], conv_b[l].reshape(1, REC_W), wg, bg, lam[l].reshape(1, REC_W))
        rec_p, h_p, conv_p = _rglru(u, zero_conv, zero_h, *rec_args,
                                    nseq=BATCH, slen=SEQ, tc=TC_REC, row0=0)
        rec_s, h_s, conv_s = _rglru(u, state_conv[l], state_h[l].reshape(DEC_BATCH, 1, REC_W), *rec_args,
                                    nseq=DEC_BATCH, slen=DEC_SEQ, tc=DEC_SEQ, row0=N_PROMPT)

        att = jnp.concatenate([att_p, att_s], axis=0)
        rec = jnp.concatenate([rec_p, rec_s], axis=0)
        rw = jnp.pad(w_router[l], ((0, 0), (0, LANES - N_EXPERTS))).astype(BF16)
        rb = jnp.pad(router_bias[l], (0, LANES - N_EXPERTS)).reshape(1, LANES)
        x1, gate, cnt = _out_proj(att, rec, x, g_att[l].reshape(1, ATT_W), g_rec[l].reshape(1, REC_W),
                             w_out[l][:ATT_W].astype(BF16), w_out[l][ATT_W:].astype(BF16),
                             ln1_g[l].reshape(1, D_MODEL), ln1_b[l].reshape(1, D_MODEL),
                             rw, rb, TM_OUT)

        mp = _moe_plan(cnt[:, 0, :N_EXPERTS].astype(jnp.int32))
        xg, pmat = _moe_dispatch(mp, x1, gate)
        yg = _moe_grouped(l, mp, xg, w1, w3, w2)
        x = _moe_combine(mp, x1, gate, pmat, yg, ws1[l].astype(BF16), ws3[l].astype(BF16), ws2[l].astype(BF16),
                         ln2_g[l].reshape(1, D_MODEL), ln2_b[l].reshape(1, D_MODEL))

        kv = lambda rows, c0, shape: u[rows, c0:c0 + ATT_W].reshape(shape)
        p_rows, s_rows = slice(0, N_PROMPT), slice(N_PROMPT, N_TOK)
        p_shape = (BATCH, SEQ, H_ATT, HEAD_DIM)
        s_shape = (DEC_BATCH, DEC_SEQ, H_ATT, HEAD_DIM)
        outs["kp"].append(kv(p_rows, ATT_W, p_shape))
        outs["vp"].append(kv(p_rows, 2 * ATT_W, p_shape))
        outs["fp"].append(logf_p)
        outs["hp"].append(h_p.reshape(BATCH, REC_W))
        outs["cp"].append(conv_p)
        outs["ks"].append(kv(s_rows, ATT_W, s_shape))
        outs["vs"].append(kv(s_rows, 2 * ATT_W, s_shape))
        outs["fs"].append(logf_s)
        outs["hs"].append(h_s.reshape(DEC_BATCH, REC_W))
        outs["cs"].append(conv_s)

    st = lambda name: jnp.stack(outs[name])
    return (x[:N_PROMPT].reshape(BATCH, SEQ, D_MODEL), x[N_PROMPT:].reshape(DEC_BATCH, DEC_SEQ, D_MODEL),
            st("kp"), st("vp"), st("fp"), st("hp"), st("cp"),
            st("ks"), st("vs"), st("fs"), st("hs"), st("cs"))
```

```python
import functools

import jax
import jax.numpy as jnp
from jax import lax
from jax.experimental import pallas as pl
from jax.experimental.pallas import tpu as pltpu

F32 = jnp.float32
BF16 = jnp.bfloat16

D_MODEL = 1024
BATCH = 4
SEQ = 4096
DEPTH = 2
DEC_BATCH = 32
DEC_SEQ = 8
PAST_LEN = 8192
PAGE_SIZE = 128
ATT_W = 512
REC_W = 512
HEAD_DIM = 64
H_ATT = 8
N_REC_BLOCKS = 8
CONV_W = 4
RG_C = 8.0
N_EXPERTS = 64
TOP_K = 8
N_GROUPS = 8
TOPK_GROUPS = 4
D_EXPERT = 256
ROUTED_SCALE = 2.5
ALPHA = (2 * DEPTH) ** 0.25
EPS = 1e-5
ATT_SCALE = HEAD_DIM ** -0.5

N_PROMPT = BATCH * SEQ
N_SAMPLE = DEC_BATCH * DEC_SEQ
N_TOK = N_PROMPT + N_SAMPLE
N_PAGES = PAST_LEN // PAGE_SIZE
U_W = 3 * ATT_W + 2 * REC_W
LANES = 128
NEG = -1e30

TM_PROJ = 1280
TN_PROJ = 640
TM_OUT = 1280
TT = 256
ROW_Q = 16
SEG_MAX = TT
LC = TT * TOP_K + N_EXPERTS * ROW_Q
LCP = LC + SEG_MAX
TG = 512
CHUNK_BITS = (256, 128, 64, 32, 16)
ROWS_BITS = (2048, 1024, 512) + CHUNK_BITS
N_TILES = N_TOK // TT
P_ROWS = -(-(N_TOK * TOP_K + N_TILES * N_EXPERTS * (ROW_Q - 1) + N_EXPERTS * (TG - 1)) // TG) * TG
TQ = 512
TC_REC = 512
PAGES_PER_STEP = 32
VMEM_LIMIT = 56 * 1024 * 1024


def _cparams(sem, vmem=None):
    return pltpu.CompilerParams(dimension_semantics=sem, vmem_limit_bytes=vmem)


def _mm_kernel(x_ref, w_ref, o_ref):
    o_ref[...] = jnp.dot(x_ref[...].astype(BF16), w_ref[...], preferred_element_type=F32)


def _matmul(x, w, tm, tn):
    m, k = x.shape
    n = w.shape[1]
    return pl.pallas_call(
        _mm_kernel,
        grid=(m // tm, n // tn),
        in_specs=[pl.BlockSpec((tm, k), lambda i, j: (i, 0)),
                  pl.BlockSpec((k, tn), lambda i, j: (0, j))],
        out_specs=pl.BlockSpec((tm, tn), lambda i, j: (i, j)),
        out_shape=jax.ShapeDtypeStruct((m, n), F32),
        compiler_params=_cparams(("parallel", "arbitrary"), VMEM_LIMIT),
        name="in_proj",
    )(x, w)


def _forget_kernel(x_ref, w_ref, b_ref, o_ref):
    z = jnp.dot(x_ref[...].astype(BF16), w_ref[...], preferred_element_type=F32) + b_ref[...]
    o_ref[...] = jnp.minimum(z, 0.0) - jnp.log1p(jnp.exp(-jnp.abs(z)))


def _forget_logits(x, w, b, tm):
    m, k = x.shape
    n = w.shape[1]
    return pl.pallas_call(
        _forget_kernel,
        grid=(m // tm,),
        in_specs=[pl.BlockSpec((tm, k), lambda i: (i, 0)),
                  pl.BlockSpec((k, n), lambda i: (0, 0)),
                  pl.BlockSpec((1, n), lambda i: (0, 0))],
        out_specs=pl.BlockSpec((tm, n), lambda i: (i, 0)),
        out_shape=jax.ShapeDtypeStruct((m, n), F32),
        compiler_params=_cparams(("parallel",), VMEM_LIMIT),
        name="forget_logits",
    )(x, w, b)


def _split3(x):
    hi = x.astype(BF16)
    r1 = x - hi.astype(F32)
    mid = r1.astype(BF16)
    lo = (r1 - mid.astype(F32)).astype(BF16)
    return hi, mid, lo


def _cumsum_kernel(x_ref, o_ref, *, nchunk):
    rows = x_ref.shape[0]
    r = lax.broadcasted_iota(jnp.int32, (LANES, LANES), 0)
    c = lax.broadcasted_iota(jnp.int32, (LANES, LANES), 1)
    upper = jnp.where(r <= c, 1.0, 0.0).astype(BF16)
    carry = jnp.zeros((rows, 1), F32)
    for ch in range(nchunk):
        x = x_ref[:, ch * LANES:(ch + 1) * LANES]
        hi, mid, lo = _split3(x)
        w = (jnp.dot(hi, upper, preferred_element_type=F32)
             + jnp.dot(mid, upper, preferred_element_type=F32)
             + jnp.dot(lo, upper, preferred_element_type=F32)) + carry
        o_ref[:, ch * LANES:(ch + 1) * LANES] = w
        carry = w[:, LANES - 1:LANES]


def _cumsum_lanes(x, tr):
    rows, s = x.shape
    return pl.pallas_call(
        functools.partial(_cumsum_kernel, nchunk=s // LANES),
        grid=(rows // tr,),
        in_specs=[pl.BlockSpec((tr, s), lambda i: (i, 0))],
        out_specs=pl.BlockSpec((tr, s), lambda i: (i, 0)),
        out_shape=jax.ShapeDtypeStruct((rows, s), F32),
        compiler_params=_cparams(("parallel",)),
        name="cumsum_lanes",
    )(x)


def _attn_prompt_kernel(q_ref, k_ref, v_ref, fc_ref, fr_ref, o_ref, s_sc, kt_sc, vb_sc):
    qb = pl.program_id(2)
    nq = SEQ // TQ

    @pl.when(qb == 0)
    def _():
        for c in range(nq):
            kt_sc[c] = k_ref[c * TQ:(c + 1) * TQ, :].T.astype(BF16)
        vb_sc[...] = v_ref[...].astype(BF16)

    lane = lax.broadcasted_iota(jnp.int32, (TQ, LANES), 1)
    rowh = lax.broadcasted_iota(jnp.int32, (TQ, TQ // 2), 0)
    colh = lax.broadcasted_iota(jnp.int32, (TQ, TQ // 2), 1)
    q = q_ref[...] * ATT_SCALE
    fc = fc_ref[0, 0]
    qm = [jnp.where(lane < HEAD_DIM, q, 0.0).astype(BF16), jnp.where(lane >= HEAD_DIM, q, 0.0).astype(BF16)]
    fq = [fc[:, 0:1], fc[:, 1:2]]

    def scores(kb, ms, masked):
        k0 = pl.multiple_of(kb * TQ, TQ)
        kt = kt_sc[kb]
        out = []
        half = TQ // 2
        for h in range(2):
            m = ms[h]
            for c0 in (0, half):
                fk = fr_ref[0, 0, h:h + 1, pl.ds(k0 + c0, half)]
                s = jnp.dot(qm[h], kt[:, c0:c0 + half], preferred_element_type=F32) + (fq[h] - fk)
                if masked:
                    s = jnp.where(colh + c0 <= rowh, s, NEG)
                s_sc[h, kb, :, c0:c0 + half] = s
                m = jnp.maximum(m, jnp.max(s, axis=1, keepdims=True))
            out.append(m)
        return tuple(out)

    neg = jnp.full((TQ, 1), NEG, F32)
    ms = lax.fori_loop(0, qb, lambda kb, c: scores(kb, c, False), (neg, neg))
    ms = scores(qb, ms, True)

    def exps(kb, ls):
        out = []
        for h in range(2):
            e = jnp.exp(s_sc[h, kb] - ms[h])
            s_sc[h, kb] = e
            out.append(ls[h] + jnp.sum(e, axis=1, keepdims=True))
        return tuple(out)

    zero = jnp.zeros((TQ, 1), F32)
    ls = lax.fori_loop(0, qb + 1, exps, (zero, zero))
    rs = [1.0 / ls[0], 1.0 / ls[1]]

    def values(kb, accs):
        k0 = pl.multiple_of(kb * TQ, TQ)
        v = vb_sc[pl.ds(k0, TQ), :]
        return tuple(accs[h] + jnp.dot((s_sc[h, kb] * rs[h]).astype(BF16), v, preferred_element_type=F32)
                     for h in range(2))

    zacc = jnp.zeros((TQ, LANES), F32)
    outs = lax.fori_loop(0, qb + 1, values, (zacc, zacc))
    o_ref[...] = jnp.where(lane < HEAD_DIM, outs[0], outs[1])


def _attn_prompt(u, fc, fr):
    nq = SEQ // TQ
    kblk = ATT_W // LANES
    return pl.pallas_call(
        _attn_prompt_kernel,
        grid=(BATCH, H_ATT // 2, nq),
        in_specs=[
            pl.BlockSpec((TQ, LANES), lambda b, p, i: (b * nq + i, p)),
            pl.BlockSpec((SEQ, LANES), lambda b, p, i: (b, kblk + p)),
            pl.BlockSpec((SEQ, LANES), lambda b, p, i: (b, 2 * kblk + p)),
            pl.BlockSpec((1, 1, TQ, 2), lambda b, p, i: (b, p, i, 0)),
            pl.BlockSpec((1, 1, 2, SEQ), lambda b, p, i: (b, p, 0, 0)),
        ],
        out_specs=pl.BlockSpec((TQ, LANES), lambda b, p, i: (b * nq + i, p)),
        out_shape=jax.ShapeDtypeStruct((N_PROMPT, ATT_W), F32),
        scratch_shapes=[pltpu.VMEM((2, nq, TQ, TQ), F32), pltpu.VMEM((nq, LANES, TQ), BF16),
                        pltpu.VMEM((SEQ, LANES), BF16)],
        compiler_params=_cparams(("parallel", "parallel", "arbitrary"), VMEM_LIMIT),
        name="attn_prompt",
    )(u, u, u, fc, fr)


def _expand_heads(x):
    n = x.shape[1]
    return jnp.concatenate(
        [jnp.broadcast_to(x[h:h + 1, :], (DEC_SEQ, n)) for h in range(H_ATT)], axis=0)


def _attn_sample_kernel(pt_ref, q_ref, kn_ref, vn_ref, fq_ref, fn_ref, *rest):
    del pt_ref
    g = PAGES_PER_STEP
    nstep = N_PAGES // g
    k_refs, v_refs, w_refs = rest[:g], rest[g:2 * g], rest[2 * g:3 * g]
    o_ref = rest[3 * g]
    qbd_sc, s_sc, m_sc, r_sc, acc_sc, suf_sc = rest[3 * g + 1:]
    step = pl.program_id(1)
    nrow = H_ATT * DEC_SEQ
    row = lax.broadcasted_iota(jnp.int32, (nrow, ATT_W), 0)
    colw = lax.broadcasted_iota(jnp.int32, (nrow, ATT_W), 1)
    own_head = (colw >> 6) == (row >> 3)
    fq = fq_ref[0]
    pad = jnp.zeros((PAGE_SIZE - DEC_SEQ, ATT_W), F32)
    nt = (((1,), (1,)), ((), ()))

    @pl.when(step == 0)
    def _():
        q = q_ref[...] * ATT_SCALE
        qt = jnp.concatenate([q] * H_ATT, axis=0)
        qbd_sc[...] = jnp.where(own_head, qt, 0.0).astype(BF16)
        suf_sc[...] = jnp.zeros((H_ATT, 1), F32)
        kn = jnp.concatenate([kn_ref[...], pad], axis=0).astype(BF16)
        s = lax.dot_general(qbd_sc[...], kn, nt, preferred_element_type=F32)
        s = s + (fq - _expand_heads(fn_ref[0]))
        r = lax.broadcasted_iota(jnp.int32, (nrow, PAGE_SIZE), 0)
        c = lax.broadcasted_iota(jnp.int32, (nrow, PAGE_SIZE), 1)
        s = jnp.where(c <= (r & (DEC_SEQ - 1)), s, NEG)
        s_sc[N_PAGES] = s
        m_sc[...] = jnp.max(s, axis=1, keepdims=True)

    @pl.when(step < nstep)
    def _():
        first = (nstep - 1 - step) * g
        qbd = qbd_sc[...]
        suf = suf_sc[...]
        mrun = jnp.full((nrow, PAGE_SIZE), NEG, F32)
        for i in range(g - 1, -1, -1):
            w = w_refs[i][0, 0]
            suf = suf + w[:, PAGE_SIZE - 1:PAGE_SIZE]
            kt = k_refs[i][0, 0].astype(BF16)
            s = jnp.dot(qbd, kt, preferred_element_type=F32)
            s = s + (fq + _expand_heads(suf - w))
            s_sc[first + i] = s
            mrun = jnp.maximum(mrun, s)
        suf_sc[...] = suf
        m_sc[...] = jnp.maximum(m_sc[...], jnp.max(mrun, axis=1, keepdims=True))

    @pl.when(step == nstep - 1)
    def _():
        m = m_sc[...]

        def slot(p, lacc):
            ex = jnp.exp(s_sc[p] - m)
            s_sc[p] = ex
            return lacc + ex

        lacc = lax.fori_loop(0, N_PAGES + 1, slot, jnp.zeros((nrow, PAGE_SIZE), F32))
        r_sc[...] = 1.0 / jnp.sum(lacc, axis=1, keepdims=True)

    @pl.when(step >= nstep)
    def _():
        first = (2 * nstep - 1 - step) * g
        r = r_sc[...]

        @pl.when(step == nstep)
        def _():
            vn = jnp.concatenate([vn_ref[...], pad], axis=0).astype(BF16)
            acc_sc[...] = jnp.dot((s_sc[N_PAGES] * r).astype(BF16), vn, preferred_element_type=F32)

        acc = acc_sc[...]
        for i in range(g - 1, -1, -1):
            vt = v_refs[i][0, 0].astype(BF16)
            p = (s_sc[first + i] * r).astype(BF16)
            acc = acc + lax.dot_general(p, vt, nt, preferred_element_type=F32)
        acc_sc[...] = acc

    @pl.when(step == 2 * nstep - 1)
    def _():
        o = jnp.where(own_head, acc_sc[...], 0.0)
        out = o[0:DEC_SEQ, :]
        for h in range(1, H_ATT):
            out = out + o[h * DEC_SEQ:(h + 1) * DEC_SEQ, :]
        o_ref[...] = out


def _attn_sample(layer, page_table, u, cache_kt, cache_vt, wt, fq, fn):
    g = PAGES_PER_STEP
    nstep = N_PAGES // g
    srow = N_PROMPT // DEC_SEQ
    nrow = H_ATT * DEC_SEQ

    def key_map(i):
        return lambda b, s, pt: (layer, pt[b, (nstep - 1 - jnp.minimum(s, nstep - 1)) * g + i], 0, 0)

    def value_map(i):
        return lambda b, s, pt: (layer, pt[b, (nstep - 1 - jnp.maximum(s - nstep, 0)) * g + i], 0, 0)

    in_specs = [
        pl.BlockSpec((DEC_SEQ, ATT_W), lambda b, s, pt: (srow + b, 0)),
        pl.BlockSpec((DEC_SEQ, ATT_W), lambda b, s, pt: (srow + b, 1)),
        pl.BlockSpec((DEC_SEQ, ATT_W), lambda b, s, pt: (srow + b, 2)),
        pl.BlockSpec((1, nrow, 1), lambda b, s, pt: (b, 0, 0)),
        pl.BlockSpec((1, H_ATT, LANES), lambda b, s, pt: (b, 0, 0)),
    ]
    in_specs += [pl.BlockSpec((1, 1, ATT_W, PAGE_SIZE), key_map(i)) for i in range(g)]
    in_specs += [pl.BlockSpec((1, 1, ATT_W, PAGE_SIZE), value_map(i)) for i in range(g)]
    in_specs += [pl.BlockSpec((1, 1, H_ATT, PAGE_SIZE), key_map(i)) for i in range(g)]
    grid_spec = pltpu.PrefetchScalarGridSpec(
        num_scalar_prefetch=1,
        grid=(DEC_BATCH, 2 * nstep),
        in_specs=in_specs,
        out_specs=pl.BlockSpec((DEC_SEQ, ATT_W), lambda b, s, pt: (b, 0)),
        scratch_shapes=[
            pltpu.VMEM((nrow, ATT_W), BF16),
            pltpu.VMEM((N_PAGES + 1, nrow, PAGE_SIZE), F32),
            pltpu.VMEM((nrow, 1), F32),
            pltpu.VMEM((nrow, 1), F32),
            pltpu.VMEM((nrow, ATT_W), F32),
            pltpu.VMEM((H_ATT, 1), F32),
        ],
    )
    return pl.pallas_call(
        _attn_sample_kernel,
        grid_spec=grid_spec,
        out_shape=jax.ShapeDtypeStruct((N_SAMPLE, ATT_W), F32),
        compiler_params=_cparams(("parallel", "arbitrary"), VMEM_LIMIT),
        name="attn_sample",
    )(page_table, u, u, u, fq, fn, *([cache_kt] * g), *([cache_vt] * g), *([wt] * g))


def _rglru_kernel(xr_ref, gr_ref, cs_ref, h0_ref, cw_ref, cb_ref, wg_ref, bg_ref, lam_ref,
                  y_ref, hN_ref, cN_ref, xbuf, abuf, bbuf, hbuf, hcar, *, tc):
    c = pl.program_id(1)
    x = xr_ref[...]

    @pl.when(c == 0)
    def _():
        xbuf[0:8, :] = jnp.zeros((8, REC_W), F32)
        xbuf[8 - (CONV_W - 1):8, :] = cs_ref[0]
        hcar[...] = h0_ref[0]

    xbuf[8:8 + tc, :] = x
    xc = cb_ref[...] + xbuf[5:5 + tc, :] * cw_ref[0:1, :]
    xc = xc + xbuf[6:6 + tc, :] * cw_ref[1:2, :]
    xc = xc + xbuf[7:7 + tc, :] * cw_ref[2:3, :]
    xc = xc + x * cw_ref[3:4, :]
    xbuf[0:8, :] = xbuf[tc:tc + 8, :]

    gates = jnp.dot(xc.astype(BF16), wg_ref[...], preferred_element_type=F32) + bg_ref[...]
    rg = jax.nn.sigmoid(gates[:, :REC_W])
    ig = jax.nn.sigmoid(gates[:, REC_W:])
    nl = -lam_ref[...]
    softplus = jnp.maximum(nl, 0.0) + jnp.log1p(jnp.exp(-jnp.abs(nl)))
    log_a = (-RG_C * softplus) * rg
    a = jnp.exp(log_a)
    abuf[...] = a
    bbuf[...] = jnp.sqrt(-jnp.tanh(log_a) * (a * a + 1.0)) * (ig * xc)

    def group(gi, h):
        base = pl.multiple_of(gi * 8, 8)
        a8 = abuf[pl.ds(base, 8), :]
        b8 = bbuf[pl.ds(base, 8), :]
        rows = []
        for r in range(8):
            h = a8[r:r + 1, :] * h + b8[r:r + 1, :]
            rows.append(h)
        hbuf[pl.ds(base, 8), :] = jnp.concatenate(rows, axis=0)
        return h

    h_last = lax.fori_loop(0, tc // 8, group, hcar[...])
    hcar[...] = h_last
    y_ref[...] = hbuf[...] * jax.nn.gelu(gr_ref[...], approximate=True)

    @pl.when(c == pl.num_programs(1) - 1)
    def _():
        hN_ref[0] = h_last
        cN_ref[0] = x[tc - (CONV_W - 1):tc, :]


def _rglru(u, conv_state, h0, cw, cb, wg, bg, lam, *, nseq, slen, tc, row0):
    nchunk = slen // tc
    blk0 = row0 // tc
    xcol = (3 * ATT_W) // REC_W
    rep = lambda shape: pl.BlockSpec(shape, lambda b, c: (0,) * len(shape))
    return pl.pallas_call(
        functools.partial(_rglru_kernel, tc=tc),
        grid=(nseq, nchunk),
        in_specs=[
            pl.BlockSpec((tc, REC_W), lambda b, c: (blk0 + b * nchunk + c, xcol)),
            pl.BlockSpec((tc, REC_W), lambda b, c: (blk0 + b * nchunk + c, xcol + 1)),
            pl.BlockSpec((1, CONV_W - 1, REC_W), lambda b, c: (b, 0, 0)),
            pl.BlockSpec((1, 1, REC_W), lambda b, c: (b, 0, 0)),
            rep((CONV_W, REC_W)), rep((1, REC_W)), rep((REC_W, 2 * REC_W)),
            rep((1, 2 * REC_W)), rep((1, REC_W)),
        ],
        out_specs=[
            pl.BlockSpec((tc, REC_W), lambda b, c: (b * nchunk + c, 0)),
            pl.BlockSpec((1, 1, REC_W), lambda b, c: (b, 0, 0)),
            pl.BlockSpec((1, CONV_W - 1, REC_W), lambda b, c: (b, 0, 0)),
        ],
        out_shape=[
            jax.ShapeDtypeStruct((nseq * slen, REC_W), F32),
            jax.ShapeDtypeStruct((nseq, 1, REC_W), F32),
            jax.ShapeDtypeStruct((nseq, CONV_W - 1, REC_W), F32),
        ],
        scratch_shapes=[
            pltpu.VMEM((tc + 8, REC_W), F32),
            pltpu.VMEM((tc, REC_W), F32),
            pltpu.VMEM((tc, REC_W), F32),
            pltpu.VMEM((tc, REC_W), F32),
            pltpu.VMEM((1, REC_W), F32),
        ],
        compiler_params=_cparams(("parallel", "arbitrary"), VMEM_LIMIT),
        name="rglru",
    )(u, u, conv_state, h0, cw, cb, wg, bg, lam)


def _layer_norm(x, g, b):
    mu = jnp.mean(x, axis=-1, keepdims=True)
    xc = x - mu
    var = jnp.mean(xc * xc, axis=-1, keepdims=True)
    return xc * lax.rsqrt(var + EPS) * g + b


def _rms_norm(x, g):
    return x * lax.rsqrt(jnp.mean(x * x, axis=-1, keepdims=True) + EPS) * g


def _route(logits, rbias):
    shape = logits.shape
    ninf = -jnp.inf
    lane = lax.broadcasted_iota(jnp.int32, shape, 1)
    lane_f = lane.astype(F32)
    gsz = N_EXPERTS // N_GROUPS
    grp_f = (lane >> 3).astype(F32)
    valid = lane < N_EXPERTS
    scores = jax.nn.sigmoid(logits)
    biased = jnp.where(valid, scores + rbias, ninf)

    def first_max(vals, idx_f):
        m = jnp.max(vals, axis=1, keepdims=True)
        i = jnp.min(jnp.where(vals == m, idx_f, 1e9), axis=1, keepdims=True)
        return m, i

    gs = jnp.full(shape, ninf, F32)
    for g in range(N_GROUPS):
        in_g = (lane >= g * gsz) & (lane < (g + 1) * gsz)
        v = jnp.where(in_g, biased, ninf)
        m1, i1 = first_max(v, lane_f)
        m2 = jnp.max(jnp.where(lane_f == i1, ninf, v), axis=1, keepdims=True)
        gs = jnp.where(in_g, m1 + m2, gs)
    keep = jnp.zeros(shape, F32)
    for _ in range(TOPK_GROUPS):
        _, gi = first_max(gs, grp_f)
        pick = grp_f == gi
        keep = jnp.where(pick, 1.0, keep)
        gs = jnp.where(pick, ninf, gs)
    cand = jnp.where(keep > 0.0, biased, ninf)
    sel = jnp.zeros(shape, F32)
    for _ in range(TOP_K):
        _, ei = first_max(cand, lane_f)
        pick = lane_f == ei
        sel = jnp.where(pick, 1.0, sel)
        cand = jnp.where(pick, ninf, cand)
    picked = jnp.where(sel > 0.0, scores, 0.0)
    gate = picked / jnp.sum(picked, axis=1, keepdims=True) * ROUTED_SCALE
    return jnp.where(lane == N_EXPERTS, 1.0, gate)


def _out_kernel(att_ref, rec_ref, x_ref, ga_ref, gr_ref, wa_ref, wr_ref, g1_ref, b1_ref,
                rw_ref, rb_ref, x1_ref, gate_ref, cnt_ref):
    na = _rms_norm(att_ref[...], ga_ref[...]).astype(BF16)
    nr = _rms_norm(rec_ref[...], gr_ref[...]).astype(BF16)
    mix = (jnp.dot(na, wa_ref[...], preferred_element_type=F32)
           + jnp.dot(nr, wr_ref[...], preferred_element_type=F32))
    x1 = _layer_norm(ALPHA * x_ref[...] + mix, g1_ref[...], b1_ref[...])
    x1_ref[...] = x1
    logits = jnp.dot(x1.astype(BF16), rw_ref[...], preferred_element_type=F32)
    gate = _route(logits, rb_ref[...])
    gate_ref[...] = gate
    lane = lax.broadcasted_iota(jnp.int32, gate.shape, 1)
    routed = (gate > 0.0) & (lane < N_EXPERTS)
    hit = jnp.where(routed, 1.0, 0.0)
    for j in range(cnt_ref.shape[0]):
        cnt_ref[j] = jnp.sum(hit[j * TT:(j + 1) * TT, :], axis=0, keepdims=True)


def _out_proj(att, rec, x, ga, gr, wa, wr, g1, b1, rw, rb, tm):
    n = x.shape[0]
    rep = lambda shape: pl.BlockSpec(shape, lambda i: (0,) * len(shape))
    return pl.pallas_call(
        _out_kernel,
        grid=(n // tm,),
        in_specs=[
            pl.BlockSpec((tm, ATT_W), lambda i: (i, 0)),
            pl.BlockSpec((tm, REC_W), lambda i: (i, 0)),
            pl.BlockSpec((tm, D_MODEL), lambda i: (i, 0)),
            rep((1, ATT_W)), rep((1, REC_W)),
            rep((ATT_W, D_MODEL)), rep((REC_W, D_MODEL)),
            rep((1, D_MODEL)), rep((1, D_MODEL)),
            rep((D_MODEL, LANES)), rep((1, LANES)),
        ],
        out_specs=[pl.BlockSpec((tm, D_MODEL), lambda i: (i, 0)),
                   pl.BlockSpec((tm, LANES), lambda i: (i, 0)),
                   pl.BlockSpec((tm // TT, 1, LANES), lambda i: (i, 0, 0))],
        out_shape=[jax.ShapeDtypeStruct((n, D_MODEL), F32),
                   jax.ShapeDtypeStruct((n, LANES), F32),
                   jax.ShapeDtypeStruct((n // TT, 1, LANES), F32)],
        compiler_params=_cparams(("parallel",), VMEM_LIMIT),
        name="out_proj_router",
    )(att, rec, x, ga, gr, wa, wr, g1, b1, rw, rb)


def _moe_plan(cnt):
    c = (cnt + ROW_Q - 1) // ROW_Q * ROW_Q
    lo = jnp.cumsum(c, axis=1) - c
    tot = jnp.sum(c, axis=0)
    cap = (tot + TG - 1) // TG * TG
    ends = jnp.cumsum(cap)
    base = ends - cap
    go = base[None, :] + jnp.cumsum(c, axis=0) - c
    nused = (ends[-1] // TG).astype(jnp.int32).reshape(1)
    nt = N_TILES
    nt3 = P_ROWS // TG
    starts = jnp.arange(nt3, dtype=jnp.int32) * TG
    te = jnp.minimum(jnp.sum((ends[None, :] <= starts[:, None]).astype(jnp.int32), axis=1), N_EXPERTS - 1)
    rows = jnp.sum(c, axis=1).astype(jnp.int32)
    tail = (cap - tot).astype(jnp.int32)
    tdst = (base + tot).astype(jnp.int32)
    pad = lambda a: jnp.pad(a.astype(F32), ((0, 0), (0, LANES - N_EXPERTS)))
    lo_col = pad(lo).reshape(nt, LANES, 1)
    lo_row = pad(lo).reshape(nt, 1, LANES)
    c_row = pad(c).reshape(nt, 1, LANES)
    return dict(c=c.astype(jnp.int32), lo=lo.astype(jnp.int32), go=go.astype(jnp.int32), rows=rows, tail=tail,
                tdst=tdst, te=te, nused=nused, lo_col=lo_col, lo_row=lo_row, c_row=c_row)


def _tile_plan(t, gate, lo_col, lo_s, c_s, dest_sc, p_sc):
    gt = gate.T
    sub = lax.broadcasted_iota(jnp.int32, (LANES, TT), 0)
    sel = (gt > 0.0) & (sub < N_EXPERTS)
    r = lax.broadcasted_iota(jnp.int32, (TT, TT), 0)
    c = lax.broadcasted_iota(jnp.int32, (TT, TT), 1)
    before = jnp.where(r < c, 1.0, 0.0).astype(BF16)
    rank = jnp.dot(jnp.where(sel, 1.0, 0.0).astype(BF16), before, preferred_element_type=F32)
    dest_sc[...] = jnp.where(sel, lo_col + rank, -1.0)
    p_sc[...] = jnp.zeros(p_sc.shape, p_sc.dtype)
    short = SEG_MAX // 4

    def window(e, carry):
        lo_e = pl.multiple_of(lo_s[t, e], ROW_Q)
        d = dest_sc[pl.ds(e, 1), :]
        fits = c_s[t, e] <= short

        def put(n):
            rows = lax.broadcasted_iota(jnp.int32, (n, TT), 0).astype(F32)
            p_sc[pl.ds(lo_e, n), :] = jnp.where(rows + lo_e.astype(F32) == d, 1.0, 0.0).astype(p_sc.dtype)

        pl.when(fits)(lambda: put(short))
        pl.when(jnp.logical_not(fits))(lambda: put(SEG_MAX))
        return carry

    lax.fori_loop(0, N_EXPERTS, window, 0)


def _for_chunks(count, src, dst, fn, bits=CHUNK_BITS):
    small = 4 * ROW_Q

    def pieces(some_bits):
        for bit in some_bits:
            off = count & ~(2 * bit - 1)

            @pl.when((count & bit) != 0)
            def _(off=off, bit=bit):
                fn(src + off, dst + off, bit)

    @pl.when(count >= small)
    def _():
        pieces([b for b in bits if b >= small])

    pieces([b for b in bits if b < small])


def _dispatch_kernel(c_s, lo_s, go_s, rows_s, tail_s, tdst_s, x_ref, gate_ref, locol_ref, xg_ref, p_ref,
                     dest_sc, xs_sc, z_sc, sem, zsem):
    t = pl.program_id(0)
    last = t == pl.num_programs(0) - 1
    slot = t & 1

    def seg_copy(sl, src, dst, n):
        return pltpu.make_async_copy(xs_sc.at[sl, pl.ds(pl.multiple_of(src, ROW_Q), n), :],
                                     xg_ref.at[pl.ds(pl.multiple_of(dst, ROW_Q), n), :], sem.at[sl])

    def wait_tile(tile, sl):
        _for_chunks(rows_s[tile], 0, 0, lambda s, d, n: seg_copy(sl, s, d, n).wait(), ROWS_BITS)

    @pl.when(t >= 2)
    def _():
        wait_tile(t - 2, slot)

    p_sc = p_ref.at[0]
    _tile_plan(t, gate_ref[...], locol_ref[0], lo_s, c_s, dest_sc, p_sc)
    xs_sc[slot] = jnp.dot(p_sc[...], x_ref[...].astype(BF16), preferred_element_type=F32).astype(BF16)

    def start(e, carry):
        _for_chunks(c_s[t, e], lo_s[t, e], go_s[t, e], lambda s, d, n: seg_copy(slot, s, d, n).start())
        return carry

    lax.fori_loop(0, N_EXPERTS, start, 0)

    def zero_copy(src, dst, n):
        return pltpu.make_async_copy(z_sc.at[pl.ds(0, n), :],
                                     xg_ref.at[pl.ds(pl.multiple_of(dst, ROW_Q), n), :], zsem)

    @pl.when(last)
    def _():
        z_sc[...] = jnp.zeros(z_sc.shape, z_sc.dtype)

        def zstart(e, carry):
            _for_chunks(tail_s[e], 0, tdst_s[e], lambda s, d, n: zero_copy(s, d, n).start())
            return carry

        def zwait(e, carry):
            _for_chunks(tail_s[e], 0, tdst_s[e], lambda s, d, n: zero_copy(s, d, n).wait())
            return carry

        lax.fori_loop(0, N_EXPERTS, zstart, 0)

        @pl.when(t >= 1)
        def _():
            wait_tile(t - 1, 1 - slot)

        wait_tile(t, slot)
        lax.fori_loop(0, N_EXPERTS, zwait, 0)


def _moe_dispatch(pl_, x1, gate):
    nt, pmax = N_TILES, P_ROWS
    grid_spec = pltpu.PrefetchScalarGridSpec(
        num_scalar_prefetch=6,
        grid=(nt,),
        in_specs=[
            pl.BlockSpec((TT, D_MODEL), lambda t, *_: (t, 0)),
            pl.BlockSpec((TT, LANES), lambda t, *_: (t, 0)),
            pl.BlockSpec((1, LANES, 1), lambda t, *_: (t, 0, 0)),
        ],
        out_specs=[pl.BlockSpec(memory_space=pl.ANY),
                   pl.BlockSpec((1, LCP, TT), lambda t, *_: (t, 0, 0))],
        scratch_shapes=[
            pltpu.VMEM((LANES, TT), F32),
            pltpu.VMEM((2, LCP, D_MODEL), BF16),
            pltpu.VMEM((CHUNK_BITS[0], D_MODEL), BF16),
            pltpu.SemaphoreType.DMA((2,)),
            pltpu.SemaphoreType.DMA,
        ],
    )
    return pl.pallas_call(
        _dispatch_kernel,
        grid_spec=grid_spec,
        out_shape=[jax.ShapeDtypeStruct((pmax, D_MODEL), BF16),
                   jax.ShapeDtypeStruct((nt, LCP, TT), BF16)],
        compiler_params=_cparams(("arbitrary",), VMEM_LIMIT),
        name="moe_dispatch",
    )(pl_["c"], pl_["lo"], pl_["go"], pl_["rows"], pl_["tail"], pl_["tdst"], x1, gate, pl_["lo_col"])


def _grouped_kernel(te_s, nused_s, x_ref, w1_ref, w3_ref, w2_ref, y_ref):
    i = pl.program_id(0)

    @pl.when(i < nused_s[0])
    def _():
        x = x_ref[...]
        h1 = jnp.dot(x, w1_ref[0, 0].astype(BF16), preferred_element_type=F32)
        h3 = jnp.dot(x, w3_ref[0, 0].astype(BF16), preferred_element_type=F32)
        h = (h1 * jax.nn.sigmoid(h1)) * h3
        y_ref[...] = jnp.dot(h.astype(BF16), w2_ref[0, 0].astype(BF16), preferred_element_type=F32)


def _moe_grouped(layer, pl_, xg, w1, w3, w2):
    pmax = xg.shape[0]
    row = lambda i, te, nu: (jnp.minimum(i, nu[0] - 1), 0)
    wmap = lambda i, te, nu: (layer, te[i], 0, 0)
    grid_spec = pltpu.PrefetchScalarGridSpec(
        num_scalar_prefetch=2,
        grid=(pmax // TG,),
        in_specs=[
            pl.BlockSpec((TG, D_MODEL), row),
            pl.BlockSpec((1, 1, D_MODEL, D_EXPERT), wmap),
            pl.BlockSpec((1, 1, D_MODEL, D_EXPERT), wmap),
            pl.BlockSpec((1, 1, D_EXPERT, D_MODEL), wmap),
        ],
        out_specs=pl.BlockSpec((TG, D_MODEL), row),
    )
    return pl.pallas_call(
        _grouped_kernel,
        grid_spec=grid_spec,
        out_shape=jax.ShapeDtypeStruct((pmax, D_MODEL), F32),
        compiler_params=_cparams(("arbitrary",), VMEM_LIMIT),
        name="moe_grouped",
    )(pl_["te"], pl_["nused"], xg, w1, w3, w2)


def _combine_kernel(c_s, lo_s, go_s, rows_s, x_ref, gate_ref, p_ref, lorow_ref, crow_ref, yg_ref,
                    ws1_ref, ws3_ref, ws2_ref, g2_ref, b2_ref, o_ref, ys_sc, acc_sc, sem):
    t = pl.program_id(0)
    slot = t & 1

    def seg_copy(sl, src, dst, n):
        return pltpu.make_async_copy(yg_ref.at[pl.ds(pl.multiple_of(src, ROW_Q), n), :],
                                     ys_sc.at[sl, pl.ds(pl.multiple_of(dst, ROW_Q), n), :], sem.at[sl])

    def start_tile(tile, sl):
        def start(e, carry):
            _for_chunks(c_s[tile, e], go_s[tile, e], lo_s[tile, e], lambda s, d, n: seg_copy(sl, s, d, n).start())
            return carry

        lax.fori_loop(0, N_EXPERTS, start, 0)

    @pl.when(t == 0)
    def _():
        ys_sc[...] = jnp.zeros(ys_sc.shape, ys_sc.dtype)
        start_tile(0, 0)

    @pl.when(t + 1 < pl.num_programs(0))
    def _():
        start_tile(t + 1, 1 - slot)

    x = x_ref[...]
    xb = x.astype(BF16)
    s1 = jnp.dot(xb, ws1_ref[...], preferred_element_type=F32)
    s3 = jnp.dot(xb, ws3_ref[...], preferred_element_type=F32)
    shared = jnp.dot(((s1 * jax.nn.sigmoid(s1)) * s3).astype(BF16), ws2_ref[...], preferred_element_type=F32)
    g3 = jnp.concatenate(_split3(gate_ref[...]), axis=1)

    _for_chunks(rows_s[t], 0, 0, lambda s, d, n: seg_copy(slot, s, d, n).wait(), ROWS_BITS)

    lo_row = lorow_ref[0]
    c_row = crow_ref[0]
    ck = 256

    def chunk(k0):
        p = p_ref[0, k0:k0 + ck, :]
        pg3 = jnp.dot(p, g3, preferred_element_type=F32)
        pg = (pg3[:, :LANES] + pg3[:, LANES:2 * LANES]) + pg3[:, 2 * LANES:]
        rowi = (lax.broadcasted_iota(jnp.int32, (ck, LANES), 0) + k0).astype(F32)
        mine = (rowi >= lo_row) & (rowi < lo_row + c_row)
        gs = jnp.sum(jnp.where(mine, pg, 0.0), axis=1, keepdims=True)
        inside = jnp.sum(jnp.where(mine, 1.0, 0.0), axis=1, keepdims=True) > 0.0
        y = jnp.where(inside, ys_sc[slot, k0:k0 + ck, :] * gs, 0.0)
        yh = y.astype(BF16)
        yl = (y - yh.astype(F32)).astype(BF16)
        tn = (((0,), (0,)), ((), ()))
        both = lax.dot_general(p, jnp.concatenate([yh, yl], axis=1), tn, preferred_element_type=F32)
        return both[:, :D_MODEL] + both[:, D_MODEL:]

    always = TT * TOP_K
    acc = chunk(0)
    for k0 in range(ck, always, ck):
        acc = acc + chunk(k0)
    acc_sc[...] = acc
    for k0 in range(always, LCP, ck):
        @pl.when(k0 < rows_s[t])
        def _(k0=k0):
            acc_sc[...] += chunk(k0)
    o_ref[...] = _layer_norm(ALPHA * x + (acc_sc[...] + shared), g2_ref[...], b2_ref[...])


def _moe_combine(pl_, x1, gate, pmat, yg, ws1, ws3, ws2, g2, b2):
    nt = N_TILES
    rep = lambda shape: pl.BlockSpec(shape, lambda t, *_: (0,) * len(shape))
    grid_spec = pltpu.PrefetchScalarGridSpec(
        num_scalar_prefetch=4,
        grid=(nt,),
        in_specs=[
            pl.BlockSpec((TT, D_MODEL), lambda t, *_: (t, 0)),
            pl.BlockSpec((TT, LANES), lambda t, *_: (t, 0)),
            pl.BlockSpec((1, LCP, TT), lambda t, *_: (t, 0, 0)),
            pl.BlockSpec((1, 1, LANES), lambda t, *_: (t, 0, 0)),
            pl.BlockSpec((1, 1, LANES), lambda t, *_: (t, 0, 0)),
            pl.BlockSpec(memory_space=pl.ANY),
            rep((D_MODEL, D_EXPERT)), rep((D_MODEL, D_EXPERT)), rep((D_EXPERT, D_MODEL)),
            rep((1, D_MODEL)), rep((1, D_MODEL)),
        ],
        out_specs=pl.BlockSpec((TT, D_MODEL), lambda t, *_: (t, 0)),
        scratch_shapes=[
            pltpu.VMEM((2, LCP, D_MODEL), F32),
            pltpu.VMEM((TT, D_MODEL), F32),
            pltpu.SemaphoreType.DMA((2,)),
        ],
    )
    return pl.pallas_call(
        _combine_kernel,
        grid_spec=grid_spec,
        out_shape=jax.ShapeDtypeStruct((nt * TT, D_MODEL), F32),
        compiler_params=_cparams(("arbitrary",), VMEM_LIMIT),
        name="moe_combine",
    )(pl_["c"], pl_["lo"], pl_["go"], pl_["rows"], x1, gate, pmat, pl_["lo_row"], pl_["c_row"], yg,
      ws1, ws3, ws2, g2, b2)


def _block_diag(w):
    nb, k, j = w.shape
    eye = jnp.eye(nb, dtype=w.dtype)
    return jnp.einsum('nkj,nm->nkmj', w, eye).reshape(nb * k, nb * j)


def kernel(x_prompt, x_sample, cache_k, cache_v, cache_logf, state_h, state_conv, page_table, w_in, b_f, conv_w, conv_b, w_gr, b_gr, w_gi, b_gi, lam, g_att, g_rec, w_out, ln1_g, ln1_b, w_router, router_bias, w1, w3, w2, ws1, ws3, ws2, ln2_g, ln2_b):
    n_pool = cache_k.shape[1]
    x = jnp.concatenate([x_prompt.reshape(N_PROMPT, D_MODEL), x_sample.reshape(N_SAMPLE, D_MODEL)], axis=0)
    ckt = jnp.transpose(cache_k, (0, 1, 3, 4, 2)).reshape(DEPTH, n_pool, ATT_W, PAGE_SIZE)
    cvt = jnp.transpose(cache_v, (0, 1, 3, 4, 2)).reshape(DEPTH, n_pool, ATT_W, PAGE_SIZE)
    lf_t = jnp.swapaxes(cache_logf, -1, -2).reshape(DEPTH * n_pool * H_ATT, PAGE_SIZE)
    wt = _cumsum_lanes(lf_t, 4096).reshape(DEPTH, n_pool, H_ATT, PAGE_SIZE)
    zero_conv = jnp.zeros((BATCH, CONV_W - 1, REC_W), F32)
    zero_h = jnp.zeros((BATCH, 1, REC_W), F32)
    cf = 3 * ATT_W

    outs = {k: [] for k in ("kp", "vp", "fp", "hp", "cp", "ks", "vs", "fs", "hs", "cs")}
    for l in range(DEPTH):
        w_main = jnp.concatenate([w_in[l][:, :cf], w_in[l][:, cf + H_ATT:]], axis=1).astype(BF16)
        w_f = jnp.pad(w_in[l][:, cf:cf + H_ATT], ((0, 0), (0, LANES - H_ATT))).astype(BF16)
        bf_pad = jnp.pad(b_f[l], (0, LANES - H_ATT)).reshape(1, LANES)
        u = _matmul(x, w_main, TM_PROJ, TN_PROJ)
        flog = _forget_logits(x, w_f, bf_pad, TM_PROJ)[:, :H_ATT]
        logf_p = flog[:N_PROMPT].reshape(BATCH, SEQ, H_ATT)
        logf_s = flog[N_PROMPT:].reshape(DEC_BATCH, DEC_SEQ, H_ATT)

        ft = _cumsum_lanes(jnp.swapaxes(logf_p, 1, 2).reshape(BATCH * H_ATT, SEQ), BATCH * H_ATT)
        fr = ft.reshape(BATCH, H_ATT // 2, 2, SEQ)
        fc = jnp.swapaxes(fr, 2, 3)
        att_p = _attn_prompt(u, fc, fr)

        ls_t = jnp.pad(jnp.swapaxes(logf_s, 1, 2), ((0, 0), (0, 0), (0, LANES - DEC_SEQ)))
        fn = _cumsum_lanes(ls_t.reshape(DEC_BATCH * H_ATT, LANES), DEC_BATCH * H_ATT)
        fn = fn.reshape(DEC_BATCH, H_ATT, LANES)
        fq = fn[:, :, :DEC_SEQ].reshape(DEC_BATCH, H_ATT * DEC_SEQ, 1)
        att_s = _attn_sample(l, page_table, u, ckt, cvt, wt, fq, fn)

        wg = jnp.concatenate([_block_diag(w_gr[l]), _block_diag(w_gi[l])], axis=1).astype(BF16)
        bg = jnp.concatenate([b_gr[l], b_gi[l]]).reshape(1, 2 * REC_W)
        rec_args = (conv_w[l], conv_b[l].reshape(1, REC_W), wg, bg, lam[l].reshape(1, REC_W))
        rec_p, h_p, conv_p = _rglru(u, zero_conv, zero_h, *rec_args,
                                    nseq=BATCH, slen=SEQ, tc=TC_REC, row0=0)
        rec_s, h_s, conv_s = _rglru(u, state_conv[l], state_h[l].reshape(DEC_BATCH, 1, REC_W), *rec_args,
                                    nseq=DEC_BATCH, slen=DEC_SEQ, tc=DEC_SEQ, row0=N_PROMPT)

        att = jnp.concatenate([att_p, att_s], axis=0)
        rec = jnp.concatenate([rec_p, rec_s], axis=0)
        rw = jnp.pad(w_router[l], ((0, 0), (0, LANES - N_EXPERTS))).astype(BF16)
        rb = jnp.pad(router_bias[l], (0, LANES - N_EXPERTS)).reshape(1, LANES)
        x1, gate, cnt = _out_proj(att, rec, x, g_att[l].reshape(1, ATT_W), g_rec[l].reshape(1, REC_W),
                             w_out[l][:ATT_W].astype(BF16), w_out[l][ATT_W:].astype(BF16),
                             ln1_g[l].reshape(1, D_MODEL), ln1_b[l].reshape(1, D_MODEL),
                             rw, rb, TM_OUT)

        mp = _moe_plan(cnt[:, 0, :N_EXPERTS].astype(jnp.int32))
        xg, pmat = _moe_dispatch(mp, x1, gate)
        yg = _moe_grouped(l, mp, xg, w1, w3, w2)
        x = _moe_combine(mp, x1, gate, pmat, yg, ws1[l].astype(BF16), ws3[l].astype(BF16), ws2[l].astype(BF16),
                         ln2_g[l].reshape(1, D_MODEL), ln2_b[l].reshape(1, D_MODEL))

        kv = lambda rows, c0, shape: u[rows, c0:c0 + ATT_W].reshape(shape)
        p_rows, s_rows = slice(0, N_PROMPT), slice(N_PROMPT, N_TOK)
        p_shape = (BATCH, SEQ, H_ATT, HEAD_DIM)
        s_shape = (DEC_BATCH, DEC_SEQ, H_ATT, HEAD_DIM)
        outs["kp"].append(kv(p_rows, ATT_W, p_shape))
        outs["vp"].append(kv(p_rows, 2 * ATT_W, p_shape))
        outs["fp"].append(logf_p)
        outs["hp"].append(h_p.reshape(BATCH, REC_W))
        outs["cp"].append(conv_p)
        outs["ks"].append(kv(s_rows, ATT_W, s_shape))
        outs["vs"].append(kv(s_rows, 2 * ATT_W, s_shape))
        outs["fs"].append(logf_s)
        outs["hs"].append(h_s.reshape(DEC_BATCH, REC_W))
        outs["cs"].append(conv_s)

    st = lambda name: jnp.stack(outs[name])
    return (x[:N_PROMPT].reshape(BATCH, SEQ, D_MODEL), x[N_PROMPT:].reshape(DEC_BATCH, DEC_SEQ, D_MODEL),
            st("kp"), st("vp"), st("fp"), st("hp"), st("cp"),
            st("ks"), st("vs"), st("fs"), st("hs"), st("cs"))
```

```python
import functools

import jax
import jax.numpy as jnp
from jax import lax
from jax.experimental import pallas as pl
from jax.experimental.pallas import tpu as pltpu

F32 = jnp.float32
BF16 = jnp.bfloat16

D_MODEL = 1024
BATCH = 4
SEQ = 4096
DEPTH = 2
DEC_BATCH = 32
DEC_SEQ = 8
PAST_LEN = 8192
PAGE_SIZE = 128
ATT_W = 512
REC_W = 512
HEAD_DIM = 64
H_ATT = 8
N_REC_BLOCKS = 8
CONV_W = 4
RG_C = 8.0
N_EXPERTS = 64
TOP_K = 8
N_GROUPS = 8
TOPK_GROUPS = 4
D_EXPERT = 256
ROUTED_SCALE = 2.5
ALPHA = (2 * DEPTH) ** 0.25
EPS = 1e-5
ATT_SCALE = HEAD_DIM ** -0.5

N_PROMPT = BATCH * SEQ
N_SAMPLE = DEC_BATCH * DEC_SEQ
N_TOK = N_PROMPT + N_SAMPLE
N_PAGES = PAST_LEN // PAGE_SIZE
U_W = 3 * ATT_W + 2 * REC_W
LANES = 128
NEG = -1e30

TM_PROJ = 1280
TN_PROJ = 640
TM_OUT = 1280
TT = 256
ROW_Q = 16
SEG_MAX = TT
LC = TT * TOP_K + N_EXPERTS * ROW_Q
LCP = LC + SEG_MAX
TG = 512
CHUNK_BITS = (256, 128, 64, 32, 16)
ROWS_BITS = (2048, 1024, 512) + CHUNK_BITS
N_TILES = N_TOK // TT
P_ROWS = -(-(N_TOK * TOP_K + N_TILES * N_EXPERTS * (ROW_Q - 1) + N_EXPERTS * (TG - 1)) // TG) * TG
TQ = 512
TC_REC = 512
PAGES_PER_STEP = 32
VMEM_LIMIT = 56 * 1024 * 1024


def _cparams(sem, vmem=None):
    return pltpu.CompilerParams(dimension_semantics=sem, vmem_limit_bytes=vmem)


def _mm_kernel(x_ref, w_ref, o_ref):
    o_ref[...] = jnp.dot(x_ref[...].astype(BF16), w_ref[...], preferred_element_type=F32)


def _matmul(x, w, tm, tn):
    m, k = x.shape
    n = w.shape[1]
    return pl.pallas_call(
        _mm_kernel,
        grid=(m // tm, n // tn),
        in_specs=[pl.BlockSpec((tm, k), lambda i, j: (i, 0)),
                  pl.BlockSpec((k, tn), lambda i, j: (0, j))],
        out_specs=pl.BlockSpec((tm, tn), lambda i, j: (i, j)),
        out_shape=jax.ShapeDtypeStruct((m, n), F32),
        compiler_params=_cparams(("parallel", "arbitrary"), VMEM_LIMIT),
        name="in_proj",
    )(x, w)


def _forget_kernel(x_ref, w_ref, b_ref, o_ref):
    z = jnp.dot(x_ref[...].astype(BF16), w_ref[...], preferred_element_type=F32) + b_ref[...]
    o_ref[...] = jnp.minimum(z, 0.0) - jnp.log1p(jnp.exp(-jnp.abs(z)))


def _forget_logits(x, w, b, tm):
    m, k = x.shape
    n = w.shape[1]
    return pl.pallas_call(
        _forget_kernel,
        grid=(m // tm,),
        in_specs=[pl.BlockSpec((tm, k), lambda i: (i, 0)),
                  pl.BlockSpec((k, n), lambda i: (0, 0)),
                  pl.BlockSpec((1, n), lambda i: (0, 0))],
        out_specs=pl.BlockSpec((tm, n), lambda i: (i, 0)),
        out_shape=jax.ShapeDtypeStruct((m, n), F32),
        compiler_params=_cparams(("parallel",), VMEM_LIMIT),
        name="forget_logits",
    )(x, w, b)


def _split3(x):
    hi = x.astype(BF16)
    r1 = x - hi.astype(F32)
    mid = r1.astype(BF16)
    lo = (r1 - mid.astype(F32)).astype(BF16)
    return hi, mid, lo


def _cumsum_kernel(x_ref, o_ref, *, nchunk):
    rows = x_ref.shape[0]
    r = lax.broadcasted_iota(jnp.int32, (LANES, LANES), 0)
    c = lax.broadcasted_iota(jnp.int32, (LANES, LANES), 1)
    upper = jnp.where(r <= c, 1.0, 0.0).astype(BF16)
    carry = jnp.zeros((rows, 1), F32)
    for ch in range(nchunk):
        x = x_ref[:, ch * LANES:(ch + 1) * LANES]
        hi, mid, lo = _split3(x)
        w = (jnp.dot(hi, upper, preferred_element_type=F32)
             + jnp.dot(mid, upper, preferred_element_type=F32)
             + jnp.dot(lo, upper, preferred_element_type=F32)) + carry
        o_ref[:, ch * LANES:(ch + 1) * LANES] = w
        carry = w[:, LANES - 1:LANES]


def _cumsum_lanes(x, tr):
    rows, s = x.shape
    return pl.pallas_call(
        functools.partial(_cumsum_kernel, nchunk=s // LANES),
        grid=(rows // tr,),
        in_specs=[pl.BlockSpec((tr, s), lambda i: (i, 0))],
        out_specs=pl.BlockSpec((tr, s), lambda i: (i, 0)),
        out_shape=jax.ShapeDtypeStruct((rows, s), F32),
        compiler_params=_cparams(("parallel",)),
        name="cumsum_lanes",
    )(x)


def _attn_prompt_kernel(q_ref, k_ref, v_ref, fc_ref, fr_ref, o_ref, s_sc, kt_sc, vb_sc):
    qb = pl.program_id(2)
    nq = SEQ // TQ

    @pl.when(qb == 0)
    def _():
        for c in range(nq):
            kt_sc[c] = k_ref[c * TQ:(c + 1) * TQ, :].T.astype(BF16)
        vb_sc[...] = v_ref[...].astype(BF16)

    lane = lax.broadcasted_iota(jnp.int32, (TQ, LANES), 1)
    rowh = lax.broadcasted_iota(jnp.int32, (TQ, TQ // 2), 0)
    colh = lax.broadcasted_iota(jnp.int32, (TQ, TQ // 2), 1)
    q = q_ref[...] * ATT_SCALE
    fc = fc_ref[0, 0]
    qm = [jnp.where(lane < HEAD_DIM, q, 0.0).astype(BF16), jnp.where(lane >= HEAD_DIM, q, 0.0).astype(BF16)]
    fq = [fc[:, 0:1], fc[:, 1:2]]

    def scores(kb, ms, masked):
        k0 = pl.multiple_of(kb * TQ, TQ)
        kt = kt_sc[kb]
        out = []
        half = TQ // 2
        for h in range(2):
            m = ms[h]
            for c0 in (0, half):
                fk = fr_ref[0, 0, h:h + 1, pl.ds(k0 + c0, half)]
                s = jnp.dot(qm[h], kt[:, c0:c0 + half], preferred_element_type=F32) + (fq[h] - fk)
                if masked:
                    s = jnp.where(colh + c0 <= rowh, s, NEG)
                s_sc[h, kb, :, c0:c0 + half] = s
                m = jnp.maximum(m, jnp.max(s, axis=1, keepdims=True))
            out.append(m)
        return tuple(out)

    neg = jnp.full((TQ, 1), NEG, F32)
    ms = lax.fori_loop(0, qb, lambda kb, c: scores(kb, c, False), (neg, neg))
    ms = scores(qb, ms, True)

    def exps(kb, ls):
        out = []
        for h in range(2):
            e = jnp.exp(s_sc[h, kb] - ms[h])
            s_sc[h, kb] = e
            out.append(ls[h] + jnp.sum(e, axis=1, keepdims=True))
        return tuple(out)

    zero = jnp.zeros((TQ, 1), F32)
    ls = lax.fori_loop(0, qb + 1, exps, (zero, zero))
    rs = [1.0 / ls[0], 1.0 / ls[1]]

    def values(kb, accs):
        k0 = pl.multiple_of(kb * TQ, TQ)
        v = vb_sc[pl.ds(k0, TQ), :]
        return tuple(accs[h] + jnp.dot((s_sc[h, kb] * rs[h]).astype(BF16), v, preferred_element_type=F32)
                     for h in range(2))

    zacc = jnp.zeros((TQ, LANES), F32)
    outs = lax.fori_loop(0, qb + 1, values, (zacc, zacc))
    o_ref[...] = jnp.where(lane < HEAD_DIM, outs[0], outs[1])


def _attn_prompt(u, fc, fr):
    nq = SEQ // TQ
    kblk = ATT_W // LANES
    return pl.pallas_call(
        _attn_prompt_kernel,
        grid=(BATCH, H_ATT // 2, nq),
        in_specs=[
            pl.BlockSpec((TQ, LANES), lambda b, p, i: (b * nq + i, p)),
            pl.BlockSpec((SEQ, LANES), lambda b, p, i: (b, kblk + p)),
            pl.BlockSpec((SEQ, LANES), lambda b, p, i: (b, 2 * kblk + p)),
            pl.BlockSpec((1, 1, TQ, 2), lambda b, p, i: (b, p, i, 0)),
            pl.BlockSpec((1, 1, 2, SEQ), lambda b, p, i: (b, p, 0, 0)),
        ],
        out_specs=pl.BlockSpec((TQ, LANES), lambda b, p, i: (b * nq + i, p)),
        out_shape=jax.ShapeDtypeStruct((N_PROMPT, ATT_W), F32),
        scratch_shapes=[pltpu.VMEM((2, nq, TQ, TQ), F32), pltpu.VMEM((nq, LANES, TQ), BF16),
                        pltpu.VMEM((SEQ, LANES), BF16)],
        compiler_params=_cparams(("parallel", "parallel", "arbitrary"), VMEM_LIMIT),
        name="attn_prompt",
    )(u, u, u, fc, fr)


def _expand_heads(x):
    n = x.shape[1]
    return jnp.concatenate(
        [jnp.broadcast_to(x[h:h + 1, :], (DEC_SEQ, n)) for h in range(H_ATT)], axis=0)


def _attn_sample_kernel(pt_ref, q_ref, kn_ref, vn_ref, fq_ref, fn_ref, *rest):
    del pt_ref
    g = PAGES_PER_STEP
    nstep = N_PAGES // g
    k_refs, v_refs, w_refs = rest[:g], rest[g:2 * g], rest[2 * g:3 * g]
    o_ref = rest[3 * g]
    qbd_sc, s_sc, m_sc, r_sc, acc_sc, suf_sc = rest[3 * g + 1:]
    step = pl.program_id(1)
    nrow = H_ATT * DEC_SEQ
    row = lax.broadcasted_iota(jnp.int32, (nrow, ATT_W), 0)
    colw = lax.broadcasted_iota(jnp.int32, (nrow, ATT_W), 1)
    own_head = (colw >> 6) == (row >> 3)
    fq = fq_ref[0]
    pad = jnp.zeros((PAGE_SIZE - DEC_SEQ, ATT_W), F32)
    nt = (((1,), (1,)), ((), ()))

    @pl.when(step == 0)
    def _():
        q = q_ref[...] * ATT_SCALE
        qt = jnp.concatenate([q] * H_ATT, axis=0)
        qbd_sc[...] = jnp.where(own_head, qt, 0.0).astype(BF16)
        suf_sc[...] = jnp.zeros((H_ATT, 1), F32)
        kn = jnp.concatenate([kn_ref[...], pad], axis=0).astype(BF16)
        s = lax.dot_general(qbd_sc[...], kn, nt, preferred_element_type=F32)
        s = s + (fq - _expand_heads(fn_ref[0]))
        r = lax.broadcasted_iota(jnp.int32, (nrow, PAGE_SIZE), 0)
        c = lax.broadcasted_iota(jnp.int32, (nrow, PAGE_SIZE), 1)
        s = jnp.where(c <= (r & (DEC_SEQ - 1)), s, NEG)
        s_sc[N_PAGES] = s
        m_sc[...] = jnp.max(s, axis=1, keepdims=True)

    @pl.when(step < nstep)
    def _():
        first = (nstep - 1 - step) * g
        qbd = qbd_sc[...]
        suf = suf_sc[...]
        mrun = jnp.full((nrow, PAGE_SIZE), NEG, F32)
        for i in range(g - 1, -1, -1):
            w = w_refs[i][0, 0]
            suf = suf + w[:, PAGE_SIZE - 1:PAGE_SIZE]
            kt = k_refs[i][0, 0].astype(BF16)
            s = jnp.dot(qbd, kt, preferred_element_type=F32)
            s = s + (fq + _expand_heads(suf - w))
            s_sc[first + i] = s
            mrun = jnp.maximum(mrun, s)
        suf_sc[...] = suf
        m_sc[...] = jnp.maximum(m_sc[...], jnp.max(mrun, axis=1, keepdims=True))

    @pl.when(step == nstep - 1)
    def _():
        m = m_sc[...]

        def slot(p, lacc):
            ex = jnp.exp(s_sc[p] - m)
            s_sc[p] = ex
            return lacc + ex

        lacc = lax.fori_loop(0, N_PAGES + 1, slot, jnp.zeros((nrow, PAGE_SIZE), F32))
        r_sc[...] = 1.0 / jnp.sum(lacc, axis=1, keepdims=True)

    @pl.when(step >= nstep)
    def _():
        first = (2 * nstep - 1 - step) * g
        r = r_sc[...]

        @pl.when(step == nstep)
        def _():
            vn = jnp.concatenate([vn_ref[...], pad], axis=0).astype(BF16)
            acc_sc[...] = jnp.dot((s_sc[N_PAGES] * r).astype(BF16), vn, preferred_element_type=F32)

        acc = acc_sc[...]
        for i in range(g - 1, -1, -1):
            vt = v_refs[i][0, 0].astype(BF16)
            p = (s_sc[first + i] * r).astype(BF16)
            acc = acc + lax.dot_general(p, vt, nt, preferred_element_type=F32)
        acc_sc[...] = acc

    @pl.when(step == 2 * nstep - 1)
    def _():
        o = jnp.where(own_head, acc_sc[...], 0.0)
        out = o[0:DEC_SEQ, :]
        for h in range(1, H_ATT):
            out = out + o[h * DEC_SEQ:(h + 1) * DEC_SEQ, :]
        o_ref[...] = out


def _attn_sample(layer, page_table, u, cache_kt, cache_vt, wt, fq, fn):
    g = PAGES_PER_STEP
    nstep = N_PAGES // g
    srow = N_PROMPT // DEC_SEQ
    nrow = H_ATT * DEC_SEQ

    def key_map(i):
        return lambda b, s, pt: (layer, pt[b, (nstep - 1 - jnp.minimum(s, nstep - 1)) * g + i], 0, 0)

    def value_map(i):
        return lambda b, s, pt: (layer, pt[b, (nstep - 1 - jnp.maximum(s - nstep, 0)) * g + i], 0, 0)

    in_specs = [
        pl.BlockSpec((DEC_SEQ, ATT_W), lambda b, s, pt: (srow + b, 0)),
        pl.BlockSpec((DEC_SEQ, ATT_W), lambda b, s, pt: (srow + b, 1)),
        pl.BlockSpec((DEC_SEQ, ATT_W), lambda b, s, pt: (srow + b, 2)),
        pl.BlockSpec((1, nrow, 1), lambda b, s, pt: (b, 0, 0)),
        pl.BlockSpec((1, H_ATT, LANES), lambda b, s, pt: (b, 0, 0)),
    ]
    in_specs += [pl.BlockSpec((1, 1, ATT_W, PAGE_SIZE), key_map(i)) for i in range(g)]
    in_specs += [pl.BlockSpec((1, 1, ATT_W, PAGE_SIZE), value_map(i)) for i in range(g)]
    in_specs += [pl.BlockSpec((1, 1, H_ATT, PAGE_SIZE), key_map(i)) for i in range(g)]
    grid_spec = pltpu.PrefetchScalarGridSpec(
        num_scalar_prefetch=1,
        grid=(DEC_BATCH, 2 * nstep),
        in_specs=in_specs,
        out_specs=pl.BlockSpec((DEC_SEQ, ATT_W), lambda b, s, pt: (b, 0)),
        scratch_shapes=[
            pltpu.VMEM((nrow, ATT_W), BF16),
            pltpu.VMEM((N_PAGES + 1, nrow, PAGE_SIZE), F32),
            pltpu.VMEM((nrow, 1), F32),
            pltpu.VMEM((nrow, 1), F32),
            pltpu.VMEM((nrow, ATT_W), F32),
            pltpu.VMEM((H_ATT, 1), F32),
        ],
    )
    return pl.pallas_call(
        _attn_sample_kernel,
        grid_spec=grid_spec,
        out_shape=jax.ShapeDtypeStruct((N_SAMPLE, ATT_W), F32),
        compiler_params=_cparams(("parallel", "arbitrary"), VMEM_LIMIT),
        name="attn_sample",
    )(page_table, u, u, u, fq, fn, *([cache_kt] * g), *([cache_vt] * g), *([wt] * g))


def _rglru_kernel(xr_ref, gr_ref, cs_ref, h0_ref, cw_ref, cb_ref, wg_ref, bg_ref, lam_ref,
                  y_ref, hN_ref, cN_ref, xbuf, abuf, bbuf, hbuf, hcar, *, tc):
    c = pl.program_id(1)
    x = xr_ref[...]

    @pl.when(c == 0)
    def _():
        xbuf[0:8, :] = jnp.zeros((8, REC_W), F32)
        xbuf[8 - (CONV_W - 1):8, :] = cs_ref[0]
        hcar[...] = h0_ref[0]

    xbuf[8:8 + tc, :] = x
    xc = cb_ref[...] + xbuf[5:5 + tc, :] * cw_ref[0:1, :]
    xc = xc + xbuf[6:6 + tc, :] * cw_ref[1:2, :]
    xc = xc + xbuf[7:7 + tc, :] * cw_ref[2:3, :]
    xc = xc + x * cw_ref[3:4, :]
    xbuf[0:8, :] = xbuf[tc:tc + 8, :]

    gates = jnp.dot(xc.astype(BF16), wg_ref[...], preferred_element_type=F32) + bg_ref[...]
    rg = jax.nn.sigmoid(gates[:, :REC_W])
    ig = jax.nn.sigmoid(gates[:, REC_W:])
    nl = -lam_ref[...]
    softplus = jnp.maximum(nl, 0.0) + jnp.log1p(jnp.exp(-jnp.abs(nl)))
    log_a = (-RG_C * softplus) * rg
    a = jnp.exp(log_a)
    abuf[...] = a
    bbuf[...] = jnp.sqrt(-jnp.tanh(log_a) * (a * a + 1.0)) * (ig * xc)

    def group(gi, h):
        base = pl.multiple_of(gi * 8, 8)
        a8 = abuf[pl.ds(base, 8), :]
        b8 = bbuf[pl.ds(base, 8), :]
        rows = []
        for r in range(8):
            h = a8[r:r + 1, :] * h + b8[r:r + 1, :]
            rows.append(h)
        hbuf[pl.ds(base, 8), :] = jnp.concatenate(rows, axis=0)
        return h

    h_last = lax.fori_loop(0, tc // 8, group, hcar[...])
    hcar[...] = h_last
    y_ref[...] = hbuf[...] * jax.nn.gelu(gr_ref[...], approximate=True)

    @pl.when(c == pl.num_programs(1) - 1)
    def _():
        hN_ref[0] = h_last
        cN_ref[0] = x[tc - (CONV_W - 1):tc, :]


def _rglru(u, conv_state, h0, cw, cb, wg, bg, lam, *, nseq, slen, tc, row0):
    nchunk = slen // tc
    blk0 = row0 // tc
    xcol = (3 * ATT_W) // REC_W
    rep = lambda shape: pl.BlockSpec(shape, lambda b, c: (0,) * len(shape))
    return pl.pallas_call(
        functools.partial(_rglru_kernel, tc=tc),
        grid=(nseq, nchunk),
        in_specs=[
            pl.BlockSpec((tc, REC_W), lambda b, c: (blk0 + b * nchunk + c, xcol)),
            pl.BlockSpec((tc, REC_W), lambda b, c: (blk0 + b * nchunk + c, xcol + 1)),
            pl.BlockSpec((1, CONV_W - 1, REC_W), lambda b, c: (b, 0, 0)),
            pl.BlockSpec((1, 1, REC_W), lambda b, c: (b, 0, 0)),
            rep((CONV_W, REC_W)), rep((1, REC_W)), rep((REC_W, 2 * REC_W)),
            rep((1, 2 * REC_W)), rep((1, REC_W)),
        ],
        out_specs=[
            pl.BlockSpec((tc, REC_W), lambda b, c: (b * nchunk + c, 0)),
            pl.BlockSpec((1, 1, REC_W), lambda b, c: (b, 0, 0)),
            pl.BlockSpec((1, CONV_W - 1, REC_W), lambda b, c: (b, 0, 0)),
        ],
        out_shape=[
            jax.ShapeDtypeStruct((nseq * slen, REC_W), F32),
            jax.ShapeDtypeStruct((nseq, 1, REC_W), F32),
            jax.ShapeDtypeStruct((nseq, CONV_W - 1, REC_W), F32),
        ],
        scratch_shapes=[
            pltpu.VMEM((tc + 8, REC_W), F32),
            pltpu.VMEM((tc, REC_W), F32),
            pltpu.VMEM((tc, REC_W), F32),
            pltpu.VMEM((tc, REC_W), F32),
            pltpu.VMEM((1, REC_W), F32),
        ],
        compiler_params=_cparams(("parallel", "arbitrary"), VMEM_LIMIT),
        name="rglru",
    )(u, u, conv_state, h0, cw, cb, wg, bg, lam)


def _layer_norm(x, g, b):
    mu = jnp.mean(x, axis=-1, keepdims=True)
    xc = x - mu
    var = jnp.mean(xc * xc, axis=-1, keepdims=True)
    return xc * lax.rsqrt(var + EPS) * g + b


def _rms_norm(x, g):
    return x * lax.rsqrt(jnp.mean(x * x, axis=-1, keepdims=True) + EPS) * g


def _route(logits, rbias):
    shape = logits.shape
    ninf = -jnp.inf
    lane = lax.broadcasted_iota(jnp.int32, shape, 1)
    lane_f = lane.astype(F32)
    gsz = N_EXPERTS // N_GROUPS
    grp_f = (lane >> 3).astype(F32)
    valid = lane < N_EXPERTS
    scores = jax.nn.sigmoid(logits)
    biased = jnp.where(valid, scores + rbias, ninf)

    def first_max(vals, idx_f):
        m = jnp.max(vals, axis=1, keepdims=True)
        i = jnp.min(jnp.where(vals == m, idx_f, 1e9), axis=1, keepdims=True)
        return m, i

    gs = jnp.full(shape, ninf, F32)
    for g in range(N_GROUPS):
        in_g = (lane >= g * gsz) & (lane < (g + 1) * gsz)
        v = jnp.where(in_g, biased, ninf)
        m1, i1 = first_max(v, lane_f)
        m2 = jnp.max(jnp.where(lane_f == i1, ninf, v), axis=1, keepdims=True)
        gs = jnp.where(in_g, m1 + m2, gs)
    keep = jnp.zeros(shape, F32)
    for _ in range(TOPK_GROUPS):
        _, gi = first_max(gs, grp_f)
        pick = grp_f == gi
        keep = jnp.where(pick, 1.0, keep)
        gs = jnp.where(pick, ninf, gs)
    cand = jnp.where(keep > 0.0, biased, ninf)
    sel = jnp.zeros(shape, F32)
    for _ in range(TOP_K):
        _, ei = first_max(cand, lane_f)
        pick = lane_f == ei
        sel = jnp.where(pick, 1.0, sel)
        cand = jnp.where(pick, ninf, cand)
    picked = jnp.where(sel > 0.0, scores, 0.0)
    gate = picked / jnp.sum(picked, axis=1, keepdims=True) * ROUTED_SCALE
    return jnp.where(lane == N_EXPERTS, 1.0, gate)


def _out_kernel(att_ref, rec_ref, x_ref, ga_ref, gr_ref, wa_ref, wr_ref, g1_ref, b1_ref,
                rw_ref, rb_ref, x1_ref, gate_ref, cnt_ref):
    na = _rms_norm(att_ref[...], ga_ref[...]).astype(BF16)
    nr = _rms_norm(rec_ref[...], gr_ref[...]).astype(BF16)
    mix = (jnp.dot(na, wa_ref[...], preferred_element_type=F32)
           + jnp.dot(nr, wr_ref[...], preferred_element_type=F32))
    x1 = _layer_norm(ALPHA * x_ref[...] + mix, g1_ref[...], b1_ref[...])
    x1_ref[...] = x1
    logits = jnp.dot(x1.astype(BF16), rw_ref[...], preferred_element_type=F32)
    gate = _route(logits, rb_ref[...])
    gate_ref[...] = gate
    lane = lax.broadcasted_iota(jnp.int32, gate.shape, 1)
    routed = (gate > 0.0) & (lane < N_EXPERTS)
    hit = jnp.where(routed, 1.0, 0.0)
    for j in range(cnt_ref.shape[0]):
        cnt_ref[j] = jnp.sum(hit[j * TT:(j + 1) * TT, :], axis=0, keepdims=True)


def _out_proj(att, rec, x, ga, gr, wa, wr, g1, b1, rw, rb, tm):
    n = x.shape[0]
    rep = lambda shape: pl.BlockSpec(shape, lambda i: (0,) * len(shape))
    return pl.pallas_call(
        _out_kernel,
        grid=(n // tm,),
        in_specs=[
            pl.BlockSpec((tm, ATT_W), lambda i: (i, 0)),
            pl.BlockSpec((tm, REC_W), lambda i: (i, 0)),
            pl.BlockSpec((tm, D_MODEL), lambda i: (i, 0)),
            rep((1, ATT_W)), rep((1, REC_W)),
            rep((ATT_W, D_MODEL)), rep((REC_W, D_MODEL)),
            rep((1, D_MODEL)), rep((1, D_MODEL)),
            rep((D_MODEL, LANES)), rep((1, LANES)),
        ],
        out_specs=[pl.BlockSpec((tm, D_MODEL), lambda i: (i, 0)),
                   pl.BlockSpec((tm, LANES), lambda i: (i, 0)),
                   pl.BlockSpec((tm // TT, 1, LANES), lambda i: (i, 0, 0))],
        out_shape=[jax.ShapeDtypeStruct((n, D_MODEL), F32),
                   jax.ShapeDtypeStruct((n, LANES), F32),
                   jax.ShapeDtypeStruct((n // TT, 1, LANES), F32)],
        compiler_params=_cparams(("parallel",), VMEM_LIMIT),
        name="out_proj_router",
    )(att, rec, x, ga, gr, wa, wr, g1, b1, rw, rb)


def _moe_plan(cnt):
    c = (cnt + ROW_Q - 1) // ROW_Q * ROW_Q
    lo = jnp.cumsum(c, axis=1) - c
    tot = jnp.sum(c, axis=0)
    cap = (tot + TG - 1) // TG * TG
    ends = jnp.cumsum(cap)
    base = ends - cap
    go = base[None, :] + jnp.cumsum(c, axis=0) - c
    nused = (ends[-1] // TG).astype(jnp.int32).reshape(1)
    nt = N_TILES
    nt3 = P_ROWS // TG
    starts = jnp.arange(nt3, dtype=jnp.int32) * TG
    te = jnp.minimum(jnp.sum((ends[None, :] <= starts[:, None]).astype(jnp.int32), axis=1), N_EXPERTS - 1)
    rows = jnp.sum(c, axis=1).astype(jnp.int32)
    tail = (cap - tot).astype(jnp.int32)
    tdst = (base + tot).astype(jnp.int32)
    pad = lambda a: jnp.pad(a.astype(F32), ((0, 0), (0, LANES - N_EXPERTS)))
    lo_col = pad(lo).reshape(nt, LANES, 1)
    lo_row = pad(lo).reshape(nt, 1, LANES)
    c_row = pad(c).reshape(nt, 1, LANES)
    return dict(c=c.astype(jnp.int32), lo=lo.astype(jnp.int32), go=go.astype(jnp.int32), rows=rows, tail=tail,
                tdst=tdst, te=te, nused=nused, lo_col=lo_col, lo_row=lo_row, c_row=c_row)


def _tile_plan(t, gate, lo_col, lo_s, c_s, dest_sc, p_sc):
    gt = gate.T
    sub = lax.broadcasted_iota(jnp.int32, (LANES, TT), 0)
    sel = (gt > 0.0) & (sub < N_EXPERTS)
    r = lax.broadcasted_iota(jnp.int32, (TT, TT), 0)
    c = lax.broadcasted_iota(jnp.int32, (TT, TT), 1)
    before = jnp.where(r < c, 1.0, 0.0).astype(BF16)
    rank = jnp.dot(jnp.where(sel, 1.0, 0.0).astype(BF16), before, preferred_element_type=F32)
    dest_sc[...] = jnp.where(sel, lo_col + rank, -1.0)
    p_sc[...] = jnp.zeros(p_sc.shape, p_sc.dtype)
    short = SEG_MAX // 4

    def window(e, carry):
        lo_e = pl.multiple_of(lo_s[t, e], ROW_Q)
        d = dest_sc[pl.ds(e, 1), :]
        fits = c_s[t, e] <= short

        def put(n):
            rows = lax.broadcasted_iota(jnp.int32, (n, TT), 0).astype(F32)
            p_sc[pl.ds(lo_e, n), :] = jnp.where(rows + lo_e.astype(F32) == d, 1.0, 0.0).astype(p_sc.dtype)

        pl.when(fits)(lambda: put(short))
        pl.when(jnp.logical_not(fits))(lambda: put(SEG_MAX))
        return carry

    lax.fori_loop(0, N_EXPERTS, window, 0)


def _for_chunks(count, src, dst, fn, bits=CHUNK_BITS):
    small = 4 * ROW_Q

    def pieces(some_bits):
        for bit in some_bits:
            off = count & ~(2 * bit - 1)

            @pl.when((count & bit) != 0)
            def _(off=off, bit=bit):
                fn(src + off, dst + off, bit)

    @pl.when(count >= small)
    def _():
        pieces([b for b in bits if b >= small])

    pieces([b for b in bits if b < small])


def _dispatch_kernel(c_s, lo_s, go_s, rows_s, tail_s, tdst_s, x_ref, gate_ref, locol_ref, xg_ref, p_ref,
                     dest_sc, xs_sc, z_sc, sem, zsem):
    t = pl.program_id(0)
    last = t == pl.num_programs(0) - 1
    slot = t & 1

    def seg_copy(sl, src, dst, n):
        return pltpu.make_async_copy(xs_sc.at[sl, pl.ds(pl.multiple_of(src, ROW_Q), n), :],
                                     xg_ref.at[pl.ds(pl.multiple_of(dst, ROW_Q), n), :], sem.at[sl])

    def wait_tile(tile, sl):
        _for_chunks(rows_s[tile], 0, 0, lambda s, d, n: seg_copy(sl, s, d, n).wait(), ROWS_BITS)

    @pl.when(t >= 2)
    def _():
        wait_tile(t - 2, slot)

    p_sc = p_ref.at[0]
    _tile_plan(t, gate_ref[...], locol_ref[0], lo_s, c_s, dest_sc, p_sc)
    xs_sc[slot] = jnp.dot(p_sc[...], x_ref[...].astype(BF16), preferred_element_type=F32).astype(BF16)

    def start(e, carry):
        _for_chunks(c_s[t, e], lo_s[t, e], go_s[t, e], lambda s, d, n: seg_copy(slot, s, d, n).start())
        return carry

    lax.fori_loop(0, N_EXPERTS, start, 0)

    def zero_copy(src, dst, n):
        return pltpu.make_async_copy(z_sc.at[pl.ds(0, n), :],
                                     xg_ref.at[pl.ds(pl.multiple_of(dst, ROW_Q), n), :], zsem)

    @pl.when(last)
    def _():
        z_sc[...] = jnp.zeros(z_sc.shape, z_sc.dtype)

        def zstart(e, carry):
            _for_chunks(tail_s[e], 0, tdst_s[e], lambda s, d, n: zero_copy(s, d, n).start())
            return carry

        def zwait(e, carry):
            _for_chunks(tail_s[e], 0, tdst_s[e], lambda s, d, n: zero_copy(s, d, n).wait())
            return carry

        lax.fori_loop(0, N_EXPERTS, zstart, 0)

        @pl.when(t >= 1)
        def _():
            wait_tile(t - 1, 1 - slot)

        wait_tile(t, slot)
        lax.fori_loop(0, N_EXPERTS, zwait, 0)


def _moe_dispatch(pl_, x1, gate):
    nt, pmax = N_TILES, P_ROWS
    grid_spec = pltpu.PrefetchScalarGridSpec(
        num_scalar_prefetch=6,
        grid=(nt,),
        in_specs=[
            pl.BlockSpec((TT, D_MODEL), lambda t, *_: (t, 0)),
            pl.BlockSpec((TT, LANES), lambda t, *_: (t, 0)),
            pl.BlockSpec((1, LANES, 1), lambda t, *_: (t, 0, 0)),
        ],
        out_specs=[pl.BlockSpec(memory_space=pl.ANY),
                   pl.BlockSpec((1, LCP, TT), lambda t, *_: (t, 0, 0))],
        scratch_shapes=[
            pltpu.VMEM((LANES, TT), F32),
            pltpu.VMEM((2, LCP, D_MODEL), BF16),
            pltpu.VMEM((CHUNK_BITS[0], D_MODEL), BF16),
            pltpu.SemaphoreType.DMA((2,)),
            pltpu.SemaphoreType.DMA,
        ],
    )
    return pl.pallas_call(
        _dispatch_kernel,
        grid_spec=grid_spec,
        out_shape=[jax.ShapeDtypeStruct((pmax, D_MODEL), BF16),
                   jax.ShapeDtypeStruct((nt, LCP, TT), BF16)],
        compiler_params=_cparams(("arbitrary",), VMEM_LIMIT),
        name="moe_dispatch",
    )(pl_["c"], pl_["lo"], pl_["go"], pl_["rows"], pl_["tail"], pl_["tdst"], x1, gate, pl_["lo_col"])


def _grouped_kernel(te_s, nused_s, xg_ref, w1_ref, w3_ref, w2_ref, yg_ref, *, layer):
    def body(x_ref, w1b_ref, w3b_ref, w2b_ref, y_ref):
        x = x_ref[...]
        h1 = jnp.dot(x, w1b_ref[0, 0].astype(BF16), preferred_element_type=F32)
        h3 = jnp.dot(x, w3b_ref[0, 0].astype(BF16), preferred_element_type=F32)
        h = (h1 * jax.nn.sigmoid(h1)) * h3
        y_ref[...] = jnp.dot(h.astype(BF16), w2b_ref[0, 0].astype(BF16), preferred_element_type=F32)

    row = lambda i: (i, 0)
    wmap = lambda i: (layer, te_s[i], 0, 0)
    pltpu.emit_pipeline(
        body,
        grid=(nused_s[0],),
        in_specs=[
            pl.BlockSpec((TG, D_MODEL), row, pipeline_mode=pl.Buffered(3)),
            pl.BlockSpec((1, 1, D_MODEL, D_EXPERT), wmap),
            pl.BlockSpec((1, 1, D_MODEL, D_EXPERT), wmap),
            pl.BlockSpec((1, 1, D_EXPERT, D_MODEL), wmap),
        ],
        out_specs=[pl.BlockSpec((TG, D_MODEL), row)],
    )(xg_ref, w1_ref, w3_ref, w2_ref, yg_ref)


def _moe_grouped(layer, pl_, xg, w1, w3, w2):
    pmax = xg.shape[0]
    smem = pl.BlockSpec(memory_space=pltpu.SMEM)
    hbm = pl.BlockSpec(memory_space=pl.ANY)
    return pl.pallas_call(
        functools.partial(_grouped_kernel, layer=layer),
        in_specs=[smem, smem, hbm, hbm, hbm, hbm],
        out_specs=hbm,
        out_shape=jax.ShapeDtypeStruct((pmax, D_MODEL), F32),
        compiler_params=pltpu.CompilerParams(vmem_limit_bytes=VMEM_LIMIT),
        name="moe_grouped",
    )(pl_["te"], pl_["nused"], xg, w1, w3, w2)


def _combine_kernel(c_s, lo_s, go_s, rows_s, x_ref, gate_ref, p_ref, lorow_ref, crow_ref, yg_ref,
                    ws1_ref, ws3_ref, ws2_ref, g2_ref, b2_ref, o_ref, ys_sc, acc_sc, sem):
    t = pl.program_id(0)
    slot = t & 1

    def seg_copy(sl, src, dst, n):
        return pltpu.make_async_copy(yg_ref.at[pl.ds(pl.multiple_of(src, ROW_Q), n), :],
                                     ys_sc.at[sl, pl.ds(pl.multiple_of(dst, ROW_Q), n), :], sem.at[sl])

    def start_tile(tile, sl):
        def start(e, carry):
            _for_chunks(c_s[tile, e], go_s[tile, e], lo_s[tile, e], lambda s, d, n: seg_copy(sl, s, d, n).start())
            return carry

        lax.fori_loop(0, N_EXPERTS, start, 0)

    @pl.when(t == 0)
    def _():
        ys_sc[...] = jnp.zeros(ys_sc.shape, ys_sc.dtype)
        start_tile(0, 0)

    @pl.when(t + 1 < pl.num_programs(0))
    def _():
        start_tile(t + 1, 1 - slot)

    x = x_ref[...]
    xb = x.astype(BF16)
    s1 = jnp.dot(xb, ws1_ref[...], preferred_element_type=F32)
    s3 = jnp.dot(xb, ws3_ref[...], preferred_element_type=F32)
    shared = jnp.dot(((s1 * jax.nn.sigmoid(s1)) * s3).astype(BF16), ws2_ref[...], preferred_element_type=F32)
    g3 = jnp.concatenate(_split3(gate_ref[...]), axis=1)

    _for_chunks(rows_s[t], 0, 0, lambda s, d, n: seg_copy(slot, s, d, n).wait(), ROWS_BITS)

    lo_row = lorow_ref[0]
    c_row = crow_ref[0]
    ck = 256

    def chunk(k0):
        p = p_ref[0, k0:k0 + ck, :]
        pg3 = jnp.dot(p, g3, preferred_element_type=F32)
        pg = (pg3[:, :LANES] + pg3[:, LANES:2 * LANES]) + pg3[:, 2 * LANES:]
        rowi = (lax.broadcasted_iota(jnp.int32, (ck, LANES), 0) + k0).astype(F32)
        mine = (rowi >= lo_row) & (rowi < lo_row + c_row)
        gs = jnp.sum(jnp.where(mine, pg, 0.0), axis=1, keepdims=True)
        inside = jnp.sum(jnp.where(mine, 1.0, 0.0), axis=1, keepdims=True) > 0.0
        y = jnp.where(inside, ys_sc[slot, k0:k0 + ck, :] * gs, 0.0)
        yh = y.astype(BF16)
        yl = (y - yh.astype(F32)).astype(BF16)
        tn = (((0,), (0,)), ((), ()))
        both = lax.dot_general(p, jnp.concatenate([yh, yl], axis=1), tn, preferred_element_type=F32)
        return both[:, :D_MODEL] + both[:, D_MODEL:]

    always = TT * TOP_K
    acc = chunk(0)
    for k0 in range(ck, always, ck):
        acc = acc + chunk(k0)
    acc_sc[...] = acc
    for k0 in range(always, LCP, ck):
        @pl.when(k0 < rows_s[t])
        def _(k0=k0):
            acc_sc[...] += chunk(k0)
    o_ref[...] = _layer_norm(ALPHA * x + (acc_sc[...] + shared), g2_ref[...], b2_ref[...])


def _moe_combine(pl_, x1, gate, pmat, yg, ws1, ws3, ws2, g2, b2):
    nt = N_TILES
    rep = lambda shape: pl.BlockSpec(shape, lambda t, *_: (0,) * len(shape))
    grid_spec = pltpu.PrefetchScalarGridSpec(
        num_scalar_prefetch=4,
        grid=(nt,),
        in_specs=[
            pl.BlockSpec((TT, D_MODEL), lambda t, *_: (t, 0)),
            pl.BlockSpec((TT, LANES), lambda t, *_: (t, 0)),
            pl.BlockSpec((1, LCP, TT), lambda t, *_: (t, 0, 0)),
            pl.BlockSpec((1, 1, LANES), lambda t, *_: (t, 0, 0)),
            pl.BlockSpec((1, 1, LANES), lambda t, *_: (t, 0, 0)),
            pl.BlockSpec(memory_space=pl.ANY),
            rep((D_MODEL, D_EXPERT)), rep((D_MODEL, D_EXPERT)), rep((D_EXPERT, D_MODEL)),
            rep((1, D_MODEL)), rep((1, D_MODEL)),
        ],
        out_specs=pl.BlockSpec((TT, D_MODEL), lambda t, *_: (t, 0)),
        scratch_shapes=[
            pltpu.VMEM((2, LCP, D_MODEL), F32),
            pltpu.VMEM((TT, D_MODEL), F32),
            pltpu.SemaphoreType.DMA((2,)),
        ],
    )
    return pl.pallas_call(
        _combine_kernel,
        grid_spec=grid_spec,
        out_shape=jax.ShapeDtypeStruct((nt * TT, D_MODEL), F32),
        compiler_params=_cparams(("arbitrary",), VMEM_LIMIT),
        name="moe_combine",
    )(pl_["c"], pl_["lo"], pl_["go"], pl_["rows"], x1, gate, pmat, pl_["lo_row"], pl_["c_row"], yg,
      ws1, ws3, ws2, g2, b2)


def _block_diag(w):
    nb, k, j = w.shape
    eye = jnp.eye(nb, dtype=w.dtype)
    return jnp.einsum('nkj,nm->nkmj', w, eye).reshape(nb * k, nb * j)


def kernel(x_prompt, x_sample, cache_k, cache_v, cache_logf, state_h, state_conv, page_table, w_in, b_f, conv_w, conv_b, w_gr, b_gr, w_gi, b_gi, lam, g_att, g_rec, w_out, ln1_g, ln1_b, w_router, router_bias, w1, w3, w2, ws1, ws3, ws2, ln2_g, ln2_b):
    n_pool = cache_k.shape[1]
    x = jnp.concatenate([x_prompt.reshape(N_PROMPT, D_MODEL), x_sample.reshape(N_SAMPLE, D_MODEL)], axis=0)
    ckt = jnp.transpose(cache_k, (0, 1, 3, 4, 2)).reshape(DEPTH, n_pool, ATT_W, PAGE_SIZE)
    cvt = jnp.transpose(cache_v, (0, 1, 3, 4, 2)).reshape(DEPTH, n_pool, ATT_W, PAGE_SIZE)
    lf_t = jnp.swapaxes(cache_logf, -1, -2).reshape(DEPTH * n_pool * H_ATT, PAGE_SIZE)
    wt = _cumsum_lanes(lf_t, 4096).reshape(DEPTH, n_pool, H_ATT, PAGE_SIZE)
    zero_conv = jnp.zeros((BATCH, CONV_W - 1, REC_W), F32)
    zero_h = jnp.zeros((BATCH, 1, REC_W), F32)
    cf = 3 * ATT_W

    outs = {k: [] for k in ("kp", "vp", "fp", "hp", "cp", "ks", "vs", "fs", "hs", "cs")}
    for l in range(DEPTH):
        w_main = jnp.concatenate([w_in[l][:, :cf], w_in[l][:, cf + H_ATT:]], axis=1).astype(BF16)
        w_f = jnp.pad(w_in[l][:, cf:cf + H_ATT], ((0, 0), (0, LANES - H_ATT))).astype(BF16)
        bf_pad = jnp.pad(b_f[l], (0, LANES - H_ATT)).reshape(1, LANES)
        u = _matmul(x, w_main, TM_PROJ, TN_PROJ)
        flog = _forget_logits(x, w_f, bf_pad, TM_PROJ)[:, :H_ATT]
        logf_p = flog[:N_PROMPT].reshape(BATCH, SEQ, H_ATT)
        logf_s = flog[N_PROMPT:].reshape(DEC_BATCH, DEC_SEQ, H_ATT)

        ft = _cumsum_lanes(jnp.swapaxes(logf_p, 1, 2).reshape(BATCH * H_ATT, SEQ), BATCH * H_ATT)
        fr = ft.reshape(BATCH, H_ATT // 2, 2, SEQ)
        fc = jnp.swapaxes(fr, 2, 3)
        att_p = _attn_prompt(u, fc, fr)

        ls_t = jnp.pad(jnp.swapaxes(logf_s, 1, 2), ((0, 0), (0, 0), (0, LANES - DEC_SEQ)))
        fn = _cumsum_lanes(ls_t.reshape(DEC_BATCH * H_ATT, LANES), DEC_BATCH * H_ATT)
        fn = fn.reshape(DEC_BATCH, H_ATT, LANES)
        fq = fn[:, :, :DEC_SEQ].reshape(DEC_BATCH, H_ATT * DEC_SEQ, 1)
        att_s = _attn_sample(l, page_table, u, ckt, cvt, wt, fq, fn)

        wg = jnp.concatenate([_block_diag(w_gr[l]), _block_diag(w_gi[l])], axis=1).astype(BF16)
        bg = jnp.concatenate([b_gr[l], b_gi[l]]).reshape(1, 2 * REC_W)
        rec_args = (conv_w[l], conv_b[l].reshape(1, REC_W), wg, bg, lam[l].reshape(1, REC_W))
        rec_p, h_p, conv_p = _rglru(u, zero_conv, zero_h, *rec_args,
                                    nseq=BATCH, slen=SEQ, tc=TC_REC, row0=0)
        rec_s, h_s, conv_s = _rglru(u, state_conv[l], state_h[l].reshape(DEC_BATCH, 1, REC_W), *rec_args,
                                    nseq=DEC_BATCH, slen=DEC_SEQ, tc=DEC_SEQ, row0=N_PROMPT)

        att = jnp.concatenate([att_p, att_s], axis=0)
        rec = jnp.concatenate([rec_p, rec_s], axis=0)
        rw = jnp.pad(w_router[l], ((0, 0), (0, LANES - N_EXPERTS))).astype(BF16)
        rb = jnp.pad(router_bias[l], (0, LANES - N_EXPERTS)).reshape(1, LANES)
        x1, gate, cnt = _out_proj(att, rec, x, g_att[l].reshape(1, ATT_W), g_rec[l].reshape(1, REC_W),
                             w_out[l][:ATT_W].astype(BF16), w_out[l][ATT_W:].astype(BF16),
                             ln1_g[l].reshape(1, D_MODEL), ln1_b[l].reshape(1, D_MODEL),
                             rw, rb, TM_OUT)

        mp = _moe_plan(cnt[:, 0, :N_EXPERTS].astype(jnp.int32))
        xg, pmat = _moe_dispatch(mp, x1, gate)
        yg = _moe_grouped(l, mp, xg, w1, w3, w2)
        x = _moe_combine(mp, x1, gate, pmat, yg, ws1[l].astype(BF16), ws3[l].astype(BF16), ws2[l].astype(BF16),
                         ln2_g[l].reshape(1, D_MODEL), ln2_b[l].reshape(1, D_MODEL))

        kv = lambda rows, c0, shape: u[rows, c0:c0 + ATT_W].reshape(shape)
        p_rows, s_rows = slice(0, N_PROMPT), slice(N_PROMPT, N_TOK)
        p_shape = (BATCH, SEQ, H_ATT, HEAD_DIM)
        s_shape = (DEC_BATCH, DEC_SEQ, H_ATT, HEAD_DIM)
        outs["kp"].append(kv(p_rows, ATT_W, p_shape))
        outs["vp"].append(kv(p_rows, 2 * ATT_W, p_shape))
        outs["fp"].append(logf_p)
        outs["hp"].append(h_p.reshape(BATCH, REC_W))
        outs["cp"].append(conv_p)
        outs["ks"].append(kv(s_rows, ATT_W, s_shape))
        outs["vs"].append(kv(s_rows, 2 * ATT_W, s_shape))
        outs["fs"].append(logf_s)
        outs["hs"].append(h_s.reshape(DEC_BATCH, REC_W))
        outs["cs"].append(conv_s)

    st = lambda name: jnp.stack(outs[name])
    return (x[:N_PROMPT].reshape(BATCH, SEQ, D_MODEL), x[N_PROMPT:].reshape(DEC_BATCH, DEC_SEQ, D_MODEL),
            st("kp"), st("vp"), st("fp"), st("hp"), st("cp"),
            st("ks"), st("vs"), st("fs"), st("hs"), st("cs"))
```
